```python
import jax, jax.numpy as jnp
from jax import lax
import numpy as np

D_MODEL = 1024
BATCH = 32
SEQ = 256
DEPTH = 4
DEC_BATCH = 2
DEC_SEQ = 1024
PAST_LEN = 256

GRID_W = 64
HEAD_DIM = 64
GROUP_W = D_MODEL // 4
FOURIER_HEADS = GROUP_W // HEAD_DIM
NA_HEADS = GROUP_W // HEAD_DIM
GQA_Q_HEADS = GROUP_W // HEAD_DIM
GQA_KV_HEADS = GQA_Q_HEADS // 2
GQA_GROUP = GQA_Q_HEADS // GQA_KV_HEADS
POOL_GROUPS = 4
POOL_GW = GROUP_W // POOL_GROUPS
POOL_WINDOWS = (2, 4, 8, 16)
NA_KH_MAX = 8
NA_KW = 16
N_EXPERTS = 32
TOP_K = 4
D_FF = D_MODEL
SWIGLU_LIMIT = 7.0
SWIGLU_ALPHA = 1.702
ROPE_THETA = 10000.0
EPS = 1e-6
Q_BLOCK = 128
MOE_BLOCK = 128
NEG_INF = -1e30
KV_W = GQA_KV_HEADS * HEAD_DIM
PROJ_W = GROUP_W + 3 * GROUP_W + GROUP_W + 2 * KV_W + GROUP_W

kernel_name = 'hybrid_fourier_na_gqa_pool_moe_diffusion_step'


def rmsnorm(x, g):
    xf = x.astype(jnp.float32)
    y = xf * lax.rsqrt(jnp.mean(xf * xf, axis=-1, keepdims=True) + EPS)
    return (y * g.astype(jnp.float32)).astype(x.dtype)


def adaln(cond, w, b):
    m = jax.nn.silu(cond) @ w + b
    return jnp.split(m[:, None, :], 6, axis=-1)


def split_proj(p):
    sizes = (GROUP_W, GROUP_W, GROUP_W, GROUP_W, GROUP_W, KV_W, KV_W)
    idx, acc = [], 0
    for s in sizes:
        acc += s
        idx.append(acc)
    return jnp.split(p, idx, axis=-1)


def axial_rope_angles(n_tokens):
    t = jnp.arange(n_tokens)
    row = (t // GRID_W).astype(jnp.float32)
    col = (t % GRID_W).astype(jnp.float32)
    n_freq = HEAD_DIM // 4
    inv = ROPE_THETA ** (-jnp.arange(n_freq, dtype=jnp.float32) / n_freq)
    return row[:, None] * inv[None, :], col[:, None] * inv[None, :]


def rotate(x, ang):
    n = x.shape[-1] // 2
    x1, x2 = x[..., :n], x[..., n:]
    cos = jnp.cos(ang)[:, None, :]
    sin = jnp.sin(ang)[:, None, :]
    return jnp.concatenate([x1 * cos - x2 * sin, x2 * cos + x1 * sin], axis=-1)


def apply_axial_rope(x, ang_r, ang_c):
    half = HEAD_DIM // 2
    xf = x.astype(jnp.float32)
    out = jnp.concatenate([rotate(xf[..., :half], ang_r), rotate(xf[..., half:], ang_c)], axis=-1)
    return out.astype(x.dtype)


def block_attention(q, k, v):
    B, L, Hk, G, d = q.shape
    nb = L // Q_BLOCK
    qb = jnp.moveaxis(q.reshape(B, nb, Q_BLOCK, Hk, G, d), 1, 0)
    scale = d ** -0.5

    def one(qblk):
        s = jnp.einsum('bqhgd,bkhd->bhgqk', qblk, k, preferred_element_type=jnp.float32) * scale
        p = jax.nn.softmax(s, axis=-1)
        return jnp.einsum('bhgqk,bkhd->bqhgd', p.astype(v.dtype), v)

    out = lax.map(one, qb)
    return jnp.moveaxis(out, 0, 1).reshape(B, L, Hk, G, d)


def na_window_bias(rpb, rows, kh):
    r = jnp.arange(rows)
    r0 = jnp.clip(r - kh // 2, 0, rows - kh)
    row_idx = r0[:, None] + jnp.arange(kh)[None, :]
    dr = row_idx - r[:, None] + (NA_KH_MAX - 1)
    cq = jnp.arange(GRID_W)
    c0 = jnp.clip(cq - NA_KW // 2, 0, GRID_W - NA_KW)
    valid = (cq[None, :] >= c0[:, None]) & (cq[None, :] < c0[:, None] + NA_KW)
    dc = jnp.clip(cq[None, :] - cq[:, None] + (NA_KW - 1), 0, 2 * NA_KW - 2)
    b = rpb.astype(jnp.float32)[:, dr[:, None, :, None], dc[None, :, None, :]]
    b = jnp.where(valid[None, None, :, None, :], b, NEG_INF)
    return row_idx, b


def na_latent(q, k, v, k_ctx, v_ctx, rpb):
    B, L, H, d = q.shape
    rows = L // GRID_W
    kh = min(NA_KH_MAX, rows)
    row_idx, bias = na_window_bias(rpb, rows, kh)
    qg = q.reshape(B, rows, GRID_W, H, d)
    k_rows = k.reshape(B, rows, GRID_W, H, d)[:, row_idx]
    v_rows = v.reshape(B, rows, GRID_W, H, d)[:, row_idx]
    scale = d ** -0.5
    s_win = jnp.einsum('brqhd,brikhd->bhrqik', qg, k_rows, preferred_element_type=jnp.float32) * scale + bias[None]
    s_ctx = jnp.einsum('brqhd,bchd->bhrqc', qg, k_ctx, preferred_element_type=jnp.float32) * scale
    n_win = kh * GRID_W
    s = jnp.concatenate([s_win.reshape(B, H, rows, GRID_W, n_win), s_ctx], axis=-1)
    p = jax.nn.softmax(s, axis=-1).astype(v.dtype)
    p_win = p[..., :n_win].reshape(B, H, rows, GRID_W, kh, GRID_W)
    out = (jnp.einsum('bhrqik,brikhd->brqhd', p_win, v_rows)
           + jnp.einsum('bhrqc,bchd->brqhd', p[..., n_win:], v_ctx))
    return out.reshape(B, L, H, d)


def fourier_mix(u, w):
    B, L, _ = u.shape
    uf = u.astype(jnp.float32).reshape(B, L, FOURIER_HEADS, HEAD_DIM)
    f = jnp.fft.fft2(uf, axes=(1, 3), norm='ortho').real
    return f.reshape(B, L, GROUP_W).astype(u.dtype) @ w


def pool_mix(u, w_pool, scale):
    B, L, _ = u.shape
    ug = u.reshape(B, L, POOL_GROUPS, POOL_GW).astype(jnp.float32)
    cs = jnp.concatenate([jnp.zeros((B, 1, POOL_GROUPS, POOL_GW), jnp.float32), jnp.cumsum(ug, axis=1)], axis=1)
    t = jnp.arange(L)
    outs = []
    for gi, win in enumerate(POOL_WINDOWS):
        a = jnp.clip(t - win // 2, 0, L)
        b = jnp.clip(t + win // 2, 0, L)
        mean = (cs[:, b, gi] - cs[:, a, gi]) / (b - a).astype(jnp.float32)[None, :, None]
        outs.append(mean - ug[:, :, gi])
    dlt = jnp.stack(outs, axis=2).astype(u.dtype)
    y = jnp.einsum('blgc,gce->blge', dlt, w_pool).reshape(B, L, GROUP_W)
    return y * scale


def expert_ffn(xb, w_gu, b_gu, w_down, b_down):
    h = xb @ w_gu + b_gu
    gate = jnp.minimum(h[..., ::2], SWIGLU_LIMIT)
    up = jnp.clip(h[..., 1::2], -SWIGLU_LIMIT, SWIGLU_LIMIT)
    glu = gate * jax.nn.sigmoid(gate * SWIGLU_ALPHA)
    return ((up + 1.0) * glu) @ w_down + b_down


def moe(x, router_w, router_b, w_gu, b_gu, w_down, b_down):
    B, L, D = x.shape
    xt = x.reshape(-1, D)
    T = xt.shape[0]
    A = T * TOP_K
    logits = (xt @ router_w).astype(jnp.float32) + router_b.astype(jnp.float32)
    top_v, top_e = lax.top_k(logits, TOP_K)
    gates = jax.nn.softmax(top_v, axis=-1)
    flat_e = top_e.reshape(-1)
    order = jnp.argsort(flat_e)
    sorted_e = flat_e[order]
    tok = order // TOP_K
    counts = jnp.bincount(flat_e, length=N_EXPERTS)
    padded = (counts + MOE_BLOCK - 1) // MOE_BLOCK * MOE_BLOCK
    pend = jnp.cumsum(padded)
    pstart = pend - padded
    sstart = jnp.cumsum(counts) - counts
    dest = pstart[sorted_e] + jnp.arange(A) - sstart[sorted_e]
    n_blocks = -(-A // MOE_BLOCK) + N_EXPERTS
    xp = jnp.zeros((n_blocks * MOE_BLOCK, D), x.dtype).at[dest].set(xt[tok])
    block_e = jnp.minimum(jnp.searchsorted(pend, jnp.arange(n_blocks) * MOE_BLOCK, side='right'), N_EXPERTS - 1)

    def run(args):
        xb, e = args
        return expert_ffn(xb, w_gu[e], b_gu[e], w_down[e], b_down[e])

    yp = lax.map(run, (xp.reshape(n_blocks, MOE_BLOCK, D), block_e)).reshape(-1, D)
    wgt = gates.reshape(-1)[order].astype(x.dtype)
    y = jnp.zeros((T, D), x.dtype).at[tok].add(yp[dest] * wgt[:, None])
    return y.reshape(B, L, D)


def trunk_layer(x, mods, lp, ctx_kv, rope):
    B, L, _ = x.shape
    sh1, sc1, g1, sh2, sc2, g2 = mods
    h = rmsnorm(x, lp['norm1_g']) * (1.0 + sc1) + sh1
    u_f, na_q, na_k, na_v, g_q, g_k, g_v, u_p = split_proj(h @ lp['w_in'])
    heads = lambda t: t.reshape(B, L, -1, HEAD_DIM)
    na_q, na_k, na_v = heads(na_q), heads(na_k), heads(na_v)
    g_q = rmsnorm(heads(g_q), lp['q_norm_g'])
    g_k = rmsnorm(heads(g_k), lp['k_norm_g'])
    g_v = heads(g_v)
    if ctx_kv is None:
        o_na = block_attention(na_q[:, :, :, None, :], na_k, na_v)
        o_g = block_attention(g_q.reshape(B, L, GQA_KV_HEADS, GQA_GROUP, HEAD_DIM), g_k, g_v)
        new_ctx = (na_k, na_v, g_k, g_v)
    else:
        na_kc, na_vc, g_kc, g_vc = ctx_kv
        o_na = na_latent(na_q, na_k, na_v, na_kc, na_vc, lp['na_rpb'])
        ang_r, ang_c = rope
        g_q = apply_axial_rope(g_q, ang_r, ang_c)
        g_k = apply_axial_rope(g_k, ang_r, ang_c)
        o_g = block_attention(g_q.reshape(B, L, GQA_KV_HEADS, GQA_GROUP, HEAD_DIM),
                              jnp.concatenate([g_k, g_kc.astype(g_k.dtype)], axis=1),
                              jnp.concatenate([g_v, g_vc.astype(g_v.dtype)], axis=1))
        new_ctx = None
    mixed = jnp.concatenate([fourier_mix(u_f, lp['w_fourier']),
                             o_na.reshape(B, L, GROUP_W),
                             o_g.reshape(B, L, GROUP_W),
                             pool_mix(u_p, lp['w_pool'], lp['pool_scale'])], axis=-1)
    x = x + g1 * (mixed @ lp['w_out'])
    h2 = rmsnorm(x, lp['norm2_g']) * (1.0 + sc2) + sh2
    x = x + g2 * moe(h2, lp['router_w'], lp['router_b'], lp['w_gu'], lp['b_gu'], lp['w_down'], lp['b_down'])
    return x, new_ctx


def setup_inputs(seed: int = 0) -> dict:
    key = jax.random.key(seed)
    ks = jax.random.split(key, 26)
    f32 = jnp.float32
    nrm = lambda k, shape, s: jax.random.normal(k, shape, f32) * s
    return {
        'x_prompt': nrm(ks[0], (BATCH, SEQ, D_MODEL), 1.0),
        'x_sample': nrm(ks[1], (DEC_BATCH, DEC_SEQ, D_MODEL), 1.0),
        'cache_na_kv': nrm(ks[2], (DEC_BATCH, DEPTH, 2, PAST_LEN, NA_HEADS, HEAD_DIM), 1.0),
        'cache_gqa_kv': nrm(ks[3], (DEC_BATCH, DEPTH, 2, PAST_LEN, GQA_KV_HEADS, HEAD_DIM), 1.0),
        'c': nrm(ks[4], (DEC_BATCH, D_MODEL), 1.0),
        'c_ctx': nrm(ks[5], (D_MODEL,), 1.0),
        'norm1_g': 1.0 + nrm(ks[6], (DEPTH, D_MODEL), 0.05),
        'norm2_g': 1.0 + nrm(ks[7], (DEPTH, D_MODEL), 0.05),
        'w_mod': nrm(ks[8], (DEPTH, D_MODEL, 6 * D_MODEL), 0.5 * D_MODEL ** -0.5),
        'b_mod': nrm(ks[9], (DEPTH, 6 * D_MODEL), 0.02),
        'w_in': nrm(ks[10], (DEPTH, D_MODEL, PROJ_W), D_MODEL ** -0.5),
        'w_fourier': nrm(ks[11], (DEPTH, GROUP_W, GROUP_W), GROUP_W ** -0.5),
        'na_rpb': nrm(ks[12], (DEPTH, NA_HEADS, 2 * NA_KH_MAX - 1, 2 * NA_KW - 1), 0.1),
        'q_norm_g': 1.0 + nrm(ks[13], (DEPTH, HEAD_DIM), 0.05),
        'k_norm_g': 1.0 + nrm(ks[14], (DEPTH, HEAD_DIM), 0.05),
        'w_pool': nrm(ks[15], (DEPTH, POOL_GROUPS, POOL_GW, POOL_GW), POOL_GW ** -0.5),
        'pool_scale': 1.0 + nrm(ks[16], (DEPTH, GROUP_W), 0.05),
        'w_out': nrm(ks[17], (DEPTH, D_MODEL, D_MODEL), D_MODEL ** -0.5),
        'router_w': nrm(ks[18], (DEPTH, D_MODEL, N_EXPERTS), D_MODEL ** -0.5),
        'router_b': nrm(ks[19], (DEPTH, N_EXPERTS), 0.01),
        'w_gu': nrm(ks[20], (DEPTH, N_EXPERTS, D_MODEL, 2 * D_FF), D_MODEL ** -0.5),
        'b_gu': nrm(ks[21], (DEPTH, N_EXPERTS, 2 * D_FF), 0.01),
        'w_down': nrm(ks[22], (DEPTH, N_EXPERTS, D_FF, D_MODEL), D_FF ** -0.5),
        'b_down': nrm(ks[23], (DEPTH, N_EXPERTS, D_MODEL), 0.01),
        'final_g': 1.0 + nrm(ks[24], (D_MODEL,), 0.05),
    }


def reference(x_prompt, x_sample, cache_na_kv, cache_gqa_kv, c, c_ctx, norm1_g, norm2_g, w_mod, b_mod,
              w_in, w_fourier, na_rpb, q_norm_g, k_norm_g, w_pool, pool_scale, w_out, router_w, router_b,
              w_gu, b_gu, w_down, b_down, final_g):
    rope = axial_rope_angles(x_sample.shape[1])
    xp, xs = x_prompt, x_sample
    na_states, gqa_states = [], []
    for l in range(DEPTH):
        lp = {'norm1_g': norm1_g[l], 'norm2_g': norm2_g[l], 'w_in': w_in[l], 'w_fourier': w_fourier[l],
              'na_rpb': na_rpb[l], 'q_norm_g': q_norm_g[l], 'k_norm_g': k_norm_g[l], 'w_pool': w_pool[l],
              'pool_scale': pool_scale[l], 'w_out': w_out[l], 'router_w': router_w[l], 'router_b': router_b[l],
              'w_gu': w_gu[l], 'b_gu': b_gu[l], 'w_down': w_down[l], 'b_down': b_down[l]}
        mods_ctx = adaln(c_ctx[None, :], w_mod[l], b_mod[l])
        mods_lat = adaln(c, w_mod[l], b_mod[l])
        xp, (nk, nv, gk, gv) = trunk_layer(xp, mods_ctx, lp, None, None)
        ctx_kv = (cache_na_kv[:, l, 0], cache_na_kv[:, l, 1], cache_gqa_kv[:, l, 0], cache_gqa_kv[:, l, 1])
        xs, _ = trunk_layer(xs, mods_lat, lp, ctx_kv, rope)
        na_states.append(jnp.stack([nk, nv], axis=1))
        gqa_states.append(jnp.stack([gk, gv], axis=1))
    y_prompt = rmsnorm(xp, final_g)
    y_sample = rmsnorm(xs, final_g)
    new_na_kv = jnp.stack(na_states, axis=1)
    new_gqa_kv = jnp.stack(gqa_states, axis=1)
    return (y_prompt, y_sample, new_na_kv, new_gqa_kv)
```

```python
import functools

import numpy as np
import jax
import jax.numpy as jnp
from jax import lax
from jax.experimental import pallas as pl
from jax.experimental.pallas import tpu as pltpu

F32 = jnp.float32
BF16 = jnp.bfloat16
I32 = jnp.int32

D_MODEL = 1024
BATCH = 32
SEQ = 256
DEPTH = 4
DEC_BATCH = 2
DEC_SEQ = 1024
PAST_LEN = 256
GRID_W = 64
GRID_ROWS = DEC_SEQ // GRID_W
HEAD_DIM = 64
GROUP_W = 256
KV_W = 128
POOL_WINDOWS = (2, 4, 8, 16)
NA_KH = 8
NA_KW = 16
N_EXPERTS = 32
TOP_K = 4
D_FF = 1024
SWIGLU_LIMIT = 7.0
SWIGLU_ALPHA = 1.702
ROPE_THETA = 10000.0
EPS = 1e-6
NEG_INF = -1e30
PROJ_W = 1792

ROWS = 256
SUB = 8
LANES = 128
ROW_TILES = D_MODEL // LANES
T_P = BATCH * SEQ
T_S = DEC_BATCH * DEC_SEQ
T_ALL = T_P + T_S
MOE_ROWS = 256
N_ASSIGN = T_ALL * TOP_K
N_BLOCKS = N_ASSIGN // MOE_ROWS + N_EXPERTS
VMEM_LIMIT = 56 * 1024 * 1024


def _dft_tables(L):
    n = np.arange(L)
    ang = 2.0 * np.pi * ((n[:, None] * n[None, :]) % L) / L
    c64 = np.arange(HEAD_DIM)
    a64 = 2.0 * np.pi * ((c64[:, None] * c64[None, :]) % HEAD_DIM) / HEAD_DIM
    bc = np.kron(np.eye(GROUP_W // HEAD_DIM), np.cos(a64))
    bs = np.kron(np.eye(GROUP_W // HEAD_DIM), np.sin(a64))
    return (np.cos(ang).astype(np.float32), np.sin(ang).astype(np.float32),
            bc.astype(np.float32), bs.astype(np.float32))


def _pool_tables(L):
    t = np.arange(L)
    bands = np.zeros((len(POOL_WINDOWS), L, L), np.float32)
    icnt = np.zeros((L, GROUP_W), np.float64)
    for gi, win in enumerate(POOL_WINDOWS):
        a = np.clip(t - win // 2, 0, L)
        b = np.clip(t + win // 2, 0, L)
        n = t[None, :]
        bands[gi] = ((n >= a[:, None]) & (n < b[:, None])).astype(np.float32)
        icnt[:, gi * 64:(gi + 1) * 64] = (1.0 / (b - a))[:, None]
    return bands, icnt.astype(np.float32)


def _rope_tables():
    t = np.arange(DEC_SEQ)
    row = (t // GRID_W).astype(np.float64)
    col = (t % GRID_W).astype(np.float64)
    n_freq = HEAD_DIM // 4
    inv = ROPE_THETA ** (-np.arange(n_freq, dtype=np.float64) / n_freq)
    ar = row[:, None] * inv[None, :]
    ac = col[:, None] * inv[None, :]
    cos = np.concatenate([np.cos(ar), np.cos(ar), np.cos(ac), np.cos(ac)], axis=1)
    sin = np.concatenate([-np.sin(ar), np.sin(ar), -np.sin(ac), np.sin(ac)], axis=1)
    reps = GROUP_W // HEAD_DIM
    return (np.tile(cos, (1, reps)).astype(np.float32), np.tile(sin, (1, reps)).astype(np.float32))


def _group_mean_matrix(width):
    g = np.kron(np.eye(width // HEAD_DIM), np.ones((HEAD_DIM, HEAD_DIM)))
    return g.astype(np.float32)


def _dot(a, b):
    return jnp.dot(a, b, preferred_element_type=F32)


def _dot_nt(a, b):
    return lax.dot_general(a, b, (((1,), (1,)), ((), ())), preferred_element_type=F32)


def _split_bf16(x):
    hi = x.astype(BF16)
    lo = (x - hi.astype(F32)).astype(BF16)
    return hi, lo


def _rmsnorm_rows(x, g):
    ms = jnp.mean(x * x, axis=-1, keepdims=True)
    return x * lax.rsqrt(ms + EPS) * g


def _softmax_rows(s):
    m = jnp.max(s, axis=-1, keepdims=True)
    p = jnp.exp(s - m)
    return p, jnp.sum(p, axis=-1, keepdims=True)


MOD_COLS = 512


def _mods_body(cond_ref, w_ref, b_ref, o_ref):
    c = cond_ref[...]
    s = c * (1.0 / (1.0 + jnp.exp(-c)))
    s_hi, s_lo = _split_bf16(s)
    w_hi, w_lo = _split_bf16(w_ref[...])
    o_ref[...] = _dot(s_hi, w_hi) + _dot(s_lo, w_hi) + _dot(s_hi, w_lo) + b_ref[...]


def _mods_call(cond8, w_mod, b_mod3):
    n_col = 6 * D_MODEL // MOD_COLS
    return pl.pallas_call(
        _mods_body,
        grid=(DEPTH, n_col),
        in_specs=[
            pl.BlockSpec((SUB, D_MODEL), lambda l, j: (0, 0)),
            pl.BlockSpec((None, D_MODEL, MOD_COLS), lambda l, j: (l, 0, j)),
            pl.BlockSpec((None, 1, MOD_COLS), lambda l, j: (l, 0, j)),
        ],
        out_specs=pl.BlockSpec((None, SUB, MOD_COLS), lambda l, j: (l, 0, j)),
        out_shape=jax.ShapeDtypeStruct((DEPTH, SUB, 6 * D_MODEL), F32),
        name="adaln_mods",
    )(cond8, w_mod, b_mod3)


def _head_norm(t, g, gm):
    hi, lo = _split_bf16(t * t)
    msq = (_dot(hi, gm) + _dot(lo, gm)) * (1.0 / HEAD_DIM)
    return t * lax.rsqrt(msq + EPS) * g


def _rope(t, cos, sin):
    w = t.shape[1]
    lane = lax.broadcasted_iota(I32, t.shape, 1)
    first = (lane % 32) < 16
    swapped = jnp.where(first, pltpu.roll(t, w - 16, 1), pltpu.roll(t, 16, 1))
    return t * cos + swapped * sin


def _k1_body(use_rope, x_ref, mod_ref, g1_ref, win_ref, qg_ref, kg_ref, gm_ref, cos_ref, sin_ref,
             uf_ref, naq_ref, nakv_ref, gq_ref, gkv_ref, up_ref):
    sh1 = mod_ref[:, 0:D_MODEL]
    sc1 = mod_ref[:, D_MODEL:2 * D_MODEL]
    h = _rmsnorm_rows(x_ref[...], g1_ref[...]) * (1.0 + sc1) + sh1
    hb = h.astype(BF16)

    def proj(lo, hi):
        return _dot(hb, win_ref[:, lo:hi])

    uf_ref[...] = proj(0, 256).astype(BF16)
    naq_ref[...] = proj(256, 512).astype(BF16)
    nakv_ref[0] = proj(512, 768)
    nakv_ref[1] = proj(768, 1024)
    gq = _head_norm(proj(1024, 1280), qg_ref[...], gm_ref[...])
    gk = _head_norm(proj(1280, 1408), kg_ref[:, 0:KV_W], gm_ref[0:KV_W, 0:KV_W])
    if use_rope:
        gq = _rope(gq, cos_ref[...], sin_ref[...])
        gk = _rope(gk, cos_ref[:, 0:KV_W], sin_ref[:, 0:KV_W])
    gq_ref[...] = gq.astype(BF16)
    gkv_ref[0] = gk
    gkv_ref[1] = proj(1408, 1536)
    up_ref[...] = proj(1536, 1792)


def _k1_call(x, mods_l, mod_row_fn, g1, w_in_bf, qg, kg, gm, cos, sin, use_rope, name):
    n_tok = x.shape[0]
    nb = n_tok // ROWS
    blocks_per_seq = DEC_SEQ // ROWS
    full = lambda shape: pl.BlockSpec(shape, lambda i: (0,) * len(shape))
    return pl.pallas_call(
        functools.partial(_k1_body, use_rope),
        grid=(nb,),
        in_specs=[
            pl.BlockSpec((ROWS, D_MODEL), lambda i: (i, 0)),
            pl.BlockSpec((None, 1, 6 * D_MODEL), lambda i: (mod_row_fn(i), 0, 0)),
            full((1, D_MODEL)),
            full((D_MODEL, PROJ_W)),
            full((1, GROUP_W)),
            full((1, GROUP_W)),
            full((GROUP_W, GROUP_W)),
            pl.BlockSpec((ROWS, GROUP_W), lambda i: (i % blocks_per_seq, 0)),
            pl.BlockSpec((ROWS, GROUP_W), lambda i: (i % blocks_per_seq, 0)),
        ],
        out_specs=[
            pl.BlockSpec((ROWS, GROUP_W), lambda i: (i, 0)),
            pl.BlockSpec((ROWS, GROUP_W), lambda i: (i, 0)),
            pl.BlockSpec((None, 2, ROWS, GROUP_W), lambda i: (i, 0, 0, 0)),
            pl.BlockSpec((ROWS, GROUP_W), lambda i: (i, 0)),
            pl.BlockSpec((None, 2, ROWS, KV_W), lambda i: (i, 0, 0, 0)),
            pl.BlockSpec((ROWS, GROUP_W), lambda i: (i, 0)),
        ],
        out_shape=[
            jax.ShapeDtypeStruct((n_tok, GROUP_W), BF16),
            jax.ShapeDtypeStruct((n_tok, GROUP_W), BF16),
            jax.ShapeDtypeStruct((nb, 2, ROWS, GROUP_W), F32),
            jax.ShapeDtypeStruct((n_tok, GROUP_W), BF16),
            jax.ShapeDtypeStruct((nb, 2, ROWS, KV_W), F32),
            jax.ShapeDtypeStruct((n_tok, GROUP_W), F32),
        ],
        compiler_params=pltpu.CompilerParams(vmem_limit_bytes=VMEM_LIMIT),
        name=name,
    )(x, mods_l, g1, w_in_bf, qg, kg, gm, cos, sin)


def _linmix_body(seq, uf_ref, up_ref, bc_ref, bs_ref, cl_ref, sl_ref, wf_ref, band_ref, icnt_ref,
                 wp_ref, ps_ref, a_ref, d_ref):
    u = uf_ref[...]
    t1 = _dot(u, bc_ref[...]).astype(BF16)
    t2 = _dot(u, bs_ref[...]).astype(BF16)
    f = (_dot(cl_ref[...], t1) - _dot(sl_ref[...], t2)) * (1.0 / float(np.sqrt(seq * HEAD_DIM)))
    a_ref[...] = _dot(f.astype(BF16), wf_ref[...]).astype(BF16)

    up = up_ref[...]
    up_hi, up_lo = _split_bf16(up)
    group = lax.broadcasted_iota(I32, up.shape, 1) // HEAD_DIM
    win_sum = jnp.zeros_like(up)
    for gi in range(len(POOL_WINDOWS)):
        s = _dot(band_ref[gi], up_hi) + _dot(band_ref[gi], up_lo)
        win_sum = jnp.where(group == gi, s, win_sum)
    delta = win_sum * icnt_ref[...] - up
    d_ref[...] = (_dot(delta.astype(BF16), wp_ref[...]) * ps_ref[...]).astype(BF16)


def _linmix_call(uf, up, seq, tabs, wf_bf, wp_bd_bf, ps, name):
    bc, bs, cl, sl, bands, icnt = tabs
    nb = uf.shape[0] // seq
    full = lambda shape: pl.BlockSpec(shape, lambda i: (0,) * len(shape))
    return pl.pallas_call(
        functools.partial(_linmix_body, seq),
        grid=(nb,),
        in_specs=[
            pl.BlockSpec((seq, GROUP_W), lambda i: (i, 0)),
            pl.BlockSpec((seq, GROUP_W), lambda i: (i, 0)),
            full((GROUP_W, GROUP_W)), full((GROUP_W, GROUP_W)),
            full((seq, seq)), full((seq, seq)),
            full((GROUP_W, GROUP_W)),
            full((len(POOL_WINDOWS), seq, seq)),
            full((seq, GROUP_W)),
            full((GROUP_W, GROUP_W)),
            full((1, GROUP_W)),
        ],
        out_specs=[pl.BlockSpec((seq, GROUP_W), lambda i: (i, 0))] * 2,
        out_shape=[jax.ShapeDtypeStruct((uf.shape[0], GROUP_W), BF16)] * 2,
        compiler_params=pltpu.CompilerParams(vmem_limit_bytes=VMEM_LIMIT),
        name=name,
    )(uf, up, bc, bs, cl, sl, wf_bf, bands, icnt, wp_bd_bf, ps)


def _attend(q, k, v):
    s = _dot_nt(q, k) * (HEAD_DIM ** -0.5)
    p, l = _softmax_rows(s)
    return _dot(p.astype(BF16), v) / l


def _attn_p_body(naq_ref, nakv_ref, gq_ref, gkv_ref, b_ref, c_ref):
    for h in range(GROUP_W // HEAD_DIM):
        sl = slice(h * HEAD_DIM, (h + 1) * HEAD_DIM)
        k = nakv_ref[0, :, sl].astype(BF16)
        v = nakv_ref[1, :, sl].astype(BF16)
        b_ref[:, sl] = _attend(naq_ref[:, sl], k, v).astype(BF16)
    for h in range(GROUP_W // HEAD_DIM):
        sl = slice(h * HEAD_DIM, (h + 1) * HEAD_DIM)
        kvh = h // 2
        ksl = slice(kvh * HEAD_DIM, (kvh + 1) * HEAD_DIM)
        k = gkv_ref[0, :, ksl].astype(BF16)
        v = gkv_ref[1, :, ksl].astype(BF16)
        c_ref[:, sl] = _attend(gq_ref[:, sl], k, v).astype(BF16)


def _attn_p_call(naq, nakv, gq, gkv):
    return pl.pallas_call(
        _attn_p_body,
        grid=(BATCH,),
        in_specs=[
            pl.BlockSpec((SEQ, GROUP_W), lambda i: (i, 0)),
            pl.BlockSpec((None, 2, SEQ, GROUP_W), lambda i: (i, 0, 0, 0)),
            pl.BlockSpec((SEQ, GROUP_W), lambda i: (i, 0)),
            pl.BlockSpec((None, 2, SEQ, KV_W), lambda i: (i, 0, 0, 0)),
        ],
        out_specs=[pl.BlockSpec((SEQ, GROUP_W), lambda i: (i, 0))] * 2,
        out_shape=[jax.ShapeDtypeStruct((T_P, GROUP_W), BF16)] * 2,
        name="attn_ctx",
    )(naq, nakv, gq, gkv)


N_DR = 2 * NA_KH - 1
N_DC = 2 * NA_KW - 1


def _nabias_body(rpb_ref, o_ref):
    lh = pl.program_id(0)
    q = lax.broadcasted_iota(I32, (GRID_W, LANES), 0)
    j = lax.broadcasted_iota(I32, (GRID_W, LANES), 1)
    k = j % GRID_W
    second = j >= GRID_W
    c0 = jnp.clip(q - NA_KW // 2, 0, GRID_W - NA_KW)
    valid = (k >= c0) & (k < c0 + NA_KW)
    diff = k - q + (NA_KW - 1)
    for dr in range(N_DR - 1):
        acc = jnp.zeros((GRID_W, LANES), F32)
        for dc in range(N_DC):
            val = jnp.where(second, rpb_ref[lh, (dr + 1) * N_DC + dc], rpb_ref[lh, dr * N_DC + dc])
            acc = jnp.where(diff == dc, val, acc)
        o_ref[dr] = jnp.where(valid, acc, NEG_INF)


def _nabias_call(rpb_flat):
    n = rpb_flat.shape[0]
    return pl.pallas_call(
        _nabias_body,
        grid=(n,),
        in_specs=[pl.BlockSpec(memory_space=pltpu.SMEM)],
        out_specs=pl.BlockSpec((None, N_DR - 1, GRID_W, LANES), lambda i: (i, 0, 0, 0)),
        out_shape=jax.ShapeDtypeStruct((n, N_DR - 1, GRID_W, LANES), F32),
        name="na_bias_tiles",
    )(rpb_flat)


N_WIN = NA_KH * GRID_W


def _na_s_body(q_ref, kv_ref, ckv_ref, bias_ref, o_ref):
    r = pl.program_id(1)
    r0 = jnp.clip(r - NA_KH // 2, 0, GRID_ROWS - NA_KH)
    start = pl.multiple_of(r0 * GRID_W, GRID_W)
    dr0 = r0 - r + (NA_KH - 1)
    scale = HEAD_DIM ** -0.5
    for h in range(GROUP_W // HEAD_DIM):
        sl = slice(h * HEAD_DIM, (h + 1) * HEAD_DIM)
        q = q_ref[:, sl]
        kw = kv_ref[0, pl.ds(start, N_WIN), sl].astype(BF16)
        vw = kv_ref[1, pl.ds(start, N_WIN), sl].astype(BF16)
        kc = ckv_ref[0, :, sl].astype(BF16)
        vc = ckv_ref[1, :, sl].astype(BF16)
        bias = jnp.concatenate([bias_ref[h, dr0 + i] for i in range(0, NA_KH, 2)], axis=1)
        s_win = _dot_nt(q, kw) * scale + bias
        s_ctx = _dot_nt(q, kc) * scale
        s = jnp.concatenate([s_win, s_ctx], axis=1)
        p, l = _softmax_rows(s)
        pb = p.astype(BF16)
        o = _dot(pb[:, 0:N_WIN], vw) + _dot(pb[:, N_WIN:], vc)
        o_ref[:, sl] = (o / l).astype(BF16)


def _na_s_call(naq, nakv, cache_na_l, bias_l):
    nakv_b = nakv.reshape(DEC_BATCH, DEC_SEQ // ROWS, 2, ROWS, GROUP_W)
    nakv_b = jnp.transpose(nakv_b, (0, 2, 1, 3, 4)).reshape(DEC_BATCH, 2, DEC_SEQ, GROUP_W)
    return pl.pallas_call(
        _na_s_body,
        grid=(DEC_BATCH, GRID_ROWS),
        in_specs=[
            pl.BlockSpec((GRID_W, GROUP_W), lambda b, r: (b * GRID_ROWS + r, 0)),
            pl.BlockSpec((None, 2, DEC_SEQ, GROUP_W), lambda b, r: (b, 0, 0, 0)),
            pl.BlockSpec((None, 2, PAST_LEN, GROUP_W), lambda b, r: (b, 0, 0, 0)),
            pl.BlockSpec((GROUP_W // HEAD_DIM, N_DR - 1, GRID_W, LANES), lambda b, r: (0, 0, 0, 0)),
        ],
        out_specs=pl.BlockSpec((GRID_W, GROUP_W), lambda b, r: (b * GRID_ROWS + r, 0)),
        out_shape=jax.ShapeDtypeStruct((T_S, GROUP_W), BF16),
        compiler_params=pltpu.CompilerParams(vmem_limit_bytes=VMEM_LIMIT),
        name="attn_na_latent",
    )(naq, nakv_b, cache_na_l, bias_l)


GQ_ROWS = 128


def _gqa_s_body(q_ref, kv_ref, ckv_ref, o_ref):
    scale = HEAD_DIM ** -0.5
    for h in range(GROUP_W // HEAD_DIM):
        sl = slice(h * HEAD_DIM, (h + 1) * HEAD_DIM)
        kvh = h // 2
        ksl = slice(kvh * HEAD_DIM, (kvh + 1) * HEAD_DIM)
        q = q_ref[:, sl]
        kl = kv_ref[0, :, ksl].astype(BF16)
        vl = kv_ref[1, :, ksl].astype(BF16)
        kc = ckv_ref[0, :, ksl].astype(BF16)
        vc = ckv_ref[1, :, ksl].astype(BF16)
        s = jnp.concatenate([_dot_nt(q, kl), _dot_nt(q, kc)], axis=1) * scale
        p, l = _softmax_rows(s)
        pb = p.astype(BF16)
        o = _dot(pb[:, 0:DEC_SEQ], vl) + _dot(pb[:, DEC_SEQ:], vc)
        o_ref[:, sl] = (o / l).astype(BF16)


def _gqa_s_call(gq, gkv, cache_gqa_l):
    gkv_b = gkv.reshape(DEC_BATCH, DEC_SEQ // ROWS, 2, ROWS, KV_W)
    gkv_b = jnp.transpose(gkv_b, (0, 2, 1, 3, 4)).reshape(DEC_BATCH, 2, DEC_SEQ, KV_W)
    nq = DEC_SEQ // GQ_ROWS
    return pl.pallas_call(
        _gqa_s_body,
        grid=(DEC_BATCH, nq),
        in_specs=[
            pl.BlockSpec((GQ_ROWS, GROUP_W), lambda b, j: (b * nq + j, 0)),
            pl.BlockSpec((None, 2, DEC_SEQ, KV_W), lambda b, j: (b, 0, 0, 0)),
            pl.BlockSpec((None, 2, PAST_LEN, KV_W), lambda b, j: (b, 0, 0, 0)),
        ],
        out_specs=pl.BlockSpec((GQ_ROWS, GROUP_W), lambda b, j: (b * nq + j, 0)),
        out_shape=jax.ShapeDtypeStruct((T_S, GROUP_W), BF16),
        compiler_params=pltpu.CompilerParams(vmem_limit_bytes=VMEM_LIMIT),
        name="attn_gqa_latent",
    )(gq, gkv_b, cache_gqa_l)


def _k3_body(x_ref, a_ref, b_ref, c_ref, d_ref, mod_ref, wout_ref, g2_ref, rwh_ref, rwl_ref, rb_ref,
             xmid_ref, h2_ref, te_ref, tg_ref):
    acc = _dot(a_ref[...], wout_ref[0:256, :])
    acc += _dot(b_ref[...], wout_ref[256:512, :])
    acc += _dot(c_ref[...], wout_ref[512:768, :])
    acc += _dot(d_ref[...], wout_ref[768:1024, :])
    gate1 = mod_ref[:, 2 * D_MODEL:3 * D_MODEL]
    sh2 = mod_ref[:, 3 * D_MODEL:4 * D_MODEL]
    sc2 = mod_ref[:, 4 * D_MODEL:5 * D_MODEL]
    xm = x_ref[...] + gate1 * acc
    xmid_ref[...] = xm
    h2 = _rmsnorm_rows(xm, g2_ref[...]) * (1.0 + sc2) + sh2
    for s in range(ROW_TILES):
        h2_ref[pl.ds(s, ROWS, stride=ROW_TILES), :] = h2[:, s * LANES:(s + 1) * LANES]
    hi, lo = _split_bf16(h2)
    logits = _dot(hi, rwh_ref[...]) + _dot(lo, rwh_ref[...]) + _dot(hi, rwl_ref[...]) + rb_ref[...]

    lane = lax.broadcasted_iota(I32, logits.shape, 1)
    cur = logits
    vals, idxs = [], []
    for _ in range(TOP_K):
        m = jnp.max(cur, axis=-1, keepdims=True)
        idx = jnp.min(jnp.where(cur == m, lane, LANES), axis=-1, keepdims=True)
        vals.append(m)
        idxs.append(idx)
        cur = jnp.where(lane == idx, -jnp.inf, cur)
    exps = [jnp.exp(v - vals[0]) for v in vals]
    tot = exps[0] + exps[1] + exps[2] + exps[3]
    te = jnp.zeros(logits.shape, I32)
    tg = jnp.zeros(logits.shape, F32)
    for k in range(TOP_K):
        te = jnp.where(lane == k, idxs[k], te)
        tg = jnp.where(lane == k, exps[k] / tot, tg)
    te_ref[...] = te
    tg_ref[...] = tg


def _k3_call(x, a, b, c, d, mods_l, mod_row_fn, w_out_bf, g2, rwh, rwl, rb, name):
    n_tok = x.shape[0]
    nb = n_tok // ROWS
    full = lambda shape: pl.BlockSpec(shape, lambda i: (0,) * len(shape))
    row256 = pl.BlockSpec((ROWS, GROUP_W), lambda i: (i, 0))
    return pl.pallas_call(
        _k3_body,
        grid=(nb,),
        in_specs=[
            pl.BlockSpec((ROWS, D_MODEL), lambda i: (i, 0)),
            row256, row256, row256, row256,
            pl.BlockSpec((None, 1, 6 * D_MODEL), lambda i: (mod_row_fn(i), 0, 0)),
            full((D_MODEL, D_MODEL)),
            full((1, D_MODEL)),
            full((D_MODEL, LANES)), full((D_MODEL, LANES)), full((1, LANES)),
        ],
        out_specs=[
            pl.BlockSpec((ROWS, D_MODEL), lambda i: (i, 0)),
            pl.BlockSpec((ROWS * ROW_TILES, LANES), lambda i: (i, 0)),
            pl.BlockSpec((ROWS, LANES), lambda i: (i, 0)),
            pl.BlockSpec((ROWS, LANES), lambda i: (i, 0)),
        ],
        out_shape=[
            jax.ShapeDtypeStruct((n_tok, D_MODEL), F32),
            jax.ShapeDtypeStruct((n_tok * ROW_TILES, LANES), F32),
            jax.ShapeDtypeStruct((n_tok, LANES), I32),
            jax.ShapeDtypeStruct((n_tok, LANES), F32),
        ],
        compiler_params=pltpu.CompilerParams(vmem_limit_bytes=VMEM_LIMIT),
        name=name,
    )(x, a, b, c, d, mods_l, w_out_bf, g2, rwh, rwl, rb)


def _row_gather_body(n_rows, idx_ref, tab_ref, o_ref, sem):
    def issue(r, carry):
        t = idx_ref[0, r]
        dst = o_ref.at[pl.ds(pl.multiple_of(r * SUB, SUB), SUB)]
        pltpu.make_async_copy(tab_ref.at[t], dst, sem).start()
        return carry

    lax.fori_loop(0, n_rows, issue, 0)

    def drain(r, carry):
        dst = o_ref.at[pl.ds(pl.multiple_of(r * SUB, SUB), SUB)]
        pltpu.make_async_copy(tab_ref.at[0], dst, sem).wait()
        return carry

    lax.fori_loop(0, n_rows, drain, 0)


def _dispatch_call(src_tok, h2_tiles):
    return pl.pallas_call(
        functools.partial(_row_gather_body, MOE_ROWS),
        grid=(N_BLOCKS,),
        in_specs=[
            pl.BlockSpec((None, 1, MOE_ROWS), lambda i: (i, 0, 0), memory_space=pltpu.SMEM),
            pl.BlockSpec(memory_space=pl.ANY),
        ],
        out_specs=pl.BlockSpec((MOE_ROWS * ROW_TILES, LANES), lambda i: (i, 0)),
        out_shape=jax.ShapeDtypeStruct((N_BLOCKS * MOE_ROWS * ROW_TILES, LANES), F32),
        scratch_shapes=[pltpu.SemaphoreType.DMA],
        name="moe_dispatch",
    )(src_tok, h2_tiles)


def _moe_body(be_ref, nu_ref, x_ref, wgu_ref, bgu_ref, wd_ref, bd_ref, o_ref, wd_scr):
    i = pl.program_id(0)

    @pl.when(i < nu_ref[0])
    def _():
        x = jnp.concatenate(
            [x_ref[pl.ds(s, MOE_ROWS, stride=ROW_TILES), :] for s in range(ROW_TILES)], axis=1)
        xb = x.astype(BF16)
        ha = _dot(xb, wgu_ref[:, 0:D_FF].astype(BF16)) + bgu_ref[:, 0:D_FF]
        hb = _dot(xb, wgu_ref[:, D_FF:2 * D_FF].astype(BF16)) + bgu_ref[:, D_FF:2 * D_FF]
        even = (lax.broadcasted_iota(I32, ha.shape, 1) % 2) == 0
        gate = jnp.where(even, ha, pltpu.roll(hb, 1, 1))
        up = jnp.where(even, pltpu.roll(ha, D_FF - 1, 1), hb)
        gate = jnp.minimum(gate, SWIGLU_LIMIT)
        up = jnp.clip(up, -SWIGLU_LIMIT, SWIGLU_LIMIT)
        glu = gate * (1.0 / (1.0 + jnp.exp(-gate * SWIGLU_ALPHA)))
        act = ((up + 1.0) * glu).astype(BF16)
        half = D_FF // 2
        for c in range(ROW_TILES):
            cs = slice(c * LANES, (c + 1) * LANES)
            wd_scr[c, pl.ds(0, half, stride=2), :] = wd_ref[0:half, cs]
            wd_scr[c, pl.ds(1, half, stride=2), :] = wd_ref[half:D_FF, cs]
        wd = jnp.concatenate([wd_scr[c].astype(BF16) for c in range(ROW_TILES)], axis=1)
        y = _dot(act, wd) + bd_ref[...]
        for s in range(ROW_TILES):
            o_ref[pl.ds(s, MOE_ROWS, stride=ROW_TILES), :] = y[:, s * LANES:(s + 1) * LANES]

    @pl.when(i >= nu_ref[0])
    def _():
        o_ref[...] = jnp.zeros(o_ref.shape, F32)


def _moe_call(layer, block_e, n_used, xs, w_gu, b_gu4, w_down, b_down4):
    def xmap(i, be, nu):
        return (jnp.minimum(i, nu[0] - 1), 0)

    grid_spec = pltpu.PrefetchScalarGridSpec(
        num_scalar_prefetch=2,
        grid=(N_BLOCKS,),
        in_specs=[
            pl.BlockSpec((MOE_ROWS * ROW_TILES, LANES), xmap),
            pl.BlockSpec((None, None, D_MODEL, 2 * D_FF), lambda i, be, nu: (layer, be[i], 0, 0)),
            pl.BlockSpec((None, None, 1, 2 * D_FF), lambda i, be, nu: (layer, be[i], 0, 0)),
            pl.BlockSpec((None, None, D_FF, D_MODEL), lambda i, be, nu: (layer, be[i], 0, 0)),
            pl.BlockSpec((None, None, 1, D_MODEL), lambda i, be, nu: (layer, be[i], 0, 0)),
        ],
        out_specs=pl.BlockSpec((MOE_ROWS * ROW_TILES, LANES), lambda i, be, nu: (i, 0)),
        scratch_shapes=[pltpu.VMEM((ROW_TILES, D_FF, LANES), F32)],
    )
    return pl.pallas_call(
        _moe_body,
        grid_spec=grid_spec,
        out_shape=jax.ShapeDtypeStruct((N_BLOCKS * MOE_ROWS * ROW_TILES, LANES), F32),
        compiler_params=pltpu.CompilerParams(vmem_limit_bytes=VMEM_LIMIT),
        name="moe_experts",
    )(block_e, n_used, xs, w_gu, b_gu4, w_down, b_down4)


def _combine_body(final, pos_ref, yp_ref, tg_ref, xmid_ref, mod_ref, fg_ref, o_ref, buf, sem):
    n = ROWS * TOP_K

    def issue(a, carry):
        p = pos_ref[0, a]
        dst = buf.at[pl.ds(pl.multiple_of(a * SUB, SUB), SUB)]
        pltpu.make_async_copy(yp_ref.at[p], dst, sem).start()
        return carry

    lax.fori_loop(0, n, issue, 0)

    def drain(a, carry):
        dst = buf.at[pl.ds(pl.multiple_of(a * SUB, SUB), SUB)]
        pltpu.make_async_copy(yp_ref.at[0], dst, sem).wait()
        return carry

    lax.fori_loop(0, n, drain, 0)

    gates = [tg_ref[:, k:k + 1] for k in range(TOP_K)]
    cols = []
    for s in range(ROW_TILES):
        acc = buf[pl.ds(s, ROWS, stride=TOP_K * ROW_TILES), :] * gates[0]
        for k in range(1, TOP_K):
            acc = acc + buf[pl.ds(k * ROW_TILES + s, ROWS, stride=TOP_K * ROW_TILES), :] * gates[k]
        cols.append(acc)
    moe_out = jnp.concatenate(cols, axis=1)
    gate2 = mod_ref[:, 5 * D_MODEL:6 * D_MODEL]
    x = xmid_ref[...] + gate2 * moe_out
    if final:
        x = _rmsnorm_rows(x, fg_ref[...])
    o_ref[...] = x


def _combine_call(pos, yp_tiles, tg, xmid, mods_l, mod_row_fn, final_g, final, name):
    n_tok = xmid.shape[0]
    nb = n_tok // ROWS
    return pl.pallas_call(
        functools.partial(_combine_body, final),
        grid=(nb,),
        in_specs=[
            pl.BlockSpec((None, 1, ROWS * TOP_K), lambda i: (i, 0, 0), memory_space=pltpu.SMEM),
            pl.BlockSpec(memory_space=pl.ANY),
            pl.BlockSpec((ROWS, LANES), lambda i: (i, 0)),
            pl.BlockSpec((ROWS, D_MODEL), lambda i: (i, 0)),
            pl.BlockSpec((None, 1, 6 * D_MODEL), lambda i: (mod_row_fn(i), 0, 0)),
            pl.BlockSpec((1, D_MODEL), lambda i: (0, 0)),
        ],
        out_specs=pl.BlockSpec((ROWS, D_MODEL), lambda i: (i, 0)),
        out_shape=jax.ShapeDtypeStruct((n_tok, D_MODEL), F32),
        scratch_shapes=[pltpu.VMEM((ROWS * TOP_K * ROW_TILES, LANES), F32), pltpu.SemaphoreType.DMA],
        compiler_params=pltpu.CompilerParams(vmem_limit_bytes=VMEM_LIMIT),
        name=name,
    )(pos, yp_tiles, tg, xmid, mods_l, final_g)


def _routing_tables(top_e):
    flat_e = top_e.reshape(-1)
    onehot = (flat_e[:, None] == jnp.arange(N_EXPERTS, dtype=I32)[None, :]).astype(I32)
    csum = jnp.cumsum(onehot, axis=0)
    counts = csum[-1]
    rank = jnp.take_along_axis(csum, flat_e[:, None], axis=1)[:, 0] - 1
    padded = (counts + MOE_ROWS - 1) // MOE_ROWS * MOE_ROWS
    pend = jnp.cumsum(padded)
    pstart = pend - padded
    dest = pstart[flat_e] + rank
    n_rows = N_BLOCKS * MOE_ROWS
    tok = jnp.arange(N_ASSIGN, dtype=I32) // TOP_K
    src_tok = jnp.zeros((n_rows,), I32).at[dest].set(tok)
    block_start = jnp.arange(N_BLOCKS, dtype=I32) * MOE_ROWS
    block_e = jnp.minimum(jnp.searchsorted(pend, block_start, side='right'), N_EXPERTS - 1).astype(I32)
    n_used = (pend[-1] // MOE_ROWS).astype(I32).reshape(1)
    return dest.astype(I32), src_tok, block_e, n_used


def kernel(x_prompt, x_sample, cache_na_kv, cache_gqa_kv, c, c_ctx, norm1_g, norm2_g, w_mod, b_mod,
           w_in, w_fourier, na_rpb, q_norm_g, k_norm_g, w_pool, pool_scale, w_out, router_w, router_b,
           w_gu, b_gu, w_down, b_down, final_g):
    def lin_tables(L):
        cl, sl, bc, bs = _dft_tables(L)
        bands, icnt = _pool_tables(L)
        return (jnp.asarray(bc).astype(BF16), jnp.asarray(bs).astype(BF16),
                jnp.asarray(cl).astype(BF16), jnp.asarray(sl).astype(BF16),
                jnp.asarray(bands).astype(BF16), jnp.asarray(icnt))

    tabs_p = lin_tables(SEQ)
    tabs_s = lin_tables(DEC_SEQ)
    cos_np, sin_np = _rope_tables()
    cos_t, sin_t = jnp.asarray(cos_np), jnp.asarray(sin_np)
    gm = jnp.asarray(_group_mean_matrix(GROUP_W)).astype(BF16)

    cond8 = jnp.zeros((SUB, D_MODEL), F32).at[0].set(c_ctx).at[1:1 + DEC_BATCH].set(c)
    mods = _mods_call(cond8, w_mod, b_mod.reshape(DEPTH, 1, 6 * D_MODEL))
    mods = mods.reshape(DEPTH, SUB, 1, 6 * D_MODEL)
    row_p = lambda i: 0
    row_s = lambda i: 1 + i // (DEC_SEQ // ROWS)

    bias_tiles = _nabias_call(na_rpb.reshape(DEPTH * 4, N_DR * N_DC))
    bias_tiles = bias_tiles.reshape(DEPTH, 4, N_DR - 1, GRID_W, LANES)

    w_in_bf = w_in.astype(BF16)
    w_out_bf = w_out.astype(BF16)
    w_f_bf = w_fourier.astype(BF16)
    eye4 = jnp.eye(4, dtype=F32)
    w_pool_bd = jnp.einsum('lgce,gh->lgche', w_pool, eye4).reshape(DEPTH, GROUP_W, GROUP_W).astype(BF16)
    qg = jnp.tile(q_norm_g, (1, GROUP_W // HEAD_DIM)).reshape(DEPTH, 1, GROUP_W)
    kg = jnp.tile(k_norm_g, (1, GROUP_W // HEAD_DIM)).reshape(DEPTH, 1, GROUP_W)
    rw_pad = jnp.zeros((DEPTH, D_MODEL, LANES), F32).at[:, :, :N_EXPERTS].set(router_w)
    rw_hi = rw_pad.astype(BF16)
    rw_lo = (rw_pad - rw_hi.astype(F32)).astype(BF16)
    rb_pad = jnp.full((DEPTH, 1, LANES), NEG_INF, F32).at[:, 0, :N_EXPERTS].set(router_b)
    b_gu4 = b_gu.reshape(DEPTH, N_EXPERTS, 1, 2 * D_FF)
    b_down4 = b_down.reshape(DEPTH, N_EXPERTS, 1, D_MODEL)
    final_g2 = final_g.reshape(1, D_MODEL)

    xp = x_prompt.reshape(T_P, D_MODEL)
    xs = x_sample.reshape(T_S, D_MODEL)
    na_states, gqa_states = [], []
    for l in range(DEPTH):
        g1 = norm1_g[l].reshape(1, D_MODEL)
        g2 = norm2_g[l].reshape(1, D_MODEL)
        ps = pool_scale[l].reshape(1, GROUP_W)

        uf_p, naq_p, nakv_p, gq_p, gkv_p, up_p = _k1_call(
            xp, mods[l], row_p, g1, w_in_bf[l], qg[l], kg[l], gm, cos_t, sin_t, False, "k1_ctx")
        uf_s, naq_s, nakv_s, gq_s, gkv_s, up_s = _k1_call(
            xs, mods[l], row_s, g1, w_in_bf[l], qg[l], kg[l], gm, cos_t, sin_t, True, "k1_latent")
        na_states.append(nakv_p)
        gqa_states.append(gkv_p)

        a_p, d_p = _linmix_call(uf_p, up_p, SEQ, tabs_p, w_f_bf[l], w_pool_bd[l], ps, "linmix_ctx")
        a_s, d_s = _linmix_call(uf_s, up_s, DEC_SEQ, tabs_s, w_f_bf[l], w_pool_bd[l], ps, "linmix_latent")
        b_p, c_p = _attn_p_call(naq_p, nakv_p, gq_p, gkv_p)
        b_s = _na_s_call(naq_s, nakv_s, cache_na_kv[:, l].reshape(DEC_BATCH, 2, PAST_LEN, GROUP_W),
                         bias_tiles[l])
        c_s = _gqa_s_call(gq_s, gkv_s, cache_gqa_kv[:, l].reshape(DEC_BATCH, 2, PAST_LEN, KV_W))

        xm_p, h2_p, te_p, tg_p = _k3_call(xp, a_p, b_p, c_p, d_p, mods[l], row_p, w_out_bf[l], g2,
                                          rw_hi[l], rw_lo[l], rb_pad[l], "k3_ctx")
        xm_s, h2_s, te_s, tg_s = _k3_call(xs, a_s, b_s, c_s, d_s, mods[l], row_s, w_out_bf[l], g2,
                                          rw_hi[l], rw_lo[l], rb_pad[l], "k3_latent")

        h2_tiles = jnp.concatenate([h2_p, h2_s], axis=0).reshape(T_ALL, ROW_TILES, LANES)
        top_e = jnp.concatenate([te_p[:, :TOP_K], te_s[:, :TOP_K]], axis=0)
        dest, src_tok, block_e, n_used = _routing_tables(top_e)

        x_sorted = _dispatch_call(src_tok.reshape(N_BLOCKS, 1, MOE_ROWS), h2_tiles)
        yp = _moe_call(l, block_e, n_used, x_sorted, w_gu, b_gu4, w_down, b_down4)
        yp_tiles = yp.reshape(N_BLOCKS * MOE_ROWS, ROW_TILES, LANES)
        pos = dest.reshape(T_ALL // ROWS, 1, ROWS * TOP_K)
        final = l == DEPTH - 1
        xp = _combine_call(pos[:T_P // ROWS], yp_tiles, tg_p, xm_p, mods[l], row_p, final_g2, final,
                           "combine_ctx")
        xs = _combine_call(pos[T_P // ROWS:], yp_tiles, tg_s, xm_s, mods[l], row_s, final_g2, final,
                           "combine_latent")

    y_prompt = xp.reshape(BATCH, SEQ, D_MODEL)
    y_sample = xs.reshape(DEC_BATCH, DEC_SEQ, D_MODEL)
    new_na_kv = jnp.stack(na_states, axis=1).reshape(BATCH, DEPTH, 2, SEQ, 4, HEAD_DIM)
    new_gqa_kv = jnp.stack(gqa_states, axis=1).reshape(BATCH, DEPTH, 2, SEQ, 2, HEAD_DIM)
    return (y_prompt, y_sample, new_na_kv, new_gqa_kv)
```

```python
import functools

import numpy as np
import jax
import jax.numpy as jnp
from jax import lax
from jax.experimental import pallas as pl
from jax.experimental.pallas import tpu as pltpu

F32 = jnp.float32
BF16 = jnp.bfloat16
I32 = jnp.int32

D_MODEL = 1024
BATCH = 32
SEQ = 256
DEPTH = 4
DEC_BATCH = 2
DEC_SEQ = 1024
PAST_LEN = 256
GRID_W = 64
GRID_ROWS = DEC_SEQ // GRID_W
HEAD_DIM = 64
GROUP_W = 256
N_HEADS = GROUP_W // HEAD_DIM
KV_W = 128
POOL_WINDOWS = (2, 4, 8, 16)
NA_KH = 8
NA_KW = 16
N_EXPERTS = 32
TOP_K = 4
D_FF = 1024
SWIGLU_LIMIT = 7.0
SWIGLU_ALPHA = 1.702
ROPE_THETA = 10000.0
EPS = 1e-6
NEG_INF = -1e30
PROJ_W = 1792

ROWS = 256
SUB = 8
LANES = 128
ROW_TILES = D_MODEL // LANES
T_P = BATCH * SEQ
T_S = DEC_BATCH * DEC_SEQ
T_ALL = T_P + T_S
NB_P = T_P // ROWS
NB_S = T_S // ROWS
NB_TOK = NB_P + NB_S
SUB_S = DEC_SEQ // ROWS
MOE_ROWS = 256
N_ASSIGN = T_ALL * TOP_K
BLOCK_ASSIGN = ROWS * TOP_K
SEG = 16
SEG_SHIFT = 4
N_BLOCKS = N_ASSIGN // MOE_ROWS + 34
N_SORTED = N_BLOCKS * MOE_ROWS
STAGE_ROWS = 1792
VMEM_LIMIT = 56 * 1024 * 1024


def _dft_tables(L):
    n = np.arange(L)
    ang = 2.0 * np.pi * ((n[:, None] * n[None, :]) % L) / L
    c64 = np.arange(HEAD_DIM)
    a64 = 2.0 * np.pi * ((c64[:, None] * c64[None, :]) % HEAD_DIM) / HEAD_DIM
    bc = np.kron(np.eye(N_HEADS), np.cos(a64))
    bs = np.kron(np.eye(N_HEADS), np.sin(a64))
    return (np.cos(ang).astype(np.float32), np.sin(ang).astype(np.float32),
            bc.astype(np.float32), bs.astype(np.float32))


def _pool_tables(L):
    t = np.arange(L)
    bands = np.zeros((len(POOL_WINDOWS), L, L), np.float32)
    icnt = np.zeros((L, GROUP_W), np.float64)
    for gi, win in enumerate(POOL_WINDOWS):
        a = np.clip(t - win // 2, 0, L)
        b = np.clip(t + win // 2, 0, L)
        n = t[None, :]
        bands[gi] = ((n >= a[:, None]) & (n < b[:, None])).astype(np.float32)
        icnt[:, gi * 64:(gi + 1) * 64] = (1.0 / (b - a))[:, None]
    return bands, icnt.astype(np.float32)


def _rope_tables():
    t = np.arange(DEC_SEQ)
    row = (t // GRID_W).astype(np.float64)
    col = (t % GRID_W).astype(np.float64)
    n_freq = HEAD_DIM // 4
    inv = ROPE_THETA ** (-np.arange(n_freq, dtype=np.float64) / n_freq)
    ar = row[:, None] * inv[None, :]
    ac = col[:, None] * inv[None, :]
    cos = np.concatenate([np.cos(ar), np.cos(ar), np.cos(ac), np.cos(ac)], axis=1)
    sin = np.concatenate([-np.sin(ar), np.sin(ar), -np.sin(ac), np.sin(ac)], axis=1)
    cos = np.concatenate([np.ones((ROWS, HEAD_DIM)), cos], axis=0)
    sin = np.concatenate([np.zeros((ROWS, HEAD_DIM)), sin], axis=0)
    return (np.tile(cos, (1, N_HEADS)).astype(np.float32), np.tile(sin, (1, N_HEADS)).astype(np.float32))


def _group_mean_matrix(width):
    g = np.kron(np.eye(width // HEAD_DIM), np.ones((HEAD_DIM, HEAD_DIM)))
    return g.astype(np.float32)


def _dot(a, b):
    return jnp.dot(a, b, preferred_element_type=F32)


def _dot_nt(a, b):
    return lax.dot_general(a, b, (((1,), (1,)), ((), ())), preferred_element_type=F32)


def _split_bf16(x):
    hi = x.astype(BF16)
    lo = (x - hi.astype(F32)).astype(BF16)
    return hi, lo


def _rmsnorm_rows(x, g):
    ms = jnp.mean(x * x, axis=-1, keepdims=True)
    return x * lax.rsqrt(ms + EPS) * g


def _softmax_rows(s):
    m = jnp.max(s, axis=-1, keepdims=True)
    p = jnp.exp(s - m)
    return p, jnp.sum(p, axis=-1, keepdims=True)


def _full(shape):
    return pl.BlockSpec(shape, lambda *_: (0,) * len(shape))


def _mod_row(i):
    return jnp.where(i < NB_P, 0, 1 + (i - NB_P) // SUB_S)


def _latent_block(i):
    return jnp.maximum(i - NB_P, 0)


MOD_COLS = 512


def _mods_body(cond_ref, w_ref, b_ref, o_ref):
    c = cond_ref[...]
    s = c * (1.0 / (1.0 + jnp.exp(-c)))
    s_hi, s_lo = _split_bf16(s)
    w_hi, w_lo = _split_bf16(w_ref[...])
    o_ref[...] = _dot(s_hi, w_hi) + _dot(s_lo, w_hi) + _dot(s_hi, w_lo) + b_ref[...]


def _mods_call(cond8, w_mod, b_mod3):
    n_col = 6 * D_MODEL // MOD_COLS
    return pl.pallas_call(
        _mods_body,
        grid=(DEPTH, n_col),
        in_specs=[
            pl.BlockSpec((SUB, D_MODEL), lambda l, j: (0, 0)),
            pl.BlockSpec((None, D_MODEL, MOD_COLS), lambda l, j: (l, 0, j)),
            pl.BlockSpec((None, 1, MOD_COLS), lambda l, j: (l, 0, j)),
        ],
        out_specs=pl.BlockSpec((None, SUB, MOD_COLS), lambda l, j: (l, 0, j)),
        out_shape=jax.ShapeDtypeStruct((DEPTH, SUB, 6 * D_MODEL), F32),
        name="adaln_mods",
    )(cond8, w_mod, b_mod3)


def _head_norm(t, g, gm):
    hi, lo = _split_bf16(t * t)
    msq = (_dot(hi, gm) + _dot(lo, gm)) * (1.0 / HEAD_DIM)
    return t * lax.rsqrt(msq + EPS) * g


def _rope(t, cos, sin):
    w = t.shape[1]
    lane = lax.broadcasted_iota(I32, t.shape, 1)
    first = (lane % 32) < 16
    swapped = jnp.where(first, pltpu.roll(t, w - 16, 1), pltpu.roll(t, 16, 1))
    return t * cos + swapped * sin


def _k1_body(x_ref, mod_ref, g1_ref, win_ref, qg_ref, kg_ref, gm_ref, cos_ref, sin_ref, *rest):
    uf_ref, naq_ref, gq_ref, up_ref, nakv_ref, gkv_ref, nakv_s_ref, gkv_s_ref = rest[-8:]
    i = pl.program_id(0)
    sh1 = mod_ref[:, 0:D_MODEL]
    sc1 = mod_ref[:, D_MODEL:2 * D_MODEL]
    h = _rmsnorm_rows(x_ref[...], g1_ref[...]) * (1.0 + sc1) + sh1
    hb = h.astype(BF16)

    def proj(lo, hi):
        return _dot(hb, win_ref[:, lo:hi])

    uf_ref[...] = proj(0, 256).astype(BF16)
    naq_ref[...] = proj(256, 512).astype(BF16)
    na_k = proj(512, 768)
    na_v = proj(768, 1024)
    gq = _head_norm(proj(1024, 1280), qg_ref[...], gm_ref[...])
    gk = _head_norm(proj(1280, 1408), kg_ref[:, 0:KV_W], gm_ref[0:KV_W, 0:KV_W])
    gv = proj(1408, 1536)
    up_ref[...] = proj(1536, 1792)
    gq = _rope(gq, cos_ref[...], sin_ref[...])
    gk = _rope(gk, cos_ref[:, 0:KV_W], sin_ref[:, 0:KV_W])
    gq_ref[...] = gq.astype(BF16)

    @pl.when(i < NB_P)
    def _():
        nakv_ref[0] = na_k
        nakv_ref[1] = na_v
        gkv_ref[0] = gk
        gkv_ref[1] = gv

    @pl.when(i >= NB_P)
    def _():
        nakv_s_ref[0] = na_k.astype(BF16)
        nakv_s_ref[1] = na_v.astype(BF16)
        gkv_s_ref[0] = gk.astype(BF16)
        gkv_s_ref[1] = gv.astype(BF16)


def _k1_call(layer, x, mods_l, g1, w_in_bf, qg, kg, gm, cos, sin, na_glob, gqa_glob):
    rope_blk = lambda i: (jnp.where(i < NB_P, 0, 1 + (i - NB_P) % SUB_S), 0)
    ctx_blk = lambda i: (jnp.minimum(i, NB_P - 1), layer, 0, 0, 0)
    lat_blk = lambda i: (_latent_block(i) // SUB_S, 0, _latent_block(i) % SUB_S, 0)
    row256 = pl.BlockSpec((ROWS, GROUP_W), lambda i: (i, 0))
    in_specs = [
        pl.BlockSpec((ROWS, D_MODEL), lambda i: (i, 0)),
        pl.BlockSpec((None, 1, 6 * D_MODEL), lambda i: (_mod_row(i), 0, 0)),
        _full((1, D_MODEL)),
        _full((D_MODEL, PROJ_W)),
        _full((1, GROUP_W)),
        _full((1, GROUP_W)),
        _full((GROUP_W, GROUP_W)),
        pl.BlockSpec((ROWS, GROUP_W), rope_blk),
        pl.BlockSpec((ROWS, GROUP_W), rope_blk),
    ]
    in_specs += [pl.BlockSpec(memory_space=pl.ANY)] * 2
    args = [x, mods_l, g1, w_in_bf, qg, kg, gm, cos, sin, na_glob, gqa_glob]
    return pl.pallas_call(
        _k1_body,
        grid=(NB_TOK,),
        in_specs=in_specs,
        out_specs=[
            row256, row256, row256, row256,
            pl.BlockSpec((None, None, 2, ROWS, GROUP_W), ctx_blk),
            pl.BlockSpec((None, None, 2, ROWS, KV_W), ctx_blk),
            pl.BlockSpec((None, 2, ROWS, GROUP_W), lat_blk),
            pl.BlockSpec((None, 2, ROWS, KV_W), lat_blk),
        ],
        out_shape=[
            jax.ShapeDtypeStruct((T_ALL, GROUP_W), BF16),
            jax.ShapeDtypeStruct((T_ALL, GROUP_W), BF16),
            jax.ShapeDtypeStruct((T_ALL, GROUP_W), BF16),
            jax.ShapeDtypeStruct((T_ALL, GROUP_W), F32),
            jax.ShapeDtypeStruct((BATCH, DEPTH, 2, SEQ, GROUP_W), F32),
            jax.ShapeDtypeStruct((BATCH, DEPTH, 2, SEQ, KV_W), F32),
            jax.ShapeDtypeStruct((DEC_BATCH, 2, DEC_SEQ, GROUP_W), BF16),
            jax.ShapeDtypeStruct((DEC_BATCH, 2, DEC_SEQ, KV_W), BF16),
        ],
        input_output_aliases={9: 4, 10: 5},
        compiler_params=pltpu.CompilerParams(vmem_limit_bytes=VMEM_LIMIT),
        name="k1_norm_proj",
    )(*args)


def _linmix_body(seq, uf_ref, up_ref, bc_ref, bs_ref, cl_ref, sl_ref, wf_ref, band_ref, icnt_ref,
                 wp_ref, ps_ref, a_ref, d_ref):
    u = uf_ref[...]
    t1 = _dot(u, bc_ref[...]).astype(BF16)
    t2 = _dot(u, bs_ref[...]).astype(BF16)
    f = (_dot(cl_ref[...], t1) - _dot(sl_ref[...], t2)) * (1.0 / float(np.sqrt(seq * HEAD_DIM)))
    a_ref[...] = _dot(f.astype(BF16), wf_ref[...]).astype(BF16)

    up = up_ref[...]
    up_hi, up_lo = _split_bf16(up)
    group = lax.broadcasted_iota(I32, up.shape, 1) // HEAD_DIM
    win_sum = jnp.zeros_like(up)
    for gi in range(len(POOL_WINDOWS)):
        s = _dot(band_ref[gi], up_hi) + _dot(band_ref[gi], up_lo)
        win_sum = jnp.where(group == gi, s, win_sum)
    delta = win_sum * icnt_ref[...] - up
    d_ref[...] = (_dot(delta.astype(BF16), wp_ref[...]) * ps_ref[...]).astype(BF16)


def _linmix_call(uf, up, seq, first_blk, n_blk, tabs, wf_bf, wp_bd_bf, ps, name):
    bc, bs, cl, sl, bands, icnt = tabs
    blk = pl.BlockSpec((seq, GROUP_W), lambda i: (first_blk + i, 0))
    oblk = pl.BlockSpec((seq, GROUP_W), lambda i: (i, 0))
    return pl.pallas_call(
        functools.partial(_linmix_body, seq),
        grid=(n_blk,),
        in_specs=[
            blk, blk,
            _full((GROUP_W, GROUP_W)), _full((GROUP_W, GROUP_W)),
            _full((seq, seq)), _full((seq, seq)),
            _full((GROUP_W, GROUP_W)),
            _full((len(POOL_WINDOWS), seq, seq)),
            _full((seq, GROUP_W)),
            _full((GROUP_W, GROUP_W)),
            _full((1, GROUP_W)),
        ],
        out_specs=[oblk, oblk],
        out_shape=[jax.ShapeDtypeStruct((n_blk * seq, GROUP_W), BF16)] * 2,
        compiler_params=pltpu.CompilerParams(vmem_limit_bytes=VMEM_LIMIT),
        name=name,
    )(uf, up, bc, bs, cl, sl, wf_bf, bands, icnt, wp_bd_bf, ps)


def _attend(q, k, v):
    s = _dot_nt(q, k) * (HEAD_DIM ** -0.5)
    p, l = _softmax_rows(s)
    return _dot(p.astype(BF16), v) / l


def _attn_p_body(naq_ref, nakv_ref, gq_ref, gkv_ref, b_ref, c_ref):
    for h in range(N_HEADS):
        sl = slice(h * HEAD_DIM, (h + 1) * HEAD_DIM)
        k = nakv_ref[0, :, sl].astype(BF16)
        v = nakv_ref[1, :, sl].astype(BF16)
        b_ref[:, sl] = _attend(naq_ref[:, sl], k, v).astype(BF16)
    for h in range(N_HEADS):
        sl = slice(h * HEAD_DIM, (h + 1) * HEAD_DIM)
        kvh = h // 2
        ksl = slice(kvh * HEAD_DIM, (kvh + 1) * HEAD_DIM)
        k = gkv_ref[0, :, ksl].astype(BF16)
        v = gkv_ref[1, :, ksl].astype(BF16)
        c_ref[:, sl] = _attend(gq_ref[:, sl], k, v).astype(BF16)


def _attn_p_call(layer, naq, na_glob, gq, gqa_glob):
    row = pl.BlockSpec((SEQ, GROUP_W), lambda i: (i, 0))
    return pl.pallas_call(
        _attn_p_body,
        grid=(BATCH,),
        in_specs=[
            row,
            pl.BlockSpec((None, None, 2, SEQ, GROUP_W), lambda i: (i, layer, 0, 0, 0)),
            row,
            pl.BlockSpec((None, None, 2, SEQ, KV_W), lambda i: (i, layer, 0, 0, 0)),
        ],
        out_specs=[row, row],
        out_shape=[jax.ShapeDtypeStruct((T_P, GROUP_W), BF16)] * 2,
        name="attn_ctx",
    )(naq, na_glob, gq, gqa_glob)


N_DR = 2 * NA_KH - 1
N_DC = 2 * NA_KW - 1


def _nabias_body(rpb_ref, o_ref):
    lh = pl.program_id(0)
    q = lax.broadcasted_iota(I32, (GRID_W, LANES), 0)
    j = lax.broadcasted_iota(I32, (GRID_W, LANES), 1)
    k = j % GRID_W
    second = j >= GRID_W
    c0 = jnp.clip(q - NA_KW // 2, 0, GRID_W - NA_KW)
    valid = (k >= c0) & (k < c0 + NA_KW)
    diff = k - q + (NA_KW - 1)
    for dr in range(N_DR - 1):
        acc = jnp.zeros((GRID_W, LANES), F32)
        for dc in range(N_DC):
            val = jnp.where(second, rpb_ref[lh, (dr + 1) * N_DC + dc], rpb_ref[lh, dr * N_DC + dc])
            acc = jnp.where(diff == dc, val, acc)
        o_ref[dr] = jnp.where(valid, acc, NEG_INF)


def _nabias_call(rpb_flat):
    n = rpb_flat.shape[0]
    return pl.pallas_call(
        _nabias_body,
        grid=(n,),
        in_specs=[pl.BlockSpec(memory_space=pltpu.SMEM)],
        out_specs=pl.BlockSpec((None, N_DR - 1, GRID_W, LANES), lambda i: (i, 0, 0, 0)),
        out_shape=jax.ShapeDtypeStruct((n, N_DR - 1, GRID_W, LANES), F32),
        name="na_bias_tiles",
    )(rpb_flat)


N_WIN = NA_KH * GRID_W


def _na_s_body(q_ref, kv_ref, ckv_ref, bias_ref, o_ref):
    r = pl.program_id(1)
    r0 = jnp.clip(r - NA_KH // 2, 0, GRID_ROWS - NA_KH)
    start = pl.multiple_of(r0 * GRID_W, GRID_W)
    dr0 = r0 - r + (NA_KH - 1)
    scale = HEAD_DIM ** -0.5
    for h in range(N_HEADS):
        sl = slice(h * HEAD_DIM, (h + 1) * HEAD_DIM)
        q = q_ref[:, sl]
        kw = kv_ref[0, pl.ds(start, N_WIN), sl]
        vw = kv_ref[1, pl.ds(start, N_WIN), sl]
        kc = ckv_ref[0, :, sl].astype(BF16)
        vc = ckv_ref[1, :, sl].astype(BF16)
        bias = jnp.concatenate([bias_ref[h, dr0 + i] for i in range(0, NA_KH, 2)], axis=1)
        s_win = _dot_nt(q, kw) * scale + bias
        s_ctx = _dot_nt(q, kc) * scale
        s = jnp.concatenate([s_win, s_ctx], axis=1)
        p, l = _softmax_rows(s)
        pb = p.astype(BF16)
        o = _dot(pb[:, 0:N_WIN], vw) + _dot(pb[:, N_WIN:], vc)
        o_ref[:, sl] = (o / l).astype(BF16)


def _na_s_call(naq, nakv_s, cache_na_l, bias_l):
    first = T_P // GRID_W
    return pl.pallas_call(
        _na_s_body,
        grid=(DEC_BATCH, GRID_ROWS),
        in_specs=[
            pl.BlockSpec((GRID_W, GROUP_W), lambda b, r: (first + b * GRID_ROWS + r, 0)),
            pl.BlockSpec((None, 2, DEC_SEQ, GROUP_W), lambda b, r: (b, 0, 0, 0)),
            pl.BlockSpec((None, 2, PAST_LEN, GROUP_W), lambda b, r: (b, 0, 0, 0)),
            _full((N_HEADS, N_DR - 1, GRID_W, LANES)),
        ],
        out_specs=pl.BlockSpec((GRID_W, GROUP_W), lambda b, r: (b * GRID_ROWS + r, 0)),
        out_shape=jax.ShapeDtypeStruct((T_S, GROUP_W), BF16),
        compiler_params=pltpu.CompilerParams(vmem_limit_bytes=VMEM_LIMIT),
        name="attn_na_latent",
    )(naq, nakv_s, cache_na_l, bias_l)


GQ_ROWS = 128


def _gqa_s_body(q_ref, kv_ref, ckv_ref, o_ref):
    scale = HEAD_DIM ** -0.5
    for h in range(N_HEADS):
        sl = slice(h * HEAD_DIM, (h + 1) * HEAD_DIM)
        kvh = h // 2
        ksl = slice(kvh * HEAD_DIM, (kvh + 1) * HEAD_DIM)
        q = q_ref[:, sl]
        kl = kv_ref[0, :, ksl]
        vl = kv_ref[1, :, ksl]
        kc = ckv_ref[0, :, ksl].astype(BF16)
        vc = ckv_ref[1, :, ksl].astype(BF16)
        s = jnp.concatenate([_dot_nt(q, kl), _dot_nt(q, kc)], axis=1) * scale
        p, l = _softmax_rows(s)
        pb = p.astype(BF16)
        o = _dot(pb[:, 0:DEC_SEQ], vl) + _dot(pb[:, DEC_SEQ:], vc)
        o_ref[:, sl] = (o / l).astype(BF16)


def _gqa_s_call(gq, gkv_s, cache_gqa_l):
    nq = DEC_SEQ // GQ_ROWS
    first = T_P // GQ_ROWS
    return pl.pallas_call(
        _gqa_s_body,
        grid=(DEC_BATCH, nq),
        in_specs=[
            pl.BlockSpec((GQ_ROWS, GROUP_W), lambda b, j: (first + b * nq + j, 0)),
            pl.BlockSpec((None, 2, DEC_SEQ, KV_W), lambda b, j: (b, 0, 0, 0)),
            pl.BlockSpec((None, 2, PAST_LEN, KV_W), lambda b, j: (b, 0, 0, 0)),
        ],
        out_specs=pl.BlockSpec((GQ_ROWS, GROUP_W), lambda b, j: (b * nq + j, 0)),
        out_shape=jax.ShapeDtypeStruct((T_S, GROUP_W), BF16),
        compiler_params=pltpu.CompilerParams(vmem_limit_bytes=VMEM_LIMIT),
        name="attn_gqa_latent",
    )(gq, gkv_s, cache_gqa_l)


def _k3_body(x_ref, ap_ref, bp_ref, cp_ref, dp_ref, as_ref, bs_ref, cs_ref, ds_ref, mod_ref, wout_ref, g2_ref,
             rwh_ref, rwl_ref, rb_ref, xmid_ref, h2_ref, te_ref, tg_ref, pr_ref, cnt_ref):
    is_ctx = pl.program_id(0) < NB_P
    acc = jnp.zeros((ROWS, D_MODEL), F32)
    for g, (p_ref, s_ref) in enumerate(((ap_ref, as_ref), (bp_ref, bs_ref), (cp_ref, cs_ref), (dp_ref, ds_ref))):
        mixed = jnp.where(is_ctx, p_ref[...], s_ref[...])
        acc += _dot(mixed, wout_ref[g * GROUP_W:(g + 1) * GROUP_W, :])
    gate1 = mod_ref[:, 2 * D_MODEL:3 * D_MODEL]
    sh2 = mod_ref[:, 3 * D_MODEL:4 * D_MODEL]
    sc2 = mod_ref[:, 4 * D_MODEL:5 * D_MODEL]
    xm = x_ref[...] + gate1 * acc
    xmid_ref[...] = xm
    h2 = _rmsnorm_rows(xm, g2_ref[...]) * (1.0 + sc2) + sh2
    hi, lo = _split_bf16(h2)
    h2_ref[...] = hi
    logits = _dot(hi, rwh_ref[...]) + _dot(lo, rwh_ref[...]) + _dot(hi, rwl_ref[...]) + rb_ref[...]

    lane = lax.broadcasted_iota(I32, logits.shape, 1)
    cur = logits
    vals, idxs = [], []
    for _ in range(TOP_K):
        m = jnp.max(cur, axis=-1, keepdims=True)
        idx = jnp.min(jnp.where(cur == m, lane, LANES), axis=-1, keepdims=True)
        vals.append(m)
        idxs.append(idx)
        cur = jnp.where(lane == idx, -jnp.inf, cur)
    exps = [jnp.exp(v - vals[0]) for v in vals]
    tot = exps[0] + exps[1] + exps[2] + exps[3]
    te = jnp.zeros(logits.shape, I32)
    tg = jnp.zeros(logits.shape, F32)
    for k in range(TOP_K):
        te = jnp.where(lane == k, idxs[k], te)
        tg = jnp.where(lane == k, exps[k] / tot, tg)
    te_ref[...] = te
    tg_ref[...] = tg

    member = jnp.zeros(logits.shape, F32)
    for k in range(TOP_K):
        member = jnp.where(lane == idxs[k], 1.0, member)
    rr = lax.broadcasted_iota(I32, (ROWS, ROWS), 0)
    cc = lax.broadcasted_iota(I32, (ROWS, ROWS), 1)
    earlier = jnp.where(cc < rr, 1.0, 0.0).astype(BF16)
    prefix = _dot(earlier, member.astype(BF16))
    cnt = jnp.sum(member, axis=0, keepdims=True)
    er = lax.broadcasted_iota(I32, (LANES, LANES), 0)
    ec = lax.broadcasted_iota(I32, (LANES, LANES), 1)
    lower = jnp.where(er < ec, 1.0, 0.0).astype(BF16)
    loc = _dot(jnp.broadcast_to(cnt, (SUB, LANES)).astype(BF16), lower)[0:1, :]
    pr = jnp.zeros(logits.shape, F32)
    for k in range(TOP_K):
        hit = lane == idxs[k]
        rank = jnp.sum(jnp.where(hit, prefix, 0.0), axis=-1, keepdims=True)
        start = jnp.sum(jnp.where(hit, loc, 0.0), axis=-1, keepdims=True)
        pr = jnp.where(lane == k, start + rank, pr)
        pr = jnp.where(lane == TOP_K + k, rank, pr)
    pr_ref[...] = pr.astype(I32)
    cnt_ref[...] = cnt.astype(I32)


def _k3_call(x, mix_p, mix_s, mods_l, w_out_bf, g2, rwh, rwl, rb):
    row_p = pl.BlockSpec((ROWS, GROUP_W), lambda i: (jnp.minimum(i, NB_P - 1), 0))
    row_s = pl.BlockSpec((ROWS, GROUP_W), lambda i: (_latent_block(i), 0))
    rowd = pl.BlockSpec((ROWS, D_MODEL), lambda i: (i, 0))
    row128 = pl.BlockSpec((ROWS, LANES), lambda i: (i, 0))
    return pl.pallas_call(
        _k3_body,
        grid=(NB_TOK,),
        in_specs=[
            rowd, row_p, row_p, row_p, row_p, row_s, row_s, row_s, row_s,
            pl.BlockSpec((None, 1, 6 * D_MODEL), lambda i: (_mod_row(i), 0, 0)),
            _full((D_MODEL, D_MODEL)),
            _full((1, D_MODEL)),
            _full((D_MODEL, LANES)), _full((D_MODEL, LANES)), _full((1, LANES)),
        ],
        out_specs=[rowd, rowd, row128, row128, row128,
                   pl.BlockSpec((None, 1, LANES), lambda i: (i, 0, 0))],
        out_shape=[
            jax.ShapeDtypeStruct((T_ALL, D_MODEL), F32),
            jax.ShapeDtypeStruct((T_ALL, D_MODEL), BF16),
            jax.ShapeDtypeStruct((T_ALL, LANES), I32),
            jax.ShapeDtypeStruct((T_ALL, LANES), F32),
            jax.ShapeDtypeStruct((T_ALL, LANES), I32),
            jax.ShapeDtypeStruct((NB_TOK, 1, LANES), I32),
        ],
        compiler_params=pltpu.CompilerParams(vmem_limit_bytes=VMEM_LIMIT),
        name="k3_out_router",
    )(x, *mix_p, *mix_s, mods_l, w_out_bf, g2, rwh, rwl, rb)


TAB_CNT, TAB_LOC, TAB_BASE, TAB_LSTAGE, TAB_SOFF, TAB_TOTAL, TAB_NCHUNK = range(7)


def _routing_tables(cnt):
    c = cnt.reshape(NB_TOK, LANES)[:, :N_EXPERTS]
    counts = jnp.sum(c, axis=0)
    padded = (counts + (SEG - 1) + (MOE_ROWS - 1)) // MOE_ROWS * MOE_ROWS
    pend = jnp.cumsum(padded)
    pstart = pend - padded
    base = pstart[None, :] + jnp.cumsum(c, axis=0) - c
    loc = jnp.cumsum(c, axis=1) - c
    off = base % SUB
    nch = jnp.where(c > 0, (off + c + SEG - 1) // SEG, 0)
    lstage = (jnp.cumsum(nch, axis=1) - nch) * SEG
    soff = lstage + off
    total = jnp.sum(nch, axis=1, keepdims=True) * SEG

    def pad(t):
        return jnp.zeros((NB_TOK, LANES), I32).at[:, :t.shape[1]].set(t.astype(I32))

    tab = jnp.stack([pad(c), pad(loc), pad(base), pad(lstage), pad(soff), pad(total), pad(nch),
                     jnp.zeros((NB_TOK, LANES), I32)], axis=1)
    ztab = jnp.zeros((SUB, LANES), I32).at[0, :N_EXPERTS].set((pstart + counts).astype(I32))
    ztab = ztab.at[1, :N_EXPERTS].set(pend.astype(I32))
    ztab = ztab.at[2, 0].set((pend[-1] // MOE_ROWS).astype(I32))
    block_start = jnp.arange(N_BLOCKS, dtype=I32) * MOE_ROWS
    block_e = jnp.sum((block_start[:, None] >= pend[None, :]).astype(I32), axis=1)
    block_e = jnp.minimum(block_e, N_EXPERTS - 1).astype(I32)
    n_used = (pend[-1] // MOE_ROWS).astype(I32).reshape(1)
    return tab, ztab, block_e, n_used


def _dispatch_body(tab_ref, ztab_ref, h2_ref, pr_ref, xs_ref, stage, zbuf, sem):
    b = pl.program_id(0)

    def seg_copy(src, src_row, dst_row):
        return pltpu.make_async_copy(
            src.at[pl.ds(pl.multiple_of(src_row * SUB, SUB), SEG * SUB)],
            xs_ref.at[pl.ds(pl.multiple_of(dst_row * SUB, SUB), SEG * SUB)], sem)

    def drain(n):
        def body(j, carry):
            seg_copy(zbuf, 0, 0).wait()
            return carry
        lax.fori_loop(0, n, body, 0)

    @pl.when(b == 0)
    def _():
        zbuf[...] = jnp.zeros(zbuf.shape, F32)
        stage[pl.ds(BLOCK_ASSIGN * SUB, SEG * SUB), :] = jnp.zeros((SEG * SUB, LANES), F32)

        def row_copy(dst_row):
            return pltpu.make_async_copy(
                zbuf.at[pl.ds(0, SUB)], xs_ref.at[pl.ds(pl.multiple_of(dst_row * SUB, SUB), SUB)], sem)

        def per_expert(e, carry):
            n_seg, n_row = carry
            end = ztab_ref[0, e]
            length = ztab_ref[1, e] - end
            n = lax.shift_right_logical(length, SEG_SHIFT)
            rem = length & (SEG - 1)

            def one(j, c):
                seg_copy(zbuf, 0, end + j * SEG).start()
                return c
            lax.fori_loop(0, n, one, 0)

            def one_row(j, c):
                row_copy(end + n * SEG + j).start()
                return c
            lax.fori_loop(0, rem, one_row, 0)
            return n_seg + n, n_row + rem
        n_seg, n_row = lax.fori_loop(0, N_EXPERTS, per_expert, (0, 0))
        drain(n_seg)

        def row_wait(j, c):
            row_copy(0).wait()
            return c
        lax.fori_loop(0, n_row, row_wait, 0)

        first_free = ztab_ref[2, 0]

        def block_copy(blk):
            return pltpu.make_async_copy(
                zbuf, xs_ref.at[pl.ds(pl.multiple_of(blk * (MOE_ROWS * SUB), MOE_ROWS * SUB), MOE_ROWS * SUB)], sem)

        def fill(j, c):
            block_copy(first_free + j).start()
            return c
        lax.fori_loop(0, N_BLOCKS - first_free, fill, 0)

        def fill_wait(j, c):
            block_copy(0).wait()
            return c
        lax.fori_loop(0, N_BLOCKS - first_free, fill_wait, 0)

    pos_t = pr_ref[...].astype(F32).T
    h2 = h2_ref[...]
    for c in range(BLOCK_ASSIGN // ROWS):
        p = (lax.broadcasted_iota(I32, (ROWS, ROWS), 0) + c * ROWS).astype(F32)
        sel = p == pos_t[0:1, :]
        for k in range(1, TOP_K):
            sel = sel | (p == pos_t[k:k + 1, :])
        rows = _dot(jnp.where(sel, 1.0, 0.0).astype(BF16), h2)
        for s in range(ROW_TILES):
            stage[pl.ds(c * ROWS * SUB + s, ROWS, stride=SUB), :] = rows[:, s * LANES:(s + 1) * LANES]

    def per_expert(e, tot):
        loc = tab_ref[TAB_LOC, e]
        base = tab_ref[TAB_BASE, e]
        n = lax.shift_right_logical(tab_ref[TAB_CNT, e] + (SEG - 1), SEG_SHIFT)

        def one(j, carry):
            seg_copy(stage, loc + j * SEG, base + j * SEG).start()
            return carry
        lax.fori_loop(0, n, one, 0)
        return tot + n
    drain(lax.fori_loop(0, N_EXPERTS, per_expert, 0))


def _dispatch_call(tab, ztab, h2, pr):
    return pl.pallas_call(
        _dispatch_body,
        grid=(NB_TOK,),
        in_specs=[
            pl.BlockSpec((None, SUB, LANES), lambda i: (i, 0, 0), memory_space=pltpu.SMEM),
            pl.BlockSpec(memory_space=pltpu.SMEM),
            pl.BlockSpec((ROWS, D_MODEL), lambda i: (i, 0)),
            pl.BlockSpec((ROWS, LANES), lambda i: (i, 0)),
        ],
        out_specs=pl.BlockSpec(memory_space=pl.ANY),
        out_shape=jax.ShapeDtypeStruct((N_SORTED * ROW_TILES, LANES), F32),
        scratch_shapes=[
            pltpu.VMEM(((BLOCK_ASSIGN + SEG) * SUB, LANES), F32),
            pltpu.VMEM((MOE_ROWS * SUB, LANES), F32),
            pltpu.SemaphoreType.DMA,
        ],
        compiler_params=pltpu.CompilerParams(vmem_limit_bytes=VMEM_LIMIT),
        name="moe_dispatch",
    )(tab, ztab, h2, pr)


HALF_FF = D_FF // 2


def _moe_body(be_ref, nu_ref, x_ref, wgu_ref, bgu_ref, wd_ref, bd_ref, o_ref, wgu_bf, wd_bf, wd_scr):
    i = pl.program_id(0)
    used = i < nu_ref[0]
    fresh = (i == 0) | (be_ref[i] != be_ref[jnp.maximum(i - 1, 0)])

    @pl.when(used & fresh)
    def _():
        for c in range(4):
            cs = slice(c * 512, (c + 1) * 512)
            wgu_bf[:, cs] = wgu_ref[:, cs].astype(BF16)
        for c in range(ROW_TILES):
            cs = slice(c * LANES, (c + 1) * LANES)
            wd_scr[pl.ds(0, HALF_FF, stride=2), :] = wd_ref[0:HALF_FF, cs]
            wd_scr[pl.ds(1, HALF_FF, stride=2), :] = wd_ref[HALF_FF:D_FF, cs]
            wd_bf[:, cs] = wd_scr[...].astype(BF16)

    @pl.when(used)
    def _():
        x = jnp.concatenate(
            [x_ref[pl.ds(s, MOE_ROWS, stride=ROW_TILES), :] for s in range(ROW_TILES)], axis=1)
        xb = x.astype(BF16)
        ha = _dot(xb, wgu_bf[:, 0:D_FF]) + bgu_ref[:, 0:D_FF]
        hb = _dot(xb, wgu_bf[:, D_FF:2 * D_FF]) + bgu_ref[:, D_FF:2 * D_FF]
        even = (lax.broadcasted_iota(I32, ha.shape, 1) % 2) == 0
        gate = jnp.where(even, ha, pltpu.roll(hb, 1, 1))
        up = jnp.where(even, pltpu.roll(ha, D_FF - 1, 1), hb)
        gate = jnp.minimum(gate, SWIGLU_LIMIT)
        up = jnp.clip(up, -SWIGLU_LIMIT, SWIGLU_LIMIT)
        glu = gate * (1.0 / (1.0 + jnp.exp(-gate * SWIGLU_ALPHA)))
        act = ((up + 1.0) * glu).astype(BF16)
        o_ref[...] = _dot(act, wd_bf[...]) + bd_ref[...]

    @pl.when(jnp.logical_not(used))
    def _():
        o_ref[...] = jnp.zeros(o_ref.shape, F32)


def _moe_call(layer, block_e, n_used, xs, w_gu, b_gu4, w_down, b_down4):
    def xmap(i, be, nu):
        return (jnp.minimum(i, nu[0] - 1), 0)

    grid_spec = pltpu.PrefetchScalarGridSpec(
        num_scalar_prefetch=2,
        grid=(N_BLOCKS,),
        in_specs=[
            pl.BlockSpec((MOE_ROWS * ROW_TILES, LANES), xmap),
            pl.BlockSpec((None, None, D_MODEL, 2 * D_FF), lambda i, be, nu: (layer, be[i], 0, 0)),
            pl.BlockSpec((None, None, 1, 2 * D_FF), lambda i, be, nu: (layer, be[i], 0, 0)),
            pl.BlockSpec((None, None, D_FF, D_MODEL), lambda i, be, nu: (layer, be[i], 0, 0)),
            pl.BlockSpec((None, None, 1, D_MODEL), lambda i, be, nu: (layer, be[i], 0, 0)),
        ],
        out_specs=pl.BlockSpec((MOE_ROWS, D_MODEL), lambda i, be, nu: (i, 0)),
        scratch_shapes=[
            pltpu.VMEM((D_MODEL, 2 * D_FF), BF16),
            pltpu.VMEM((D_FF, D_MODEL), BF16),
            pltpu.VMEM((D_FF, LANES), F32),
        ],
    )
    return pl.pallas_call(
        _moe_body,
        grid_spec=grid_spec,
        out_shape=jax.ShapeDtypeStruct((N_SORTED, D_MODEL), F32),
        compiler_params=pltpu.CompilerParams(vmem_limit_bytes=VMEM_LIMIT),
        name="moe_experts",
    )(block_e, n_used, xs, w_gu, b_gu4, w_down, b_down4)


def _combine_body(final, tab_ref, tabv_ref, yp_ref, te_ref, tg_ref, pr_ref, xmid_ref, mod_ref, fg_ref,
                  o_ref, stage, sem):
    def chunk_copy(src_row, dst_row):
        return pltpu.make_async_copy(
            yp_ref.at[pl.ds(pl.multiple_of(src_row, SUB), SEG)],
            stage.at[pl.ds(pl.multiple_of(dst_row, SUB), SEG)], sem)

    def per_expert(e, tot):
        base = tab_ref[TAB_BASE, e]
        off = base & (SUB - 1)
        n = tab_ref[TAB_NCHUNK, e]
        lstage = tab_ref[TAB_LSTAGE, e]

        def one(j, carry):
            chunk_copy(base - off + j * SEG, lstage + j * SEG).start()
            return carry
        lax.fori_loop(0, n, one, 0)
        return tot + n
    n_chunks = lax.fori_loop(0, N_EXPERTS, per_expert, 0)

    total = tab_ref[TAB_TOTAL, 0]

    def zero_tail(j, carry):
        stage[pl.ds(pl.multiple_of(total + j * SEG, SEG), SEG), :] = jnp.zeros((SEG, D_MODEL), F32)
        return carry
    lax.fori_loop(0, lax.shift_right_logical(STAGE_ROWS - total, SEG_SHIFT), zero_tail, 0)

    def drain(j, carry):
        chunk_copy(0, 0).wait()
        return carry
    lax.fori_loop(0, n_chunks, drain, 0)

    lane = lax.broadcasted_iota(I32, (ROWS, LANES), 1)
    te = te_ref[...]
    pr = pr_ref[...]
    soff = tabv_ref[TAB_SOFF:TAB_SOFF + 1, :].astype(F32)
    packed = jnp.where((lane >= TOP_K) & (lane < 2 * TOP_K), pltpu.roll(tg_ref[...], TOP_K, 1), 0.0)
    sp = []
    for k in range(TOP_K):
        seg_start = jnp.sum(jnp.where(lane == te[:, k:k + 1], soff, 0.0), axis=-1, keepdims=True)
        spk = seg_start + pr[:, TOP_K + k:TOP_K + k + 1].astype(F32)
        sp.append(spk)
        packed = jnp.where(lane == k, spk, packed)
    packed_t = packed.T

    out = jnp.zeros((ROWS, D_MODEL), F32)
    for c in range(STAGE_ROWS // ROWS):
        pcol = (lax.broadcasted_iota(I32, (ROWS, ROWS), 1) + c * ROWS).astype(F32)
        sel = pcol == sp[0]
        for k in range(1, TOP_K):
            sel = sel | (pcol == sp[k])
        prow = (lax.broadcasted_iota(I32, (ROWS, ROWS), 0) + c * ROWS).astype(F32)
        gmat = jnp.zeros((ROWS, ROWS), F32)
        for k in range(TOP_K):
            gmat = jnp.where(prow == packed_t[k:k + 1, :], packed_t[TOP_K + k:TOP_K + k + 1, :], gmat)
        row_gate = jnp.sum(gmat, axis=-1, keepdims=True)
        rows = (stage[c * ROWS:(c + 1) * ROWS, :] * row_gate).astype(BF16)
        out += _dot(jnp.where(sel, 1.0, 0.0).astype(BF16), rows)

    gate2 = mod_ref[:, 5 * D_MODEL:6 * D_MODEL]
    x = xmid_ref[...] + gate2 * out
    if final:
        x = _rmsnorm_rows(x, fg_ref[...])
    o_ref[...] = x


def _combine_call(tab, yp, te, tg, pr, xmid, mods_l, final_g, final):
    rowd = pl.BlockSpec((ROWS, D_MODEL), lambda i: (i, 0))
    row128 = pl.BlockSpec((ROWS, LANES), lambda i: (i, 0))
    return pl.pallas_call(
        functools.partial(_combine_body, final),
        grid=(NB_TOK,),
        in_specs=[
            pl.BlockSpec((None, SUB, LANES), lambda i: (i, 0, 0), memory_space=pltpu.SMEM),
            pl.BlockSpec((None, SUB, LANES), lambda i: (i, 0, 0)),
            pl.BlockSpec(memory_space=pl.ANY),
            row128, row128, row128,
            rowd,
            pl.BlockSpec((None, 1, 6 * D_MODEL), lambda i: (_mod_row(i), 0, 0)),
            _full((1, D_MODEL)),
        ],
        out_specs=rowd,
        out_shape=jax.ShapeDtypeStruct((T_ALL, D_MODEL), F32),
        scratch_shapes=[pltpu.VMEM((STAGE_ROWS, D_MODEL), F32), pltpu.SemaphoreType.DMA],
        compiler_params=pltpu.CompilerParams(vmem_limit_bytes=VMEM_LIMIT),
        name="moe_combine",
    )(tab, tab, yp, te, tg, pr, xmid, mods_l, final_g)


def kernel(x_prompt, x_sample, cache_na_kv, cache_gqa_kv, c, c_ctx, norm1_g, norm2_g, w_mod, b_mod,
           w_in, w_fourier, na_rpb, q_norm_g, k_norm_g, w_pool, pool_scale, w_out, router_w, router_b,
           w_gu, b_gu, w_down, b_down, final_g):
    def lin_tables(L):
        cl, sl, bc, bs = _dft_tables(L)
        bands, icnt = _pool_tables(L)
        return (jnp.asarray(bc).astype(BF16), jnp.asarray(bs).astype(BF16),
                jnp.asarray(cl).astype(BF16), jnp.asarray(sl).astype(BF16),
                jnp.asarray(bands).astype(BF16), jnp.asarray(icnt))

    tabs_p = lin_tables(SEQ)
    tabs_s = lin_tables(DEC_SEQ)
    cos_np, sin_np = _rope_tables()
    cos_t, sin_t = jnp.asarray(cos_np), jnp.asarray(sin_np)
    gm = jnp.asarray(_group_mean_matrix(GROUP_W)).astype(BF16)

    cond8 = jnp.zeros((SUB, D_MODEL), F32).at[0].set(c_ctx).at[1:1 + DEC_BATCH].set(c)
    mods = _mods_call(cond8, w_mod, b_mod.reshape(DEPTH, 1, 6 * D_MODEL))
    mods = mods.reshape(DEPTH, SUB, 1, 6 * D_MODEL)

    bias_tiles = _nabias_call(na_rpb.reshape(DEPTH * N_HEADS, N_DR * N_DC))
    bias_tiles = bias_tiles.reshape(DEPTH, N_HEADS, N_DR - 1, GRID_W, LANES)

    w_in_bf = w_in.astype(BF16)
    w_out_bf = w_out.astype(BF16)
    w_f_bf = w_fourier.astype(BF16)
    eye4 = jnp.eye(4, dtype=F32)
    w_pool_bd = jnp.einsum('lgce,gh->lgche', w_pool, eye4).reshape(DEPTH, GROUP_W, GROUP_W).astype(BF16)
    qg = jnp.tile(q_norm_g, (1, N_HEADS)).reshape(DEPTH, 1, GROUP_W)
    kg = jnp.tile(k_norm_g, (1, N_HEADS)).reshape(DEPTH, 1, GROUP_W)
    rw_pad = jnp.zeros((DEPTH, D_MODEL, LANES), F32).at[:, :, :N_EXPERTS].set(router_w)
    rw_hi = rw_pad.astype(BF16)
    rw_lo = (rw_pad - rw_hi.astype(F32)).astype(BF16)
    rb_pad = jnp.full((DEPTH, 1, LANES), NEG_INF, F32).at[:, 0, :N_EXPERTS].set(router_b)
    b_gu4 = b_gu.reshape(DEPTH, N_EXPERTS, 1, 2 * D_FF)
    b_down4 = b_down.reshape(DEPTH, N_EXPERTS, 1, D_MODEL)
    final_g2 = final_g.reshape(1, D_MODEL)
    cache_na = cache_na_kv.reshape(DEC_BATCH, DEPTH, 2, PAST_LEN, GROUP_W)
    cache_gqa = cache_gqa_kv.reshape(DEC_BATCH, DEPTH, 2, PAST_LEN, KV_W)

    x = jnp.concatenate([x_prompt.reshape(T_P, D_MODEL), x_sample.reshape(T_S, D_MODEL)], axis=0)
    na_glob = jnp.zeros((BATCH, DEPTH, 2, SEQ, GROUP_W), F32)
    gqa_glob = jnp.zeros((BATCH, DEPTH, 2, SEQ, KV_W), F32)
    for l in range(DEPTH):
        g1 = norm1_g[l].reshape(1, D_MODEL)
        g2 = norm2_g[l].reshape(1, D_MODEL)
        ps = pool_scale[l].reshape(1, GROUP_W)

        uf, naq, gq, up, na_glob, gqa_glob, nakv_s, gkv_s = _k1_call(
            l, x, mods[l], g1, w_in_bf[l], qg[l], kg[l], gm, cos_t, sin_t, na_glob, gqa_glob)

        a_p, d_p = _linmix_call(uf, up, SEQ, 0, BATCH, tabs_p, w_f_bf[l], w_pool_bd[l], ps, "linmix_ctx")
        a_s, d_s = _linmix_call(uf, up, DEC_SEQ, T_P // DEC_SEQ, DEC_BATCH, tabs_s, w_f_bf[l], w_pool_bd[l], ps,
                                "linmix_latent")
        b_p, c_p = _attn_p_call(l, naq, na_glob, gq, gqa_glob)
        b_s = _na_s_call(naq, nakv_s, cache_na[:, l], bias_tiles[l])
        c_s = _gqa_s_call(gq, gkv_s, cache_gqa[:, l])

        xmid, h2, te, tg, pr, cnt = _k3_call(x, (a_p, b_p, c_p, d_p), (a_s, b_s, c_s, d_s), mods[l], w_out_bf[l],
                                             g2, rw_hi[l], rw_lo[l], rb_pad[l])
        tab, ztab, block_e, n_used = _routing_tables(cnt)
        xs = _dispatch_call(tab, ztab, h2, pr)
        yp = _moe_call(l, block_e, n_used, xs, w_gu, b_gu4, w_down, b_down4)
        x = _combine_call(tab, yp, te, tg, pr, xmid, mods[l], final_g2, l == DEPTH - 1)

    y_prompt = x[:T_P].reshape(BATCH, SEQ, D_MODEL)
    y_sample = x[T_P:].reshape(DEC_BATCH, DEC_SEQ, D_MODEL)
    new_na_kv = na_glob.reshape(BATCH, DEPTH, 2, SEQ, N_HEADS, HEAD_DIM)
    new_gqa_kv = gqa_glob.reshape(BATCH, DEPTH, 2, SEQ, 2, HEAD_DIM)
    return (y_prompt, y_sample, new_na_kv, new_gqa_kv)
```

```python
import functools

import numpy as np
import jax
import jax.numpy as jnp
from jax import lax
from jax.experimental import pallas as pl
from jax.experimental.pallas import tpu as pltpu

F32 = jnp.float32
BF16 = jnp.bfloat16
I32 = jnp.int32

D_MODEL = 1024
BATCH = 32
SEQ = 256
DEPTH = 4
DEC_BATCH = 2
DEC_SEQ = 1024
PAST_LEN = 256
GRID_W = 64
GRID_ROWS = DEC_SEQ // GRID_W
HEAD_DIM = 64
GROUP_W = 256
N_HEADS = GROUP_W // HEAD_DIM
KV_W = 128
POOL_WINDOWS = (2, 4, 8, 16)
NA_KH = 8
NA_KW = 16
N_EXPERTS = 32
TOP_K = 4
D_FF = 1024
SWIGLU_LIMIT = 7.0
SWIGLU_ALPHA = 1.702
ROPE_THETA = 10000.0
EPS = 1e-6
NEG_INF = -1e30
PROJ_W = 1792

ROWS = 256
SUB = 8
LANES = 128
ROW_TILES = D_MODEL // LANES
T_P = BATCH * SEQ
T_S = DEC_BATCH * DEC_SEQ
T_ALL = T_P + T_S
NB_P = T_P // ROWS
NB_S = T_S // ROWS
NB_TOK = NB_P + NB_S
SUB_S = DEC_SEQ // ROWS
MOE_ROWS = 256
N_ASSIGN = T_ALL * TOP_K
BLOCK_ASSIGN = ROWS * TOP_K
SEG = 16
SEG_SHIFT = 4
N_BLOCKS = N_ASSIGN // MOE_ROWS + 34
N_SORTED = N_BLOCKS * MOE_ROWS
STAGE_ROWS = 1792
VMEM_LIMIT = 56 * 1024 * 1024


def _dft_tables(L):
    n = np.arange(L)
    ang = 2.0 * np.pi * ((n[:, None] * n[None, :]) % L) / L
    c64 = np.arange(HEAD_DIM)
    a64 = 2.0 * np.pi * ((c64[:, None] * c64[None, :]) % HEAD_DIM) / HEAD_DIM
    bc = np.kron(np.eye(N_HEADS), np.cos(a64))
    bs = np.kron(np.eye(N_HEADS), np.sin(a64))
    return (np.cos(ang).astype(np.float32), np.sin(ang).astype(np.float32),
            bc.astype(np.float32), bs.astype(np.float32))


def _pool_tables(L):
    t = np.arange(L)
    bands = np.zeros((len(POOL_WINDOWS), L, L), np.float32)
    icnt = np.zeros((L, GROUP_W), np.float64)
    for gi, win in enumerate(POOL_WINDOWS):
        a = np.clip(t - win // 2, 0, L)
        b = np.clip(t + win // 2, 0, L)
        n = t[None, :]
        bands[gi] = ((n >= a[:, None]) & (n < b[:, None])).astype(np.float32)
        icnt[:, gi * 64:(gi + 1) * 64] = (1.0 / (b - a))[:, None]
    return bands, icnt.astype(np.float32)


def _rope_tables():
    t = np.arange(DEC_SEQ)
    row = (t // GRID_W).astype(np.float64)
    col = (t % GRID_W).astype(np.float64)
    n_freq = HEAD_DIM // 4
    inv = ROPE_THETA ** (-np.arange(n_freq, dtype=np.float64) / n_freq)
    ar = row[:, None] * inv[None, :]
    ac = col[:, None] * inv[None, :]
    cos = np.concatenate([np.cos(ar), np.cos(ar), np.cos(ac), np.cos(ac)], axis=1)
    sin = np.concatenate([-np.sin(ar), np.sin(ar), -np.sin(ac), np.sin(ac)], axis=1)
    cos = np.concatenate([np.ones((ROWS, HEAD_DIM)), cos], axis=0)
    sin = np.concatenate([np.zeros((ROWS, HEAD_DIM)), sin], axis=0)
    return (np.tile(cos, (1, N_HEADS)).astype(np.float32), np.tile(sin, (1, N_HEADS)).astype(np.float32))


def _group_mean_matrix(width):
    g = np.kron(np.eye(width // HEAD_DIM), np.ones((HEAD_DIM, HEAD_DIM)))
    return g.astype(np.float32)


def _dot(a, b):
    return jnp.dot(a, b, preferred_element_type=F32)


def _dot_nt(a, b):
    return lax.dot_general(a, b, (((1,), (1,)), ((), ())), preferred_element_type=F32)


def _split_bf16(x):
    hi = x.astype(BF16)
    lo = (x - hi.astype(F32)).astype(BF16)
    return hi, lo


def _rmsnorm_rows(x, g):
    ms = jnp.mean(x * x, axis=-1, keepdims=True)
    return x * lax.rsqrt(ms + EPS) * g


def _softmax_rows(s):
    m = jnp.max(s, axis=-1, keepdims=True)
    p = jnp.exp(s - m)
    return p, jnp.sum(p, axis=-1, keepdims=True)


def _full(shape):
    return pl.BlockSpec(shape, lambda *_: (0,) * len(shape))


def _mod_row(i):
    return jnp.where(i < NB_P, 0, 1 + (i - NB_P) // SUB_S)


def _latent_block(i):
    return jnp.maximum(i - NB_P, 0)


MOD_COLS = 512


def _mods_body(cond_ref, w_ref, b_ref, o_ref):
    c = cond_ref[...]
    s = c * (1.0 / (1.0 + jnp.exp(-c)))
    s_hi, s_lo = _split_bf16(s)
    w_hi, w_lo = _split_bf16(w_ref[...])
    o_ref[...] = _dot(s_hi, w_hi) + _dot(s_lo, w_hi) + _dot(s_hi, w_lo) + b_ref[...]


def _mods_call(cond8, w_mod, b_mod3):
    n_col = 6 * D_MODEL // MOD_COLS
    return pl.pallas_call(
        _mods_body,
        grid=(DEPTH, n_col),
        in_specs=[
            pl.BlockSpec((SUB, D_MODEL), lambda l, j: (0, 0)),
            pl.BlockSpec((None, D_MODEL, MOD_COLS), lambda l, j: (l, 0, j)),
            pl.BlockSpec((None, 1, MOD_COLS), lambda l, j: (l, 0, j)),
        ],
        out_specs=pl.BlockSpec((None, SUB, MOD_COLS), lambda l, j: (l, 0, j)),
        out_shape=jax.ShapeDtypeStruct((DEPTH, SUB, 6 * D_MODEL), F32),
        name="adaln_mods",
    )(cond8, w_mod, b_mod3)


def _head_norm(t, g, gm):
    hi, lo = _split_bf16(t * t)
    msq = (_dot(hi, gm) + _dot(lo, gm)) * (1.0 / HEAD_DIM)
    return t * lax.rsqrt(msq + EPS) * g


def _rope(t, cos, sin):
    w = t.shape[1]
    lane = lax.broadcasted_iota(I32, t.shape, 1)
    first = (lane % 32) < 16
    swapped = jnp.where(first, pltpu.roll(t, w - 16, 1), pltpu.roll(t, 16, 1))
    return t * cos + swapped * sin


def _k1_body(x_ref, mod_ref, g1_ref, win_ref, qg_ref, kg_ref, gm_ref, cos_ref, sin_ref, *rest):
    uf_ref, naq_ref, gq_ref, up_ref, nakv_ref, gkv_ref, nakv_s_ref, gkv_s_ref = rest[-8:]
    i = pl.program_id(0)
    sh1 = mod_ref[:, 0:D_MODEL]
    sc1 = mod_ref[:, D_MODEL:2 * D_MODEL]
    h = _rmsnorm_rows(x_ref[...], g1_ref[...]) * (1.0 + sc1) + sh1
    hb = h.astype(BF16)

    def proj(lo, hi):
        return _dot(hb, win_ref[:, lo:hi])

    uf_ref[...] = proj(0, 256).astype(BF16)
    naq_ref[...] = proj(256, 512).astype(BF16)
    na_k = proj(512, 768)
    na_v = proj(768, 1024)
    gq = _head_norm(proj(1024, 1280), qg_ref[...], gm_ref[...])
    gk = _head_norm(proj(1280, 1408), kg_ref[:, 0:KV_W], gm_ref[0:KV_W, 0:KV_W])
    gv = proj(1408, 1536)
    up_ref[...] = proj(1536, 1792)
    gq = _rope(gq, cos_ref[...], sin_ref[...])
    gk = _rope(gk, cos_ref[:, 0:KV_W], sin_ref[:, 0:KV_W])
    gq_ref[...] = gq.astype(BF16)

    @pl.when(i < NB_P)
    def _():
        nakv_ref[0] = na_k
        nakv_ref[1] = na_v
        gkv_ref[0] = gk
        gkv_ref[1] = gv

    @pl.when(i >= NB_P)
    def _():
        nakv_s_ref[0] = na_k.astype(BF16)
        nakv_s_ref[1] = na_v.astype(BF16)
        gkv_s_ref[0] = gk.astype(BF16)
        gkv_s_ref[1] = gv.astype(BF16)


def _k1_call(layer, x, mods_l, g1, w_in_bf, qg, kg, gm, cos, sin, na_glob, gqa_glob):
    rope_blk = lambda i: (jnp.where(i < NB_P, 0, 1 + (i - NB_P) % SUB_S), 0)
    ctx_blk = lambda i: (jnp.minimum(i, NB_P - 1), layer, 0, 0, 0)
    lat_blk = lambda i: (_latent_block(i) // SUB_S, 0, _latent_block(i) % SUB_S, 0)
    row256 = pl.BlockSpec((ROWS, GROUP_W), lambda i: (i, 0))
    in_specs = [
        pl.BlockSpec((ROWS, D_MODEL), lambda i: (i, 0)),
        pl.BlockSpec((None, 1, 6 * D_MODEL), lambda i: (_mod_row(i), 0, 0)),
        _full((1, D_MODEL)),
        _full((D_MODEL, PROJ_W)),
        _full((1, GROUP_W)),
        _full((1, GROUP_W)),
        _full((GROUP_W, GROUP_W)),
        pl.BlockSpec((ROWS, GROUP_W), rope_blk),
        pl.BlockSpec((ROWS, GROUP_W), rope_blk),
    ]
    in_specs += [pl.BlockSpec(memory_space=pl.ANY)] * 2
    args = [x, mods_l, g1, w_in_bf, qg, kg, gm, cos, sin, na_glob, gqa_glob]
    return pl.pallas_call(
        _k1_body,
        grid=(NB_TOK,),
        in_specs=in_specs,
        out_specs=[
            row256, row256, row256, row256,
            pl.BlockSpec((None, None, 2, ROWS, GROUP_W), ctx_blk),
            pl.BlockSpec((None, None, 2, ROWS, KV_W), ctx_blk),
            pl.BlockSpec((None, 2, ROWS, GROUP_W), lat_blk),
            pl.BlockSpec((None, 2, ROWS, KV_W), lat_blk),
        ],
        out_shape=[
            jax.ShapeDtypeStruct((T_ALL, GROUP_W), BF16),
            jax.ShapeDtypeStruct((T_ALL, GROUP_W), BF16),
            jax.ShapeDtypeStruct((T_ALL, GROUP_W), BF16),
            jax.ShapeDtypeStruct((T_ALL, GROUP_W), F32),
            jax.ShapeDtypeStruct((BATCH, DEPTH, 2, SEQ, GROUP_W), F32),
            jax.ShapeDtypeStruct((BATCH, DEPTH, 2, SEQ, KV_W), F32),
            jax.ShapeDtypeStruct((DEC_BATCH, 2, DEC_SEQ, GROUP_W), BF16),
            jax.ShapeDtypeStruct((DEC_BATCH, 2, DEC_SEQ, KV_W), BF16),
        ],
        input_output_aliases={9: 4, 10: 5},
        compiler_params=pltpu.CompilerParams(vmem_limit_bytes=VMEM_LIMIT),
        name="k1_norm_proj",
    )(*args)


def _linmix_body(seq, uf_ref, up_ref, bc_ref, bs_ref, cl_ref, sl_ref, wf_ref, band_ref, icnt_ref,
                 wp_ref, ps_ref, a_ref, d_ref):
    u = uf_ref[...]
    t1 = _dot(u, bc_ref[...]).astype(BF16)
    t2 = _dot(u, bs_ref[...]).astype(BF16)
    f = (_dot(cl_ref[...], t1) - _dot(sl_ref[...], t2)) * (1.0 / float(np.sqrt(seq * HEAD_DIM)))
    a_ref[...] = _dot(f.astype(BF16), wf_ref[...]).astype(BF16)

    up = up_ref[...]
    up_hi, up_lo = _split_bf16(up)
    group = lax.broadcasted_iota(I32, up.shape, 1) // HEAD_DIM
    win_sum = jnp.zeros_like(up)
    for gi in range(len(POOL_WINDOWS)):
        s = _dot(band_ref[gi], up_hi) + _dot(band_ref[gi], up_lo)
        win_sum = jnp.where(group == gi, s, win_sum)
    delta = win_sum * icnt_ref[...] - up
    d_ref[...] = (_dot(delta.astype(BF16), wp_ref[...]) * ps_ref[...]).astype(BF16)


def _linmix_call(uf, up, seq, first_blk, n_blk, tabs, wf_bf, wp_bd_bf, ps, name):
    bc, bs, cl, sl, bands, icnt = tabs
    blk = pl.BlockSpec((seq, GROUP_W), lambda i: (first_blk + i, 0))
    oblk = pl.BlockSpec((seq, GROUP_W), lambda i: (i, 0))
    return pl.pallas_call(
        functools.partial(_linmix_body, seq),
        grid=(n_blk,),
        in_specs=[
            blk, blk,
            _full((GROUP_W, GROUP_W)), _full((GROUP_W, GROUP_W)),
            _full((seq, seq)), _full((seq, seq)),
            _full((GROUP_W, GROUP_W)),
            _full((len(POOL_WINDOWS), seq, seq)),
            _full((seq, GROUP_W)),
            _full((GROUP_W, GROUP_W)),
            _full((1, GROUP_W)),
        ],
        out_specs=[oblk, oblk],
        out_shape=[jax.ShapeDtypeStruct((n_blk * seq, GROUP_W), BF16)] * 2,
        compiler_params=pltpu.CompilerParams(vmem_limit_bytes=VMEM_LIMIT),
        name=name,
    )(uf, up, bc, bs, cl, sl, wf_bf, bands, icnt, wp_bd_bf, ps)


def _attend(q, k, v):
    s = _dot_nt(q, k) * (HEAD_DIM ** -0.5)
    p, l = _softmax_rows(s)
    return _dot(p.astype(BF16), v) / l


def _attn_p_body(naq_ref, nakv_ref, gq_ref, gkv_ref, b_ref, c_ref):
    for h in range(N_HEADS):
        sl = slice(h * HEAD_DIM, (h + 1) * HEAD_DIM)
        k = nakv_ref[0, :, sl].astype(BF16)
        v = nakv_ref[1, :, sl].astype(BF16)
        b_ref[:, sl] = _attend(naq_ref[:, sl], k, v).astype(BF16)
    for h in range(N_HEADS):
        sl = slice(h * HEAD_DIM, (h + 1) * HEAD_DIM)
        kvh = h // 2
        ksl = slice(kvh * HEAD_DIM, (kvh + 1) * HEAD_DIM)
        k = gkv_ref[0, :, ksl].astype(BF16)
        v = gkv_ref[1, :, ksl].astype(BF16)
        c_ref[:, sl] = _attend(gq_ref[:, sl], k, v).astype(BF16)


def _attn_p_call(layer, naq, na_glob, gq, gqa_glob):
    row = pl.BlockSpec((SEQ, GROUP_W), lambda i: (i, 0))
    return pl.pallas_call(
        _attn_p_body,
        grid=(BATCH,),
        in_specs=[
            row,
            pl.BlockSpec((None, None, 2, SEQ, GROUP_W), lambda i: (i, layer, 0, 0, 0)),
            row,
            pl.BlockSpec((None, None, 2, SEQ, KV_W), lambda i: (i, layer, 0, 0, 0)),
        ],
        out_specs=[row, row],
        out_shape=[jax.ShapeDtypeStruct((T_P, GROUP_W), BF16)] * 2,
        name="attn_ctx",
    )(naq, na_glob, gq, gqa_glob)


N_DR = 2 * NA_KH - 1
N_DC = 2 * NA_KW - 1


def _nabias_body(rpb_ref, o_ref):
    lh = pl.program_id(0)
    q = lax.broadcasted_iota(I32, (GRID_W, LANES), 0)
    j = lax.broadcasted_iota(I32, (GRID_W, LANES), 1)
    k = j % GRID_W
    second = j >= GRID_W
    c0 = jnp.clip(q - NA_KW // 2, 0, GRID_W - NA_KW)
    valid = (k >= c0) & (k < c0 + NA_KW)
    diff = k - q + (NA_KW - 1)
    for dr in range(N_DR - 1):
        acc = jnp.zeros((GRID_W, LANES), F32)
        for dc in range(N_DC):
            val = jnp.where(second, rpb_ref[lh, (dr + 1) * N_DC + dc], rpb_ref[lh, dr * N_DC + dc])
            acc = jnp.where(diff == dc, val, acc)
        o_ref[dr] = jnp.where(valid, acc, NEG_INF)


def _nabias_call(rpb_flat):
    n = rpb_flat.shape[0]
    return pl.pallas_call(
        _nabias_body,
        grid=(n,),
        in_specs=[pl.BlockSpec(memory_space=pltpu.SMEM)],
        out_specs=pl.BlockSpec((None, N_DR - 1, GRID_W, LANES), lambda i: (i, 0, 0, 0)),
        out_shape=jax.ShapeDtypeStruct((n, N_DR - 1, GRID_W, LANES), F32),
        name="na_bias_tiles",
    )(rpb_flat)


N_WIN = NA_KH * GRID_W


def _na_s_body(q_ref, kv_ref, ckv_ref, bias_ref, o_ref):
    r = pl.program_id(1)
    r0 = jnp.clip(r - NA_KH // 2, 0, GRID_ROWS - NA_KH)
    start = pl.multiple_of(r0 * GRID_W, GRID_W)
    dr0 = r0 - r + (NA_KH - 1)
    scale = HEAD_DIM ** -0.5
    for h in range(N_HEADS):
        sl = slice(h * HEAD_DIM, (h + 1) * HEAD_DIM)
        q = q_ref[:, sl]
        kw = kv_ref[0, pl.ds(start, N_WIN), sl]
        vw = kv_ref[1, pl.ds(start, N_WIN), sl]
        kc = ckv_ref[0, :, sl].astype(BF16)
        vc = ckv_ref[1, :, sl].astype(BF16)
        bias = jnp.concatenate([bias_ref[h, dr0 + i] for i in range(0, NA_KH, 2)], axis=1)
        s_win = _dot_nt(q, kw) * scale + bias
        s_ctx = _dot_nt(q, kc) * scale
        s = jnp.concatenate([s_win, s_ctx], axis=1)
        p, l = _softmax_rows(s)
        pb = p.astype(BF16)
        o = _dot(pb[:, 0:N_WIN], vw) + _dot(pb[:, N_WIN:], vc)
        o_ref[:, sl] = (o / l).astype(BF16)


def _na_s_call(naq, nakv_s, cache_na_l, bias_l):
    first = T_P // GRID_W
    return pl.pallas_call(
        _na_s_body,
        grid=(DEC_BATCH, GRID_ROWS),
        in_specs=[
            pl.BlockSpec((GRID_W, GROUP_W), lambda b, r: (first + b * GRID_ROWS + r, 0)),
            pl.BlockSpec((None, 2, DEC_SEQ, GROUP_W), lambda b, r: (b, 0, 0, 0)),
            pl.BlockSpec((None, 2, PAST_LEN, GROUP_W), lambda b, r: (b, 0, 0, 0)),
            _full((N_HEADS, N_DR - 1, GRID_W, LANES)),
        ],
        out_specs=pl.BlockSpec((GRID_W, GROUP_W), lambda b, r: (b * GRID_ROWS + r, 0)),
        out_shape=jax.ShapeDtypeStruct((T_S, GROUP_W), BF16),
        compiler_params=pltpu.CompilerParams(vmem_limit_bytes=VMEM_LIMIT),
        name="attn_na_latent",
    )(naq, nakv_s, cache_na_l, bias_l)


GQ_ROWS = 128


def _gqa_s_body(q_ref, kv_ref, ckv_ref, o_ref):
    scale = HEAD_DIM ** -0.5
    for h in range(N_HEADS):
        sl = slice(h * HEAD_DIM, (h + 1) * HEAD_DIM)
        kvh = h // 2
        ksl = slice(kvh * HEAD_DIM, (kvh + 1) * HEAD_DIM)
        q = q_ref[:, sl]
        kl = kv_ref[0, :, ksl]
        vl = kv_ref[1, :, ksl]
        kc = ckv_ref[0, :, ksl].astype(BF16)
        vc = ckv_ref[1, :, ksl].astype(BF16)
        s = jnp.concatenate([_dot_nt(q, kl), _dot_nt(q, kc)], axis=1) * scale
        p, l = _softmax_rows(s)
        pb = p.astype(BF16)
        o = _dot(pb[:, 0:DEC_SEQ], vl) + _dot(pb[:, DEC_SEQ:], vc)
        o_ref[:, sl] = (o / l).astype(BF16)


def _gqa_s_call(gq, gkv_s, cache_gqa_l):
    nq = DEC_SEQ // GQ_ROWS
    first = T_P // GQ_ROWS
    return pl.pallas_call(
        _gqa_s_body,
        grid=(DEC_BATCH, nq),
        in_specs=[
            pl.BlockSpec((GQ_ROWS, GROUP_W), lambda b, j: (first + b * nq + j, 0)),
            pl.BlockSpec((None, 2, DEC_SEQ, KV_W), lambda b, j: (b, 0, 0, 0)),
            pl.BlockSpec((None, 2, PAST_LEN, KV_W), lambda b, j: (b, 0, 0, 0)),
        ],
        out_specs=pl.BlockSpec((GQ_ROWS, GROUP_W), lambda b, j: (b * nq + j, 0)),
        out_shape=jax.ShapeDtypeStruct((T_S, GROUP_W), BF16),
        compiler_params=pltpu.CompilerParams(vmem_limit_bytes=VMEM_LIMIT),
        name="attn_gqa_latent",
    )(gq, gkv_s, cache_gqa_l)


def _k3_body(x_ref, ap_ref, bp_ref, cp_ref, dp_ref, as_ref, bs_ref, cs_ref, ds_ref, mod_ref, wout_ref, g2_ref,
             rwh_ref, rwl_ref, rb_ref, xmid_ref, h2_ref, te_ref, tg_ref, pr_ref, cnt_ref):
    is_ctx = pl.program_id(0) < NB_P
    acc = jnp.zeros((ROWS, D_MODEL), F32)
    for g, (p_ref, s_ref) in enumerate(((ap_ref, as_ref), (bp_ref, bs_ref), (cp_ref, cs_ref), (dp_ref, ds_ref))):
        mixed = jnp.where(is_ctx, p_ref[...], s_ref[...])
        acc += _dot(mixed, wout_ref[g * GROUP_W:(g + 1) * GROUP_W, :])
    gate1 = mod_ref[:, 2 * D_MODEL:3 * D_MODEL]
    sh2 = mod_ref[:, 3 * D_MODEL:4 * D_MODEL]
    sc2 = mod_ref[:, 4 * D_MODEL:5 * D_MODEL]
    xm = x_ref[...] + gate1 * acc
    xmid_ref[...] = xm
    h2 = _rmsnorm_rows(xm, g2_ref[...]) * (1.0 + sc2) + sh2
    hi, lo = _split_bf16(h2)
    h2_ref[...] = hi
    logits = _dot(hi, rwh_ref[...]) + _dot(lo, rwh_ref[...]) + _dot(hi, rwl_ref[...]) + rb_ref[...]

    lane = lax.broadcasted_iota(I32, logits.shape, 1)
    cur = logits
    vals, idxs = [], []
    for _ in range(TOP_K):
        m = jnp.max(cur, axis=-1, keepdims=True)
        idx = jnp.min(jnp.where(cur == m, lane, LANES), axis=-1, keepdims=True)
        vals.append(m)
        idxs.append(idx)
        cur = jnp.where(lane == idx, -jnp.inf, cur)
    exps = [jnp.exp(v - vals[0]) for v in vals]
    tot = exps[0] + exps[1] + exps[2] + exps[3]
    te = jnp.zeros(logits.shape, I32)
    tg = jnp.zeros(logits.shape, F32)
    for k in range(TOP_K):
        te = jnp.where(lane == k, idxs[k], te)
        tg = jnp.where(lane == k, exps[k] / tot, tg)
    te_ref[...] = te
    tg_ref[...] = tg

    member = jnp.zeros(logits.shape, F32)
    for k in range(TOP_K):
        member = jnp.where(lane == idxs[k], 1.0, member)
    rr = lax.broadcasted_iota(I32, (ROWS, ROWS), 0)
    cc = lax.broadcasted_iota(I32, (ROWS, ROWS), 1)
    earlier = jnp.where(cc < rr, 1.0, 0.0).astype(BF16)
    prefix = _dot(earlier, member.astype(BF16))
    cnt = jnp.sum(member, axis=0, keepdims=True)
    er = lax.broadcasted_iota(I32, (LANES, LANES), 0)
    ec = lax.broadcasted_iota(I32, (LANES, LANES), 1)
    lower = jnp.where(er < ec, 1.0, 0.0).astype(BF16)
    loc = _dot(jnp.broadcast_to(cnt, (SUB, LANES)).astype(BF16), lower)[0:1, :]
    pr = jnp.zeros(logits.shape, F32)
    for k in range(TOP_K):
        hit = lane == idxs[k]
        rank = jnp.sum(jnp.where(hit, prefix, 0.0), axis=-1, keepdims=True)
        start = jnp.sum(jnp.where(hit, loc, 0.0), axis=-1, keepdims=True)
        pr = jnp.where(lane == k, start + rank, pr)
        pr = jnp.where(lane == TOP_K + k, rank, pr)
    pr_ref[...] = pr.astype(I32)
    cnt_ref[...] = cnt.astype(I32)


def _k3_call(x, mix_p, mix_s, mods_l, w_out_bf, g2, rwh, rwl, rb):
    row_p = pl.BlockSpec((ROWS, GROUP_W), lambda i: (jnp.minimum(i, NB_P - 1), 0))
    row_s = pl.BlockSpec((ROWS, GROUP_W), lambda i: (_latent_block(i), 0))
    rowd = pl.BlockSpec((ROWS, D_MODEL), lambda i: (i, 0))
    row128 = pl.BlockSpec((ROWS, LANES), lambda i: (i, 0))
    return pl.pallas_call(
        _k3_body,
        grid=(NB_TOK,),
        in_specs=[
            rowd, row_p, row_p, row_p, row_p, row_s, row_s, row_s, row_s,
            pl.BlockSpec((None, 1, 6 * D_MODEL), lambda i: (_mod_row(i), 0, 0)),
            _full((D_MODEL, D_MODEL)),
            _full((1, D_MODEL)),
            _full((D_MODEL, LANES)), _full((D_MODEL, LANES)), _full((1, LANES)),
        ],
        out_specs=[rowd, rowd, row128, row128, row128,
                   pl.BlockSpec((None, 1, LANES), lambda i: (i, 0, 0))],
        out_shape=[
            jax.ShapeDtypeStruct((T_ALL, D_MODEL), F32),
            jax.ShapeDtypeStruct((T_ALL, D_MODEL), BF16),
            jax.ShapeDtypeStruct((T_ALL, LANES), I32),
            jax.ShapeDtypeStruct((T_ALL, LANES), F32),
            jax.ShapeDtypeStruct((T_ALL, LANES), I32),
            jax.ShapeDtypeStruct((NB_TOK, 1, LANES), I32),
        ],
        compiler_params=pltpu.CompilerParams(vmem_limit_bytes=VMEM_LIMIT),
        name="k3_out_router",
    )(x, *mix_p, *mix_s, mods_l, w_out_bf, g2, rwh, rwl, rb)


TAB_CNT, TAB_LOC, TAB_BASE, TAB_LSTAGE, TAB_SOFF, TAB_TOTAL, TAB_NCHUNK = range(7)


def _routing_tables(cnt):
    c = cnt.reshape(NB_TOK, LANES)[:, :N_EXPERTS]
    counts = jnp.sum(c, axis=0)
    padded = (counts + (SEG - 1) + (MOE_ROWS - 1)) // MOE_ROWS * MOE_ROWS
    pend = jnp.cumsum(padded)
    pstart = pend - padded
    base = pstart[None, :] + jnp.cumsum(c, axis=0) - c
    loc = jnp.cumsum(c, axis=1) - c
    off = base % SUB
    nch = jnp.where(c > 0, (off + c + SEG - 1) // SEG, 0)
    lstage = (jnp.cumsum(nch, axis=1) - nch) * SEG
    soff = lstage + off
    total = jnp.sum(nch, axis=1, keepdims=True) * SEG

    def pad(t):
        return jnp.zeros((NB_TOK, LANES), I32).at[:, :t.shape[1]].set(t.astype(I32))

    tab = jnp.stack([pad(c), pad(loc), pad(base), pad(lstage), pad(soff), pad(total), pad(nch),
                     jnp.zeros((NB_TOK, LANES), I32)], axis=1)
    ztab = jnp.zeros((SUB, LANES), I32).at[0, :N_EXPERTS].set((pstart + counts).astype(I32))
    ztab = ztab.at[1, :N_EXPERTS].set(pend.astype(I32))
    ztab = ztab.at[2, 0].set((pend[-1] // MOE_ROWS).astype(I32))
    first_blk = (pstart // MOE_ROWS).astype(I32)
    n_blk = (padded // MOE_ROWS).astype(I32)
    n_used = (pend[-1] // MOE_ROWS).astype(I32).reshape(1)
    return tab, ztab, first_blk, n_blk, n_used


def _dispatch_body(tab_ref, ztab_ref, h2_ref, pr_ref, xs_ref, stage0, stage1, zbuf, pending, sems):
    b = pl.program_id(0)
    sem = sems.at[0]

    def seg_copy(src, src_row, dst_row, sem=sem):
        return pltpu.make_async_copy(
            src.at[pl.ds(pl.multiple_of(src_row * SUB, SUB), SEG * SUB)],
            xs_ref.at[pl.ds(pl.multiple_of(dst_row * SUB, SUB), SEG * SUB)], sem)

    def drain(n, sem=sem):
        def body(j, carry):
            seg_copy(zbuf, 0, 0, sem).wait()
            return carry
        lax.fori_loop(0, n, body, 0)

    @pl.when(b == 0)
    def _():
        zbuf[...] = jnp.zeros(zbuf.shape, F32)
        stage0[pl.ds(BLOCK_ASSIGN * SUB, SEG * SUB), :] = jnp.zeros((SEG * SUB, LANES), F32)
        stage1[pl.ds(BLOCK_ASSIGN * SUB, SEG * SUB), :] = jnp.zeros((SEG * SUB, LANES), F32)

        def row_copy(dst_row):
            return pltpu.make_async_copy(
                zbuf.at[pl.ds(0, SUB)], xs_ref.at[pl.ds(pl.multiple_of(dst_row * SUB, SUB), SUB)], sem)

        def per_expert(e, carry):
            n_seg, n_row = carry
            end = ztab_ref[0, e]
            length = ztab_ref[1, e] - end
            n = lax.shift_right_logical(length, SEG_SHIFT)
            rem = length & (SEG - 1)

            def one(j, c):
                seg_copy(zbuf, 0, end + j * SEG).start()
                return c
            lax.fori_loop(0, n, one, 0)

            def one_row(j, c):
                row_copy(end + n * SEG + j).start()
                return c
            lax.fori_loop(0, rem, one_row, 0)
            return n_seg + n, n_row + rem
        n_seg, n_row = lax.fori_loop(0, N_EXPERTS, per_expert, (0, 0))
        drain(n_seg)

        def row_wait(j, c):
            row_copy(0).wait()
            return c
        lax.fori_loop(0, n_row, row_wait, 0)

        first_free = ztab_ref[2, 0]

        def block_copy(blk):
            return pltpu.make_async_copy(
                zbuf, xs_ref.at[pl.ds(pl.multiple_of(blk * (MOE_ROWS * SUB), MOE_ROWS * SUB), MOE_ROWS * SUB)], sem)

        def fill(j, c):
            block_copy(first_free + j).start()
            return c
        lax.fori_loop(0, N_BLOCKS - first_free, fill, 0)

        def fill_wait(j, c):
            block_copy(0).wait()
            return c
        lax.fori_loop(0, N_BLOCKS - first_free, fill_wait, 0)

    pos_t = pr_ref[...].astype(F32).T
    h2 = h2_ref[...]

    def sort(stage):
        for c in range(BLOCK_ASSIGN // ROWS):
            p = (lax.broadcasted_iota(I32, (ROWS, ROWS), 0) + c * ROWS).astype(F32)
            sel = p == pos_t[0:1, :]
            for k in range(1, TOP_K):
                sel = sel | (p == pos_t[k:k + 1, :])
            rows = _dot(jnp.where(sel, 1.0, 0.0).astype(BF16), h2)
            for s in range(ROW_TILES):
                stage[pl.ds(c * ROWS * SUB + s, ROWS, stride=SUB), :] = rows[:, s * LANES:(s + 1) * LANES]

    def send(stage, sem):
        def per_expert(e, tot):
            loc = tab_ref[TAB_LOC, e]
            base = tab_ref[TAB_BASE, e]
            n = lax.shift_right_logical(tab_ref[TAB_CNT, e] + (SEG - 1), SEG_SHIFT)

            def one(j, carry):
                seg_copy(stage, loc + j * SEG, base + j * SEG, sem).start()
                return carry
            lax.fori_loop(0, n, one, 0)
            return tot + n
        return lax.fori_loop(0, N_EXPERTS, per_expert, 0)

    for slot, stage in enumerate((stage0, stage1)):
        @pl.when((b & 1) == slot)
        def _(stage=stage, slot=slot):
            sort(stage)

            @pl.when(b > 0)
            def _():
                drain(pending[0], sems.at[1 - slot])
            n_sent = send(stage, sems.at[slot])
            pending[0] = n_sent

            @pl.when(b == NB_TOK - 1)
            def _():
                drain(n_sent, sems.at[slot])


def _dispatch_call(tab, ztab, h2, pr):
    return pl.pallas_call(
        _dispatch_body,
        grid=(NB_TOK,),
        in_specs=[
            pl.BlockSpec((None, SUB, LANES), lambda i: (i, 0, 0), memory_space=pltpu.SMEM),
            pl.BlockSpec(memory_space=pltpu.SMEM),
            pl.BlockSpec((ROWS, D_MODEL), lambda i: (i, 0)),
            pl.BlockSpec((ROWS, LANES), lambda i: (i, 0)),
        ],
        out_specs=pl.BlockSpec(memory_space=pl.ANY),
        out_shape=jax.ShapeDtypeStruct((N_SORTED * ROW_TILES, LANES), F32),
        scratch_shapes=[
            pltpu.VMEM(((BLOCK_ASSIGN + SEG) * SUB, LANES), F32),
            pltpu.VMEM(((BLOCK_ASSIGN + SEG) * SUB, LANES), F32),
            pltpu.VMEM((MOE_ROWS * SUB, LANES), F32),
            pltpu.SMEM((1,), I32),
            pltpu.SemaphoreType.DMA((2,)),
        ],
        compiler_params=pltpu.CompilerParams(vmem_limit_bytes=VMEM_LIMIT),
        name="moe_dispatch",
    )(tab, ztab, h2, pr)


HALF_FF = D_FF // 2


def _moe_body(fb_ref, nb_ref, nu_ref, xs_ref, wgu_ref, bgu_ref, wd_ref, bd_ref, yp_ref,
              wgu_bf, wd_bf, wd_scr, xbuf0, xbuf1, ybuf0, ybuf1, xsems, ysems):
    e = pl.program_id(0)
    first = fb_ref[e]
    n_blk = nb_ref[e]
    n_used = nu_ref[0]
    xbufs = (xbuf0, xbuf1)
    ybufs = (ybuf0, ybuf1)

    def x_copy(g, slot):
        src = xs_ref.at[pl.ds(pl.multiple_of(g * (MOE_ROWS * SUB), MOE_ROWS * SUB), MOE_ROWS * SUB)]
        return pltpu.make_async_copy(src, xbufs[slot], xsems.at[slot])

    def y_copy(g, slot):
        dst = yp_ref.at[pl.ds(pl.multiple_of(g * MOE_ROWS, MOE_ROWS), MOE_ROWS)]
        return pltpu.make_async_copy(ybufs[slot], dst, ysems.at[slot])

    @pl.when(e == 0)
    def _():
        x_copy(0, 0).start()

    @pl.when(n_blk > 0)
    def _():
        for c in range(4):
            cs = slice(c * 512, (c + 1) * 512)
            wgu_bf[:, cs] = wgu_ref[:, cs].astype(BF16)
        for c in range(ROW_TILES):
            cs = slice(c * LANES, (c + 1) * LANES)
            wd_scr[pl.ds(0, HALF_FF, stride=2), :] = wd_ref[0:HALF_FF, cs]
            wd_scr[pl.ds(1, HALF_FF, stride=2), :] = wd_ref[HALF_FF:D_FF, cs]
            wd_bf[:, cs] = wd_scr[...].astype(BF16)

    def ffn(x_ref):
        x = jnp.concatenate(
            [x_ref[pl.ds(s, MOE_ROWS, stride=ROW_TILES), :] for s in range(ROW_TILES)], axis=1)
        xb = x.astype(BF16)
        ha = _dot(xb, wgu_bf[:, 0:D_FF]) + bgu_ref[:, 0:D_FF]
        hb = _dot(xb, wgu_bf[:, D_FF:2 * D_FF]) + bgu_ref[:, D_FF:2 * D_FF]
        even = (lax.broadcasted_iota(I32, ha.shape, 1) % 2) == 0
        gate = jnp.where(even, ha, pltpu.roll(hb, 1, 1))
        up = jnp.where(even, pltpu.roll(ha, D_FF - 1, 1), hb)
        gate = jnp.minimum(gate, SWIGLU_LIMIT)
        up = jnp.clip(up, -SWIGLU_LIMIT, SWIGLU_LIMIT)
        glu = gate * (1.0 / (1.0 + jnp.exp(-gate * SWIGLU_ALPHA)))
        act = ((up + 1.0) * glu).astype(BF16)
        return _dot(act, wd_bf[...]) + bd_ref[...]

    def block(j, carry):
        g = first + j
        for slot in range(2):
            @pl.when((g & 1) == slot)
            def _(slot=slot):
                x_copy(g, slot).wait()

                @pl.when(g + 1 < n_used)
                def _():
                    x_copy(g + 1, 1 - slot).start()
                y = ffn(xbufs[slot])

                @pl.when(g >= 2)
                def _():
                    y_copy(g - 2, slot).wait()
                ybufs[slot][...] = y
                y_copy(g, slot).start()
        return carry
    lax.fori_loop(0, n_blk, block, 0)

    @pl.when(e == N_EXPERTS - 1)
    def _():
        for slot in range(2):
            @pl.when(((n_used - 1) & 1) == slot)
            def _(slot=slot):
                @pl.when(n_used >= 2)
                def _():
                    y_copy(n_used - 2, 1 - slot).wait()
                y_copy(n_used - 1, slot).wait()
        ybuf0[...] = jnp.zeros(ybuf0.shape, F32)

        def fill(j, carry):
            y_copy(n_used + j, 0).start()
            return carry
        lax.fori_loop(0, N_BLOCKS - n_used, fill, 0)

        def fill_wait(j, carry):
            y_copy(0, 0).wait()
            return carry
        lax.fori_loop(0, N_BLOCKS - n_used, fill_wait, 0)


def _moe_call(layer, first_blk, n_blk, n_used, xs, w_gu, b_gu4, w_down, b_down4):
    wmap = lambda e, fb, nb, nu: (layer, e, 0, 0)
    grid_spec = pltpu.PrefetchScalarGridSpec(
        num_scalar_prefetch=3,
        grid=(N_EXPERTS,),
        in_specs=[
            pl.BlockSpec(memory_space=pl.ANY),
            pl.BlockSpec((None, None, D_MODEL, 2 * D_FF), wmap),
            pl.BlockSpec((None, None, 1, 2 * D_FF), wmap),
            pl.BlockSpec((None, None, D_FF, D_MODEL), wmap),
            pl.BlockSpec((None, None, 1, D_MODEL), wmap),
        ],
        out_specs=pl.BlockSpec(memory_space=pl.ANY),
        scratch_shapes=[
            pltpu.VMEM((D_MODEL, 2 * D_FF), BF16),
            pltpu.VMEM((D_FF, D_MODEL), BF16),
            pltpu.VMEM((D_FF, LANES), F32),
            pltpu.VMEM((MOE_ROWS * SUB, LANES), F32),
            pltpu.VMEM((MOE_ROWS * SUB, LANES), F32),
            pltpu.VMEM((MOE_ROWS, D_MODEL), F32),
            pltpu.VMEM((MOE_ROWS, D_MODEL), F32),
            pltpu.SemaphoreType.DMA((2,)),
            pltpu.SemaphoreType.DMA((2,)),
        ],
    )
    return pl.pallas_call(
        _moe_body,
        grid_spec=grid_spec,
        out_shape=jax.ShapeDtypeStruct((N_SORTED, D_MODEL), F32),
        compiler_params=pltpu.CompilerParams(vmem_limit_bytes=VMEM_LIMIT),
        name="moe_experts",
    )(first_blk, n_blk, n_used, xs, w_gu, b_gu4, w_down, b_down4)


def _combine_body(final, tab_ref, tabn_ref, tabv_ref, yp_ref, te_ref, tg_ref, pr_ref, xmid_ref, mod_ref, fg_ref,
                  o_ref, stage0, stage1, acc, sems):
    b = pl.program_id(0)
    stages = (stage0, stage1)

    def chunk_copy(src_row, dst_row, slot):
        return pltpu.make_async_copy(
            yp_ref.at[pl.ds(pl.multiple_of(src_row, SUB), SEG)],
            stages[slot].at[pl.ds(pl.multiple_of(dst_row, SUB), SEG)], sems.at[slot])

    def fetch(tab, slot):
        def per_expert(e, carry):
            base = tab[TAB_BASE, e]
            off = base & (SUB - 1)
            lstage = tab[TAB_LSTAGE, e]

            def one(j, c):
                chunk_copy(base - off + j * SEG, lstage + j * SEG, slot).start()
                return c
            lax.fori_loop(0, tab[TAB_NCHUNK, e], one, 0)
            return carry
        lax.fori_loop(0, N_EXPERTS, per_expert, 0)

        total = tab[TAB_TOTAL, 0]

        def zero_tail(j, c):
            stages[slot][pl.ds(pl.multiple_of(total + j * SEG, SEG), SEG), :] = jnp.zeros((SEG, D_MODEL), F32)
            return c
        lax.fori_loop(0, lax.shift_right_logical(STAGE_ROWS - total, SEG_SHIFT), zero_tail, 0)

    @pl.when(b == 0)
    def _():
        fetch(tab_ref, 0)

    for slot in range(2):
        @pl.when(((b & 1) == slot) & (b + 1 < NB_TOK))
        def _(slot=slot):
            fetch(tabn_ref, 1 - slot)

    total = tab_ref[TAB_TOTAL, 0]
    for slot in range(2):
        @pl.when((b & 1) == slot)
        def _(slot=slot):
            def drain(j, c):
                chunk_copy(0, 0, slot).wait()
                return c
            lax.fori_loop(0, lax.shift_right_logical(total, SEG_SHIFT), drain, 0)

    lane = lax.broadcasted_iota(I32, (ROWS, LANES), 1)
    te = te_ref[...]
    pr = pr_ref[...]
    soff = tabv_ref[TAB_SOFF:TAB_SOFF + 1, :].astype(F32)
    packed = jnp.where((lane >= TOP_K) & (lane < 2 * TOP_K), pltpu.roll(tg_ref[...], TOP_K, 1), 0.0)
    sp = []
    for k in range(TOP_K):
        seg_start = jnp.sum(jnp.where(lane == te[:, k:k + 1], soff, 0.0), axis=-1, keepdims=True)
        spk = seg_start + pr[:, TOP_K + k:TOP_K + k + 1].astype(F32)
        sp.append(spk)
        packed = jnp.where(lane == k, spk, packed)
    packed_t = packed.T

    def unsort(stage, c):
        pcol = (lax.broadcasted_iota(I32, (ROWS, ROWS), 1) + c * ROWS).astype(F32)
        sel = pcol == sp[0]
        for k in range(1, TOP_K):
            sel = sel | (pcol == sp[k])
        prow = (lax.broadcasted_iota(I32, (ROWS, ROWS), 0) + c * ROWS).astype(F32)
        gmat = jnp.zeros((ROWS, ROWS), F32)
        for k in range(TOP_K):
            gmat = jnp.where(prow == packed_t[k:k + 1, :], packed_t[TOP_K + k:TOP_K + k + 1, :], gmat)
        row_gate = jnp.sum(gmat, axis=-1, keepdims=True)
        rows = (stage[c * ROWS:(c + 1) * ROWS, :] * row_gate).astype(BF16)
        return _dot(jnp.where(sel, 1.0, 0.0).astype(BF16), rows)

    n_always = BLOCK_ASSIGN // ROWS
    for slot in range(2):
        @pl.when((b & 1) == slot)
        def _(slot=slot):
            out = unsort(stages[slot], 0)
            for c in range(1, n_always):
                out += unsort(stages[slot], c)
            acc[...] = out
            for c in range(n_always, STAGE_ROWS // ROWS):
                @pl.when(c * ROWS < total)
                def _(c=c):
                    acc[...] += unsort(stages[slot], c)

    gate2 = mod_ref[:, 5 * D_MODEL:6 * D_MODEL]
    x = xmid_ref[...] + gate2 * acc[...]
    if final:
        x = _rmsnorm_rows(x, fg_ref[...])
    o_ref[...] = x


def _combine_call(tab, yp, te, tg, pr, xmid, mods_l, final_g, final):
    rowd = pl.BlockSpec((ROWS, D_MODEL), lambda i: (i, 0))
    row128 = pl.BlockSpec((ROWS, LANES), lambda i: (i, 0))
    return pl.pallas_call(
        functools.partial(_combine_body, final),
        grid=(NB_TOK,),
        in_specs=[
            pl.BlockSpec((None, SUB, LANES), lambda i: (i, 0, 0), memory_space=pltpu.SMEM),
            pl.BlockSpec((None, SUB, LANES), lambda i: (jnp.minimum(i + 1, NB_TOK - 1), 0, 0),
                         memory_space=pltpu.SMEM),
            pl.BlockSpec((None, SUB, LANES), lambda i: (i, 0, 0)),
            pl.BlockSpec(memory_space=pl.ANY),
            row128, row128, row128,
            rowd,
            pl.BlockSpec((None, 1, 6 * D_MODEL), lambda i: (_mod_row(i), 0, 0)),
            _full((1, D_MODEL)),
        ],
        out_specs=rowd,
        out_shape=jax.ShapeDtypeStruct((T_ALL, D_MODEL), F32),
        scratch_shapes=[
            pltpu.VMEM((STAGE_ROWS, D_MODEL), F32),
            pltpu.VMEM((STAGE_ROWS, D_MODEL), F32),
            pltpu.VMEM((ROWS, D_MODEL), F32),
            pltpu.SemaphoreType.DMA((2,)),
        ],
        compiler_params=pltpu.CompilerParams(vmem_limit_bytes=VMEM_LIMIT),
        name="moe_combine",
    )(tab, tab, tab, yp, te, tg, pr, xmid, mods_l, final_g)


def kernel(x_prompt, x_sample, cache_na_kv, cache_gqa_kv, c, c_ctx, norm1_g, norm2_g, w_mod, b_mod,
           w_in, w_fourier, na_rpb, q_norm_g, k_norm_g, w_pool, pool_scale, w_out, router_w, router_b,
           w_gu, b_gu, w_down, b_down, final_g):
    def lin_tables(L):
        cl, sl, bc, bs = _dft_tables(L)
        bands, icnt = _pool_tables(L)
        return (jnp.asarray(bc).astype(BF16), jnp.asarray(bs).astype(BF16),
                jnp.asarray(cl).astype(BF16), jnp.asarray(sl).astype(BF16),
                jnp.asarray(bands).astype(BF16), jnp.asarray(icnt))

    tabs_p = lin_tables(SEQ)
    tabs_s = lin_tables(DEC_SEQ)
    cos_np, sin_np = _rope_tables()
    cos_t, sin_t = jnp.asarray(cos_np), jnp.asarray(sin_np)
    gm = jnp.asarray(_group_mean_matrix(GROUP_W)).astype(BF16)

    cond8 = jnp.zeros((SUB, D_MODEL), F32).at[0].set(c_ctx).at[1:1 + DEC_BATCH].set(c)
    mods = _mods_call(cond8, w_mod, b_mod.reshape(DEPTH, 1, 6 * D_MODEL))
    mods = mods.reshape(DEPTH, SUB, 1, 6 * D_MODEL)

    bias_tiles = _nabias_call(na_rpb.reshape(DEPTH * N_HEADS, N_DR * N_DC))
    bias_tiles = bias_tiles.reshape(DEPTH, N_HEADS, N_DR - 1, GRID_W, LANES)

    w_in_bf = w_in.astype(BF16)
    w_out_bf = w_out.astype(BF16)
    w_f_bf = w_fourier.astype(BF16)
    eye4 = jnp.eye(4, dtype=F32)
    w_pool_bd = jnp.einsum('lgce,gh->lgche', w_pool, eye4).reshape(DEPTH, GROUP_W, GROUP_W).astype(BF16)
    qg = jnp.tile(q_norm_g, (1, N_HEADS)).reshape(DEPTH, 1, GROUP_W)
    kg = jnp.tile(k_norm_g, (1, N_HEADS)).reshape(DEPTH, 1, GROUP_W)
    rw_pad = jnp.zeros((DEPTH, D_MODEL, LANES), F32).at[:, :, :N_EXPERTS].set(router_w)
    rw_hi = rw_pad.astype(BF16)
    rw_lo = (rw_pad - rw_hi.astype(F32)).astype(BF16)
    rb_pad = jnp.full((DEPTH, 1, LANES), NEG_INF, F32).at[:, 0, :N_EXPERTS].set(router_b)
    b_gu4 = b_gu.reshape(DEPTH, N_EXPERTS, 1, 2 * D_FF)
    b_down4 = b_down.reshape(DEPTH, N_EXPERTS, 1, D_MODEL)
    final_g2 = final_g.reshape(1, D_MODEL)
    cache_na = cache_na_kv.reshape(DEC_BATCH, DEPTH, 2, PAST_LEN, GROUP_W)
    cache_gqa = cache_gqa_kv.reshape(DEC_BATCH, DEPTH, 2, PAST_LEN, KV_W)

    x = jnp.concatenate([x_prompt.reshape(T_P, D_MODEL), x_sample.reshape(T_S, D_MODEL)], axis=0)
    na_glob = jnp.zeros((BATCH, DEPTH, 2, SEQ, GROUP_W), F32)
    gqa_glob = jnp.zeros((BATCH, DEPTH, 2, SEQ, KV_W), F32)
    for l in range(DEPTH):
        g1 = norm1_g[l].reshape(1, D_MODEL)
        g2 = norm2_g[l].reshape(1, D_MODEL)
        ps = pool_scale[l].reshape(1, GROUP_W)

        uf, naq, gq, up, na_glob, gqa_glob, nakv_s, gkv_s = _k1_call(
            l, x, mods[l], g1, w_in_bf[l], qg[l], kg[l], gm, cos_t, sin_t, na_glob, gqa_glob)

        a_p, d_p = _linmix_call(uf, up, SEQ, 0, BATCH, tabs_p, w_f_bf[l], w_pool_bd[l], ps, "linmix_ctx")
        a_s, d_s = _linmix_call(uf, up, DEC_SEQ, T_P // DEC_SEQ, DEC_BATCH, tabs_s, w_f_bf[l], w_pool_bd[l], ps,
                                "linmix_latent")
        b_p, c_p = _attn_p_call(l, naq, na_glob, gq, gqa_glob)
        b_s = _na_s_call(naq, nakv_s, cache_na[:, l], bias_tiles[l])
        c_s = _gqa_s_call(gq, gkv_s, cache_gqa[:, l])

        xmid, h2, te, tg, pr, cnt = _k3_call(x, (a_p, b_p, c_p, d_p), (a_s, b_s, c_s, d_s), mods[l], w_out_bf[l],
                                             g2, rw_hi[l], rw_lo[l], rb_pad[l])
        tab, ztab, first_blk, n_blk, n_used = _routing_tables(cnt)
        xs = _dispatch_call(tab, ztab, h2, pr)
        yp = _moe_call(l, first_blk, n_blk, n_used, xs, w_gu, b_gu4, w_down, b_down4)
        x = _combine_call(tab, yp, te, tg, pr, xmid, mods[l], final_g2, l == DEPTH - 1)

    y_prompt = x[:T_P].reshape(BATCH, SEQ, D_MODEL)
    y_sample = x[T_P:].reshape(DEC_BATCH, DEC_SEQ, D_MODEL)
    new_na_kv = na_glob.reshape(BATCH, DEPTH, 2, SEQ, N_HEADS, HEAD_DIM)
    new_gqa_kv = gqa_glob.reshape(BATCH, DEPTH, 2, SEQ, 2, HEAD_DIM)
    return (y_prompt, y_sample, new_na_kv, new_gqa_kv)
```

```python
import functools

import numpy as np
import jax
import jax.numpy as jnp
from jax import lax
from jax.experimental import pallas as pl
from jax.experimental.pallas import tpu as pltpu

F32 = jnp.float32
BF16 = jnp.bfloat16
I32 = jnp.int32
U32 = jnp.uint32

D_MODEL = 1024
BATCH = 32
SEQ = 256
DEPTH = 4
DEC_BATCH = 2
DEC_SEQ = 1024
PAST_LEN = 256
GRID_W = 64
GRID_ROWS = DEC_SEQ // GRID_W
HEAD_DIM = 64
GROUP_W = 256
N_HEADS = GROUP_W // HEAD_DIM
KV_W = 128
POOL_WINDOWS = (2, 4, 8, 16)
NA_KH = 8
NA_KW = 16
N_EXPERTS = 32
TOP_K = 4
D_FF = 1024
SWIGLU_LIMIT = 7.0
SWIGLU_ALPHA = 1.702
ROPE_THETA = 10000.0
EPS = 1e-6
NEG_INF = -1e30
PROJ_W = 1792

ROWS = 256
SUB = 8
LANES = 128
ROW_TILES = D_MODEL // LANES
T_P = BATCH * SEQ
T_S = DEC_BATCH * DEC_SEQ
T_ALL = T_P + T_S
NB_P = T_P // ROWS
NB_S = T_S // ROWS
NB_TOK = NB_P + NB_S
SUB_S = DEC_SEQ // ROWS
MOE_ROWS = 512
N_ASSIGN = T_ALL * TOP_K
BLOCK_ASSIGN = ROWS * TOP_K
SEG = 16
SEG_SHIFT = 4
N_BLOCKS = -(-(N_ASSIGN + N_EXPERTS * (SEG - 1 + MOE_ROWS - 1)) // MOE_ROWS)
N_SORTED = N_BLOCKS * MOE_ROWS
STAGE_ROWS = 1792
VMEM_LIMIT = 56 * 1024 * 1024


def _dft_tables(L):
    n = np.arange(L)
    ang = 2.0 * np.pi * ((n[:, None] * n[None, :]) % L) / L
    c64 = np.arange(HEAD_DIM)
    a64 = 2.0 * np.pi * ((c64[:, None] * c64[None, :]) % HEAD_DIM) / HEAD_DIM
    bc = np.kron(np.eye(N_HEADS), np.cos(a64))
    bs = np.kron(np.eye(N_HEADS), np.sin(a64))
    return (np.cos(ang).astype(np.float32), np.sin(ang).astype(np.float32),
            bc.astype(np.float32), bs.astype(np.float32))


def _pool_tables(L):
    t = np.arange(L)
    bands = np.zeros((len(POOL_WINDOWS), L, L), np.float32)
    icnt = np.zeros((L, GROUP_W), np.float64)
    for gi, win in enumerate(POOL_WINDOWS):
        a = np.clip(t - win // 2, 0, L)
        b = np.clip(t + win // 2, 0, L)
        n = t[None, :]
        bands[gi] = ((n >= a[:, None]) & (n < b[:, None])).astype(np.float32)
        icnt[:, gi * 64:(gi + 1) * 64] = (1.0 / (b - a))[:, None]
    return bands, icnt.astype(np.float32)


def _rope_tables():
    t = np.arange(DEC_SEQ)
    row = (t // GRID_W).astype(np.float64)
    col = (t % GRID_W).astype(np.float64)
    n_freq = HEAD_DIM // 4
    inv = ROPE_THETA ** (-np.arange(n_freq, dtype=np.float64) / n_freq)
    ar = row[:, None] * inv[None, :]
    ac = col[:, None] * inv[None, :]
    cos = np.concatenate([np.cos(ar), np.cos(ar), np.cos(ac), np.cos(ac)], axis=1)
    sin = np.concatenate([-np.sin(ar), np.sin(ar), -np.sin(ac), np.sin(ac)], axis=1)
    cos = np.concatenate([np.ones((ROWS, HEAD_DIM)), cos], axis=0)
    sin = np.concatenate([np.zeros((ROWS, HEAD_DIM)), sin], axis=0)
    return (np.tile(cos, (1, N_HEADS)).astype(np.float32), np.tile(sin, (1, N_HEADS)).astype(np.float32))


def _group_mean_matrix(width):
    g = np.kron(np.eye(width // HEAD_DIM), np.ones((HEAD_DIM, HEAD_DIM)))
    return g.astype(np.float32)


def _dot(a, b):
    return jnp.dot(a, b, preferred_element_type=F32)


def _dot_nt(a, b):
    return lax.dot_general(a, b, (((1,), (1,)), ((), ())), preferred_element_type=F32)


def _split_bf16(x):
    hi = x.astype(BF16)
    lo = (x - hi.astype(F32)).astype(BF16)
    return hi, lo


def _rmsnorm_rows(x, g):
    ms = jnp.mean(x * x, axis=-1, keepdims=True)
    return x * lax.rsqrt(ms + EPS) * g


def _softmax_rows(s):
    m = jnp.max(s, axis=-1, keepdims=True)
    p = jnp.exp(s - m)
    return p, jnp.sum(p, axis=-1, keepdims=True)


def _full(shape):
    return pl.BlockSpec(shape, lambda *_: (0,) * len(shape))


def _mod_row(i):
    return jnp.where(i < NB_P, 0, 1 + (i - NB_P) // SUB_S)


def _latent_block(i):
    return jnp.maximum(i - NB_P, 0)


MOD_COLS = 512


def _mods_body(cond_ref, w_ref, b_ref, o_ref):
    c = cond_ref[...]
    s = c * (1.0 / (1.0 + jnp.exp(-c)))
    s_hi, s_lo = _split_bf16(s)
    w_hi, w_lo = _split_bf16(w_ref[...])
    o_ref[...] = _dot(s_hi, w_hi) + _dot(s_lo, w_hi) + _dot(s_hi, w_lo) + b_ref[...]


def _mods_call(cond8, w_mod, b_mod3):
    n_col = 6 * D_MODEL // MOD_COLS
    return pl.pallas_call(
        _mods_body,
        grid=(DEPTH, n_col),
        in_specs=[
            pl.BlockSpec((SUB, D_MODEL), lambda l, j: (0, 0)),
            pl.BlockSpec((None, D_MODEL, MOD_COLS), lambda l, j: (l, 0, j)),
            pl.BlockSpec((None, 1, MOD_COLS), lambda l, j: (l, 0, j)),
        ],
        out_specs=pl.BlockSpec((None, SUB, MOD_COLS), lambda l, j: (l, 0, j)),
        out_shape=jax.ShapeDtypeStruct((DEPTH, SUB, 6 * D_MODEL), F32),
        name="adaln_mods",
    )(cond8, w_mod, b_mod3)


def _head_norm(t, g, gm):
    hi, lo = _split_bf16(t * t)
    msq = (_dot(hi, gm) + _dot(lo, gm)) * (1.0 / HEAD_DIM)
    return t * lax.rsqrt(msq + EPS) * g


def _rope(t, cos, sin):
    w = t.shape[1]
    lane = lax.broadcasted_iota(I32, t.shape, 1)
    first = (lane % 32) < 16
    swapped = jnp.where(first, pltpu.roll(t, w - 16, 1), pltpu.roll(t, 16, 1))
    return t * cos + swapped * sin


def _k1_body(x_ref, mod_ref, g1_ref, win_ref, qg_ref, kg_ref, gm_ref, cos_ref, sin_ref, *rest):
    uf_ref, naq_ref, gq_ref, up_ref, nakv_ref, gkv_ref, nakv_s_ref, gkv_s_ref = rest[-8:]
    i = pl.program_id(0)
    sh1 = mod_ref[:, 0:D_MODEL]
    sc1 = mod_ref[:, D_MODEL:2 * D_MODEL]
    h = _rmsnorm_rows(x_ref[...], g1_ref[...]) * (1.0 + sc1) + sh1
    hb = h.astype(BF16)

    def proj(lo, hi):
        return _dot(hb, win_ref[:, lo:hi])

    uf_ref[...] = proj(0, 256).astype(BF16)
    naq_ref[...] = proj(256, 512).astype(BF16)
    na_k = proj(512, 768)
    na_v = proj(768, 1024)
    gq = _head_norm(proj(1024, 1280), qg_ref[...], gm_ref[...])
    gk = _head_norm(proj(1280, 1408), kg_ref[:, 0:KV_W], gm_ref[0:KV_W, 0:KV_W])
    gv = proj(1408, 1536)
    up_ref[...] = proj(1536, 1792)
    gq = _rope(gq, cos_ref[...], sin_ref[...])
    gk = _rope(gk, cos_ref[:, 0:KV_W], sin_ref[:, 0:KV_W])
    gq_ref[...] = gq.astype(BF16)

    @pl.when(i < NB_P)
    def _():
        nakv_ref[0] = na_k
        nakv_ref[1] = na_v
        gkv_ref[0] = gk
        gkv_ref[1] = gv

    @pl.when(i >= NB_P)
    def _():
        nakv_s_ref[0] = na_k.astype(BF16)
        nakv_s_ref[1] = na_v.astype(BF16)
        gkv_s_ref[0] = gk.astype(BF16)
        gkv_s_ref[1] = gv.astype(BF16)


def _k1_call(layer, x, mods_l, g1, w_in_bf, qg, kg, gm, cos, sin, na_glob, gqa_glob):
    rope_blk = lambda i: (jnp.where(i < NB_P, 0, 1 + (i - NB_P) % SUB_S), 0)
    ctx_blk = lambda i: (jnp.minimum(i, NB_P - 1), layer, 0, 0, 0)
    lat_blk = lambda i: (_latent_block(i) // SUB_S, 0, _latent_block(i) % SUB_S, 0)
    row256 = pl.BlockSpec((ROWS, GROUP_W), lambda i: (i, 0))
    in_specs = [
        pl.BlockSpec((ROWS, D_MODEL), lambda i: (i, 0)),
        pl.BlockSpec((None, 1, 6 * D_MODEL), lambda i: (_mod_row(i), 0, 0)),
        _full((1, D_MODEL)),
        _full((D_MODEL, PROJ_W)),
        _full((1, GROUP_W)),
        _full((1, GROUP_W)),
        _full((GROUP_W, GROUP_W)),
        pl.BlockSpec((ROWS, GROUP_W), rope_blk),
        pl.BlockSpec((ROWS, GROUP_W), rope_blk),
    ]
    in_specs += [pl.BlockSpec(memory_space=pl.ANY)] * 2
    args = [x, mods_l, g1, w_in_bf, qg, kg, gm, cos, sin, na_glob, gqa_glob]
    return pl.pallas_call(
        _k1_body,
        grid=(NB_TOK,),
        in_specs=in_specs,
        out_specs=[
            row256, row256, row256, row256,
            pl.BlockSpec((None, None, 2, ROWS, GROUP_W), ctx_blk),
            pl.BlockSpec((None, None, 2, ROWS, KV_W), ctx_blk),
            pl.BlockSpec((None, 2, ROWS, GROUP_W), lat_blk),
            pl.BlockSpec((None, 2, ROWS, KV_W), lat_blk),
        ],
        out_shape=[
            jax.ShapeDtypeStruct((T_ALL, GROUP_W), BF16),
            jax.ShapeDtypeStruct((T_ALL, GROUP_W), BF16),
            jax.ShapeDtypeStruct((T_ALL, GROUP_W), BF16),
            jax.ShapeDtypeStruct((T_ALL, GROUP_W), F32),
            jax.ShapeDtypeStruct((BATCH, DEPTH, 2, SEQ, GROUP_W), F32),
            jax.ShapeDtypeStruct((BATCH, DEPTH, 2, SEQ, KV_W), F32),
            jax.ShapeDtypeStruct((DEC_BATCH, 2, DEC_SEQ, GROUP_W), BF16),
            jax.ShapeDtypeStruct((DEC_BATCH, 2, DEC_SEQ, KV_W), BF16),
        ],
        input_output_aliases={9: 4, 10: 5},
        compiler_params=pltpu.CompilerParams(vmem_limit_bytes=VMEM_LIMIT),
        name="k1_norm_proj",
    )(*args)


def _linmix_body(seq, uf_ref, up_ref, bc_ref, bs_ref, cl_ref, sl_ref, wf_ref, band_ref, icnt_ref,
                 wp_ref, ps_ref, a_ref, d_ref):
    u = uf_ref[...]
    t1 = _dot(u, bc_ref[...]).astype(BF16)
    t2 = _dot(u, bs_ref[...]).astype(BF16)
    f = (_dot(cl_ref[...], t1) - _dot(sl_ref[...], t2)) * (1.0 / float(np.sqrt(seq * HEAD_DIM)))
    a_ref[...] = _dot(f.astype(BF16), wf_ref[...]).astype(BF16)

    up = up_ref[...]
    up_hi, up_lo = _split_bf16(up)
    group = lax.broadcasted_iota(I32, up.shape, 1) // HEAD_DIM
    win_sum = jnp.zeros_like(up)
    for gi in range(len(POOL_WINDOWS)):
        s = _dot(band_ref[gi], up_hi) + _dot(band_ref[gi], up_lo)
        win_sum = jnp.where(group == gi, s, win_sum)
    delta = win_sum * icnt_ref[...] - up
    d_ref[...] = (_dot(delta.astype(BF16), wp_ref[...]) * ps_ref[...]).astype(BF16)


def _linmix_call(uf, up, seq, first_blk, n_blk, tabs, wf_bf, wp_bd_bf, ps, name):
    bc, bs, cl, sl, bands, icnt = tabs
    blk = pl.BlockSpec((seq, GROUP_W), lambda i: (first_blk + i, 0))
    oblk = pl.BlockSpec((seq, GROUP_W), lambda i: (i, 0))
    return pl.pallas_call(
        functools.partial(_linmix_body, seq),
        grid=(n_blk,),
        in_specs=[
            blk, blk,
            _full((GROUP_W, GROUP_W)), _full((GROUP_W, GROUP_W)),
            _full((seq, seq)), _full((seq, seq)),
            _full((GROUP_W, GROUP_W)),
            _full((len(POOL_WINDOWS), seq, seq)),
            _full((seq, GROUP_W)),
            _full((GROUP_W, GROUP_W)),
            _full((1, GROUP_W)),
        ],
        out_specs=[oblk, oblk],
        out_shape=[jax.ShapeDtypeStruct((n_blk * seq, GROUP_W), BF16)] * 2,
        compiler_params=pltpu.CompilerParams(vmem_limit_bytes=VMEM_LIMIT),
        name=name,
    )(uf, up, bc, bs, cl, sl, wf_bf, bands, icnt, wp_bd_bf, ps)


def _attend(q, k, v):
    s = _dot_nt(q, k) * (HEAD_DIM ** -0.5)
    p, l = _softmax_rows(s)
    return _dot(p.astype(BF16), v) / l


def _attn_p_body(naq_ref, nakv_ref, gq_ref, gkv_ref, b_ref, c_ref):
    for h in range(N_HEADS):
        sl = slice(h * HEAD_DIM, (h + 1) * HEAD_DIM)
        k = nakv_ref[0, :, sl].astype(BF16)
        v = nakv_ref[1, :, sl].astype(BF16)
        b_ref[:, sl] = _attend(naq_ref[:, sl], k, v).astype(BF16)
    for h in range(N_HEADS):
        sl = slice(h * HEAD_DIM, (h + 1) * HEAD_DIM)
        kvh = h // 2
        ksl = slice(kvh * HEAD_DIM, (kvh + 1) * HEAD_DIM)
        k = gkv_ref[0, :, ksl].astype(BF16)
        v = gkv_ref[1, :, ksl].astype(BF16)
        c_ref[:, sl] = _attend(gq_ref[:, sl], k, v).astype(BF16)


def _attn_p_call(layer, naq, na_glob, gq, gqa_glob):
    row = pl.BlockSpec((SEQ, GROUP_W), lambda i: (i, 0))
    return pl.pallas_call(
        _attn_p_body,
        grid=(BATCH,),
        in_specs=[
            row,
            pl.BlockSpec((None, None, 2, SEQ, GROUP_W), lambda i: (i, layer, 0, 0, 0)),
            row,
            pl.BlockSpec((None, None, 2, SEQ, KV_W), lambda i: (i, layer, 0, 0, 0)),
        ],
        out_specs=[row, row],
        out_shape=[jax.ShapeDtypeStruct((T_P, GROUP_W), BF16)] * 2,
        name="attn_ctx",
    )(naq, na_glob, gq, gqa_glob)


N_DR = 2 * NA_KH - 1
N_DC = 2 * NA_KW - 1


def _nabias_body(rpb_ref, o_ref):
    lh = pl.program_id(0)
    q = lax.broadcasted_iota(I32, (GRID_W, LANES), 0)
    j = lax.broadcasted_iota(I32, (GRID_W, LANES), 1)
    k = j % GRID_W
    second = j >= GRID_W
    c0 = jnp.clip(q - NA_KW // 2, 0, GRID_W - NA_KW)
    valid = (k >= c0) & (k < c0 + NA_KW)
    diff = k - q + (NA_KW - 1)
    for dr in range(N_DR - 1):
        acc = jnp.zeros((GRID_W, LANES), F32)
        for dc in range(N_DC):
            val = jnp.where(second, rpb_ref[lh, (dr + 1) * N_DC + dc], rpb_ref[lh, dr * N_DC + dc])
            acc = jnp.where(diff == dc, val, acc)
        o_ref[dr] = jnp.where(valid, acc, NEG_INF)


def _nabias_call(rpb_flat):
    n = rpb_flat.shape[0]
    return pl.pallas_call(
        _nabias_body,
        grid=(n,),
        in_specs=[pl.BlockSpec(memory_space=pltpu.SMEM)],
        out_specs=pl.BlockSpec((None, N_DR - 1, GRID_W, LANES), lambda i: (i, 0, 0, 0)),
        out_shape=jax.ShapeDtypeStruct((n, N_DR - 1, GRID_W, LANES), F32),
        name="na_bias_tiles",
    )(rpb_flat)


N_WIN = NA_KH * GRID_W


def _na_s_body(q_ref, kv_ref, ckv_ref, bias_ref, o_ref):
    r = pl.program_id(1)
    r0 = jnp.clip(r - NA_KH // 2, 0, GRID_ROWS - NA_KH)
    start = pl.multiple_of(r0 * GRID_W, GRID_W)
    dr0 = r0 - r + (NA_KH - 1)
    scale = HEAD_DIM ** -0.5
    for h in range(N_HEADS):
        sl = slice(h * HEAD_DIM, (h + 1) * HEAD_DIM)
        q = q_ref[:, sl]
        kw = kv_ref[0, pl.ds(start, N_WIN), sl]
        vw = kv_ref[1, pl.ds(start, N_WIN), sl]
        kc = ckv_ref[0, :, sl].astype(BF16)
        vc = ckv_ref[1, :, sl].astype(BF16)
        bias = jnp.concatenate([bias_ref[h, dr0 + i] for i in range(0, NA_KH, 2)], axis=1)
        s_win = _dot_nt(q, kw) * scale + bias
        s_ctx = _dot_nt(q, kc) * scale
        s = jnp.concatenate([s_win, s_ctx], axis=1)
        p, l = _softmax_rows(s)
        pb = p.astype(BF16)
        o = _dot(pb[:, 0:N_WIN], vw) + _dot(pb[:, N_WIN:], vc)
        o_ref[:, sl] = (o / l).astype(BF16)


def _na_s_call(naq, nakv_s, cache_na_l, bias_l):
    first = T_P // GRID_W
    return pl.pallas_call(
        _na_s_body,
        grid=(DEC_BATCH, GRID_ROWS),
        in_specs=[
            pl.BlockSpec((GRID_W, GROUP_W), lambda b, r: (first + b * GRID_ROWS + r, 0)),
            pl.BlockSpec((None, 2, DEC_SEQ, GROUP_W), lambda b, r: (b, 0, 0, 0)),
            pl.BlockSpec((None, 2, PAST_LEN, GROUP_W), lambda b, r: (b, 0, 0, 0)),
            _full((N_HEADS, N_DR - 1, GRID_W, LANES)),
        ],
        out_specs=pl.BlockSpec((GRID_W, GROUP_W), lambda b, r: (b * GRID_ROWS + r, 0)),
        out_shape=jax.ShapeDtypeStruct((T_S, GROUP_W), BF16),
        compiler_params=pltpu.CompilerParams(vmem_limit_bytes=VMEM_LIMIT),
        name="attn_na_latent",
    )(naq, nakv_s, cache_na_l, bias_l)


GQ_ROWS = 128


def _gqa_s_body(q_ref, kv_ref, ckv_ref, o_ref):
    scale = HEAD_DIM ** -0.5
    for h in range(N_HEADS):
        sl = slice(h * HEAD_DIM, (h + 1) * HEAD_DIM)
        kvh = h // 2
        ksl = slice(kvh * HEAD_DIM, (kvh + 1) * HEAD_DIM)
        q = q_ref[:, sl]
        kl = kv_ref[0, :, ksl]
        vl = kv_ref[1, :, ksl]
        kc = ckv_ref[0, :, ksl].astype(BF16)
        vc = ckv_ref[1, :, ksl].astype(BF16)
        s = jnp.concatenate([_dot_nt(q, kl), _dot_nt(q, kc)], axis=1) * scale
        p, l = _softmax_rows(s)
        pb = p.astype(BF16)
        o = _dot(pb[:, 0:DEC_SEQ], vl) + _dot(pb[:, DEC_SEQ:], vc)
        o_ref[:, sl] = (o / l).astype(BF16)


def _gqa_s_call(gq, gkv_s, cache_gqa_l):
    nq = DEC_SEQ // GQ_ROWS
    first = T_P // GQ_ROWS
    return pl.pallas_call(
        _gqa_s_body,
        grid=(DEC_BATCH, nq),
        in_specs=[
            pl.BlockSpec((GQ_ROWS, GROUP_W), lambda b, j: (first + b * nq + j, 0)),
            pl.BlockSpec((None, 2, DEC_SEQ, KV_W), lambda b, j: (b, 0, 0, 0)),
            pl.BlockSpec((None, 2, PAST_LEN, KV_W), lambda b, j: (b, 0, 0, 0)),
        ],
        out_specs=pl.BlockSpec((GQ_ROWS, GROUP_W), lambda b, j: (b * nq + j, 0)),
        out_shape=jax.ShapeDtypeStruct((T_S, GROUP_W), BF16),
        compiler_params=pltpu.CompilerParams(vmem_limit_bytes=VMEM_LIMIT),
        name="attn_gqa_latent",
    )(gq, gkv_s, cache_gqa_l)


def _k3_body(x_ref, ap_ref, bp_ref, cp_ref, dp_ref, as_ref, bs_ref, cs_ref, ds_ref, mod_ref, wout_ref, g2_ref,
             rwh_ref, rwl_ref, rb_ref, xmid_ref, h2_ref, te_ref, tg_ref, pr_ref, cnt_ref):
    is_ctx = pl.program_id(0) < NB_P
    acc = jnp.zeros((ROWS, D_MODEL), F32)
    for g, (p_ref, s_ref) in enumerate(((ap_ref, as_ref), (bp_ref, bs_ref), (cp_ref, cs_ref), (dp_ref, ds_ref))):
        mixed = jnp.where(is_ctx, p_ref[...], s_ref[...])
        acc += _dot(mixed, wout_ref[g * GROUP_W:(g + 1) * GROUP_W, :])
    gate1 = mod_ref[:, 2 * D_MODEL:3 * D_MODEL]
    sh2 = mod_ref[:, 3 * D_MODEL:4 * D_MODEL]
    sc2 = mod_ref[:, 4 * D_MODEL:5 * D_MODEL]
    xm = x_ref[...] + gate1 * acc
    xmid_ref[...] = xm
    h2 = _rmsnorm_rows(xm, g2_ref[...]) * (1.0 + sc2) + sh2
    hi, lo = _split_bf16(h2)
    h2_ref[...] = hi
    logits = _dot(hi, rwh_ref[...]) + _dot(lo, rwh_ref[...]) + _dot(hi, rwl_ref[...]) + rb_ref[...]

    lane = lax.broadcasted_iota(I32, logits.shape, 1)
    cur = logits
    vals, idxs = [], []
    for _ in range(TOP_K):
        m = jnp.max(cur, axis=-1, keepdims=True)
        idx = jnp.min(jnp.where(cur == m, lane, LANES), axis=-1, keepdims=True)
        vals.append(m)
        idxs.append(idx)
        cur = jnp.where(lane == idx, -jnp.inf, cur)
    exps = [jnp.exp(v - vals[0]) for v in vals]
    tot = exps[0] + exps[1] + exps[2] + exps[3]
    te = jnp.zeros(logits.shape, I32)
    tg = jnp.zeros(logits.shape, F32)
    for k in range(TOP_K):
        te = jnp.where(lane == k, idxs[k], te)
        tg = jnp.where(lane == k, exps[k] / tot, tg)
    te_ref[...] = te
    tg_ref[...] = tg

    member = jnp.zeros(logits.shape, F32)
    for k in range(TOP_K):
        member = jnp.where(lane == idxs[k], 1.0, member)
    rr = lax.broadcasted_iota(I32, (ROWS, ROWS), 0)
    cc = lax.broadcasted_iota(I32, (ROWS, ROWS), 1)
    earlier = jnp.where(cc < rr, 1.0, 0.0).astype(BF16)
    prefix = _dot(earlier, member.astype(BF16))
    cnt = jnp.sum(member, axis=0, keepdims=True)
    er = lax.broadcasted_iota(I32, (LANES, LANES), 0)
    ec = lax.broadcasted_iota(I32, (LANES, LANES), 1)
    lower = jnp.where(er < ec, 1.0, 0.0).astype(BF16)
    loc = _dot(jnp.broadcast_to(cnt, (SUB, LANES)).astype(BF16), lower)[0:1, :]
    pr = jnp.zeros(logits.shape, F32)
    for k in range(TOP_K):
        hit = lane == idxs[k]
        rank = jnp.sum(jnp.where(hit, prefix, 0.0), axis=-1, keepdims=True)
        start = jnp.sum(jnp.where(hit, loc, 0.0), axis=-1, keepdims=True)
        pr = jnp.where(lane == k, start + rank, pr)
        pr = jnp.where(lane == TOP_K + k, rank, pr)
    pr_ref[...] = pr.astype(I32)
    cnt_ref[...] = cnt.astype(I32)


def _k3_call(x, mix_p, mix_s, mods_l, w_out_bf, g2, rwh, rwl, rb):
    row_p = pl.BlockSpec((ROWS, GROUP_W), lambda i: (jnp.minimum(i, NB_P - 1), 0))
    row_s = pl.BlockSpec((ROWS, GROUP_W), lambda i: (_latent_block(i), 0))
    rowd = pl.BlockSpec((ROWS, D_MODEL), lambda i: (i, 0))
    row128 = pl.BlockSpec((ROWS, LANES), lambda i: (i, 0))
    return pl.pallas_call(
        _k3_body,
        grid=(NB_TOK,),
        in_specs=[
            rowd, row_p, row_p, row_p, row_p, row_s, row_s, row_s, row_s,
            pl.BlockSpec((None, 1, 6 * D_MODEL), lambda i: (_mod_row(i), 0, 0)),
            _full((D_MODEL, D_MODEL)),
            _full((1, D_MODEL)),
            _full((D_MODEL, LANES)), _full((D_MODEL, LANES)), _full((1, LANES)),
        ],
        out_specs=[rowd, rowd, row128, row128, row128,
                   pl.BlockSpec((None, 1, LANES), lambda i: (i, 0, 0))],
        out_shape=[
            jax.ShapeDtypeStruct((T_ALL, D_MODEL), F32),
            jax.ShapeDtypeStruct((T_ALL, D_MODEL), BF16),
            jax.ShapeDtypeStruct((T_ALL, LANES), I32),
            jax.ShapeDtypeStruct((T_ALL, LANES), F32),
            jax.ShapeDtypeStruct((T_ALL, LANES), I32),
            jax.ShapeDtypeStruct((NB_TOK, 1, LANES), I32),
        ],
        compiler_params=pltpu.CompilerParams(vmem_limit_bytes=VMEM_LIMIT),
        name="k3_out_router",
    )(x, *mix_p, *mix_s, mods_l, w_out_bf, g2, rwh, rwl, rb)


TAB_CNT, TAB_LOC, TAB_BASE, TAB_LSTAGE, TAB_SOFF, TAB_TOTAL, TAB_NCHUNK = range(7)


def _routing_tables(cnt):
    c = cnt.reshape(NB_TOK, LANES)[:, :N_EXPERTS]
    counts = jnp.sum(c, axis=0)
    padded = (counts + (SEG - 1) + (MOE_ROWS - 1)) // MOE_ROWS * MOE_ROWS
    pend = jnp.cumsum(padded)
    pstart = pend - padded
    base = pstart[None, :] + jnp.cumsum(c, axis=0) - c
    loc = jnp.cumsum(c, axis=1) - c
    off = base % SUB
    nch = jnp.where(c > 0, (off + c + SEG - 1) // SEG, 0)
    lstage = (jnp.cumsum(nch, axis=1) - nch) * SEG
    soff = lstage + off
    total = jnp.sum(nch, axis=1, keepdims=True) * SEG

    def pad(t):
        return jnp.zeros((NB_TOK, LANES), I32).at[:, :t.shape[1]].set(t.astype(I32))

    tab = jnp.stack([pad(c), pad(loc), pad(base), pad(lstage), pad(soff), pad(total), pad(nch),
                     jnp.zeros((NB_TOK, LANES), I32)], axis=1)
    ztab = jnp.zeros((SUB, LANES), I32).at[0, :N_EXPERTS].set((pstart + counts).astype(I32))
    ztab = ztab.at[1, :N_EXPERTS].set(pend.astype(I32))
    ztab = ztab.at[2, 0].set((pend[-1] // MOE_ROWS).astype(I32))
    first_blk = (pstart // MOE_ROWS).astype(I32)
    n_blk = (padded // MOE_ROWS).astype(I32)
    n_used = (pend[-1] // MOE_ROWS).astype(I32).reshape(1)
    return tab, ztab, first_blk, n_blk, n_used


def _dispatch_body(tab_ref, ztab_ref, h2_ref, pr_ref, xs_ref, stage0, stage1, zbuf, pending, sems):
    b = pl.program_id(0)
    sem = sems.at[0]

    def rows_copy(src, src_row, dst_row, n_rows, sem=sem):
        return pltpu.make_async_copy(
            src.at[pl.ds(pl.multiple_of(src_row * SUB, SUB), n_rows * SUB)],
            xs_ref.at[pl.ds(pl.multiple_of(dst_row * SUB, SUB), n_rows * SUB)], sem)

    def seg_copy(src, src_row, dst_row, sem=sem):
        return rows_copy(src, src_row, dst_row, SEG, sem)

    def drain(n, sem=sem):
        def body(j, carry):
            seg_copy(zbuf, 0, 0, sem).wait()
            return carry
        lax.fori_loop(0, n, body, 0)

    @pl.when(b == 0)
    def _():
        zbuf[...] = jnp.zeros(zbuf.shape, F32)
        stage0[pl.ds(BLOCK_ASSIGN * SUB, SEG * SUB), :] = jnp.zeros((SEG * SUB, LANES), F32)
        stage1[pl.ds(BLOCK_ASSIGN * SUB, SEG * SUB), :] = jnp.zeros((SEG * SUB, LANES), F32)

        def per_expert(e, carry):
            n_seg, n_row = carry
            end = ztab_ref[0, e]
            length = ztab_ref[1, e] - end
            n = lax.shift_right_logical(length, SEG_SHIFT)
            rem = length & (SEG - 1)

            def one(j, c):
                seg_copy(zbuf, 0, end + j * SEG).start()
                return c
            lax.fori_loop(0, n, one, 0)

            def one_row(j, c):
                rows_copy(zbuf, 0, end + n * SEG + j, 1).start()
                return c
            lax.fori_loop(0, rem, one_row, 0)
            return n_seg + n, n_row + rem
        n_seg, n_row = lax.fori_loop(0, N_EXPERTS, per_expert, (0, 0))
        drain(n_seg)

        def row_wait(j, c):
            rows_copy(zbuf, 0, 0, 1).wait()
            return c
        lax.fori_loop(0, n_row, row_wait, 0)

        first_free = ztab_ref[2, 0]

        def fill(j, c):
            rows_copy(zbuf, 0, (first_free + j) * MOE_ROWS, MOE_ROWS).start()
            return c
        lax.fori_loop(0, N_BLOCKS - first_free, fill, 0)

        def fill_wait(j, c):
            rows_copy(zbuf, 0, 0, MOE_ROWS).wait()
            return c
        lax.fori_loop(0, N_BLOCKS - first_free, fill_wait, 0)

    pos_t = pr_ref[...].astype(F32).T
    h2 = h2_ref[...]

    def sort(stage):
        for c in range(BLOCK_ASSIGN // ROWS):
            p = (lax.broadcasted_iota(I32, (ROWS, ROWS), 0) + c * ROWS).astype(F32)
            sel = p == pos_t[0:1, :]
            for k in range(1, TOP_K):
                sel = sel | (p == pos_t[k:k + 1, :])
            rows = _dot(jnp.where(sel, 1.0, 0.0).astype(BF16), h2)
            for s in range(ROW_TILES):
                stage[pl.ds(c * ROWS * SUB + s, ROWS, stride=SUB), :] = rows[:, s * LANES:(s + 1) * LANES]

    def send(stage, sem):
        def per_expert(e, tot):
            loc = tab_ref[TAB_LOC, e]
            base = tab_ref[TAB_BASE, e]
            n = lax.shift_right_logical(tab_ref[TAB_CNT, e] + (SEG - 1), SEG_SHIFT)

            def one(j, carry):
                seg_copy(stage, loc + j * SEG, base + j * SEG, sem).start()
                return carry
            lax.fori_loop(0, n, one, 0)
            return tot + n
        return lax.fori_loop(0, N_EXPERTS, per_expert, 0)

    for slot, stage in enumerate((stage0, stage1)):
        @pl.when((b & 1) == slot)
        def _(stage=stage, slot=slot):
            sort(stage)

            @pl.when(b > 0)
            def _():
                drain(pending[0], sems.at[1 - slot])
            n_sent = send(stage, sems.at[slot])
            pending[0] = n_sent

            @pl.when(b == NB_TOK - 1)
            def _():
                drain(n_sent, sems.at[slot])


def _dispatch_call(tab, ztab, h2, pr):
    return pl.pallas_call(
        _dispatch_body,
        grid=(NB_TOK,),
        in_specs=[
            pl.BlockSpec((None, SUB, LANES), lambda i: (i, 0, 0), memory_space=pltpu.SMEM),
            pl.BlockSpec(memory_space=pltpu.SMEM),
            pl.BlockSpec((ROWS, D_MODEL), lambda i: (i, 0)),
            pl.BlockSpec((ROWS, LANES), lambda i: (i, 0)),
        ],
        out_specs=pl.BlockSpec(memory_space=pl.ANY),
        out_shape=jax.ShapeDtypeStruct((N_SORTED * ROW_TILES, LANES), F32),
        scratch_shapes=[
            pltpu.VMEM(((BLOCK_ASSIGN + SEG) * SUB, LANES), F32),
            pltpu.VMEM(((BLOCK_ASSIGN + SEG) * SUB, LANES), F32),
            pltpu.VMEM((MOE_ROWS * SUB, LANES), F32),
            pltpu.SMEM((1,), I32),
            pltpu.SemaphoreType.DMA((2,)),
        ],
        compiler_params=pltpu.CompilerParams(vmem_limit_bytes=VMEM_LIMIT),
        name="moe_dispatch",
    )(tab, ztab, h2, pr)


HALF_FF = D_FF // 2


def _moe_body(fb_ref, nb_ref, nu_ref, xs_ref, wgu_ref, bgu_ref, wd_ref, bd_ref, yp_ref,
              wgu_bf, wd_bf, wd_scr, xbuf0, xbuf1, ybuf0, ybuf1, xsems, ysems):
    e = pl.program_id(0)
    first = fb_ref[e]
    n_blk = nb_ref[e]
    n_used = nu_ref[0]
    xbufs = (xbuf0, xbuf1)
    ybufs = (ybuf0, ybuf1)

    def x_copy(g, slot):
        src = xs_ref.at[pl.ds(pl.multiple_of(g * (MOE_ROWS * SUB), MOE_ROWS * SUB), MOE_ROWS * SUB)]
        return pltpu.make_async_copy(src, xbufs[slot], xsems.at[slot])

    def y_copy(g, slot):
        dst = yp_ref.at[pl.ds(pl.multiple_of(g * MOE_ROWS, MOE_ROWS), MOE_ROWS)]
        return pltpu.make_async_copy(ybufs[slot], dst, ysems.at[slot])

    @pl.when(e == 0)
    def _():
        x_copy(0, 0).start()

    @pl.when(n_blk > 0)
    def _():
        for c in range(4):
            cs = slice(c * 512, (c + 1) * 512)
            wgu_bf[:, cs] = wgu_ref[:, cs].astype(BF16)
        for c in range(ROW_TILES):
            cs = slice(c * LANES, (c + 1) * LANES)
            wd_scr[pl.ds(0, HALF_FF, stride=2), :] = wd_ref[0:HALF_FF, cs]
            wd_scr[pl.ds(1, HALF_FF, stride=2), :] = wd_ref[HALF_FF:D_FF, cs]
            wd_bf[:, cs] = wd_scr[...].astype(BF16)

    def ffn(x_ref):
        x = jnp.concatenate(
            [x_ref[pl.ds(s, MOE_ROWS, stride=ROW_TILES), :] for s in range(ROW_TILES)], axis=1)
        xb = x.astype(BF16)
        ha = _dot(xb, wgu_bf[:, 0:D_FF]) + bgu_ref[:, 0:D_FF]
        hb = _dot(xb, wgu_bf[:, D_FF:2 * D_FF]) + bgu_ref[:, D_FF:2 * D_FF]
        even = (lax.broadcasted_iota(I32, ha.shape, 1) % 2) == 0
        gate = jnp.where(even, ha, pltpu.roll(hb, 1, 1))
        up = jnp.where(even, pltpu.roll(ha, D_FF - 1, 1), hb)
        gate = jnp.minimum(gate, SWIGLU_LIMIT)
        up = jnp.clip(up, -SWIGLU_LIMIT, SWIGLU_LIMIT)
        glu = gate * (1.0 / (1.0 + jnp.exp(-gate * SWIGLU_ALPHA)))
        act = ((up + 1.0) * glu).astype(BF16)
        return _dot(act, wd_bf[...]) + bd_ref[...]

    def block(j, carry):
        g = first + j
        for slot in range(2):
            @pl.when((g & 1) == slot)
            def _(slot=slot):
                x_copy(g, slot).wait()

                @pl.when(g + 1 < n_used)
                def _():
                    x_copy(g + 1, 1 - slot).start()
                y = ffn(xbufs[slot])

                @pl.when(g >= 2)
                def _():
                    y_copy(g - 2, slot).wait()
                ybufs[slot][...] = y
                y_copy(g, slot).start()
        return carry
    lax.fori_loop(0, n_blk, block, 0)

    @pl.when(e == N_EXPERTS - 1)
    def _():
        for slot in range(2):
            @pl.when(((n_used - 1) & 1) == slot)
            def _(slot=slot):
                @pl.when(n_used >= 2)
                def _():
                    y_copy(n_used - 2, 1 - slot).wait()
                y_copy(n_used - 1, slot).wait()
        ybuf0[...] = jnp.zeros(ybuf0.shape, F32)

        def fill(j, carry):
            y_copy(n_used + j, 0).start()
            return carry
        lax.fori_loop(0, N_BLOCKS - n_used, fill, 0)

        def fill_wait(j, carry):
            y_copy(0, 0).wait()
            return carry
        lax.fori_loop(0, N_BLOCKS - n_used, fill_wait, 0)


def _moe_call(layer, first_blk, n_blk, n_used, xs, w_gu, b_gu4, w_down, b_down4):
    wmap = lambda e, fb, nb, nu: (layer, e, 0, 0)
    grid_spec = pltpu.PrefetchScalarGridSpec(
        num_scalar_prefetch=3,
        grid=(N_EXPERTS,),
        in_specs=[
            pl.BlockSpec(memory_space=pl.ANY),
            pl.BlockSpec((None, None, D_MODEL, 2 * D_FF), wmap),
            pl.BlockSpec((None, None, 1, 2 * D_FF), wmap),
            pl.BlockSpec((None, None, D_FF, D_MODEL), wmap),
            pl.BlockSpec((None, None, 1, D_MODEL), wmap),
        ],
        out_specs=pl.BlockSpec(memory_space=pl.ANY),
        scratch_shapes=[
            pltpu.VMEM((D_MODEL, 2 * D_FF), BF16),
            pltpu.VMEM((D_FF, D_MODEL), BF16),
            pltpu.VMEM((D_FF, LANES), F32),
            pltpu.VMEM((MOE_ROWS * SUB, LANES), F32),
            pltpu.VMEM((MOE_ROWS * SUB, LANES), F32),
            pltpu.VMEM((MOE_ROWS, D_MODEL), F32),
            pltpu.VMEM((MOE_ROWS, D_MODEL), F32),
            pltpu.SemaphoreType.DMA((2,)),
            pltpu.SemaphoreType.DMA((2,)),
        ],
    )
    return pl.pallas_call(
        _moe_body,
        grid_spec=grid_spec,
        out_shape=jax.ShapeDtypeStruct((N_SORTED, D_MODEL), F32),
        compiler_params=pltpu.CompilerParams(vmem_limit_bytes=VMEM_LIMIT),
        name="moe_experts",
    )(first_blk, n_blk, n_used, xs, w_gu, b_gu4, w_down, b_down4)


def _combine_body(final, tab_ref, tabn_ref, tabv_ref, yp_ref, te_ref, tg_ref, pr_ref, xmid_ref, mod_ref, fg_ref,
                  o_ref, stage0, stage1, acc, sems):
    b = pl.program_id(0)
    stages = (stage0, stage1)

    def chunk_copy(src_row, dst_row, slot):
        return pltpu.make_async_copy(
            yp_ref.at[pl.ds(pl.multiple_of(src_row, SUB), SEG)],
            stages[slot].at[pl.ds(pl.multiple_of(dst_row, SUB), SEG)], sems.at[slot])

    def fetch(tab, slot):
        def per_expert(e, carry):
            base = tab[TAB_BASE, e]
            off = base & (SUB - 1)
            lstage = tab[TAB_LSTAGE, e]

            def one(j, c):
                chunk_copy(base - off + j * SEG, lstage + j * SEG, slot).start()
                return c
            lax.fori_loop(0, tab[TAB_NCHUNK, e], one, 0)
            return carry
        lax.fori_loop(0, N_EXPERTS, per_expert, 0)

        total = tab[TAB_TOTAL, 0]

        def zero_tail(j, c):
            stages[slot][pl.ds(pl.multiple_of(total + j * SEG, SEG), SEG), :] = jnp.zeros((SEG, D_MODEL), F32)
            return c
        lax.fori_loop(0, lax.shift_right_logical(STAGE_ROWS - total, SEG_SHIFT), zero_tail, 0)

    @pl.when(b == 0)
    def _():
        fetch(tab_ref, 0)

    for slot in range(2):
        @pl.when(((b & 1) == slot) & (b + 1 < NB_TOK))
        def _(slot=slot):
            fetch(tabn_ref, 1 - slot)

    total = tab_ref[TAB_TOTAL, 0]
    for slot in range(2):
        @pl.when((b & 1) == slot)
        def _(slot=slot):
            def drain(j, c):
                chunk_copy(0, 0, slot).wait()
                return c
            lax.fori_loop(0, lax.shift_right_logical(total, SEG_SHIFT), drain, 0)

    lane = lax.broadcasted_iota(I32, (ROWS, LANES), 1)
    te = te_ref[...]
    pr = pr_ref[...]
    soff = tabv_ref[TAB_SOFF:TAB_SOFF + 1, :].astype(F32)
    packed = jnp.where((lane >= TOP_K) & (lane < 2 * TOP_K), pltpu.roll(tg_ref[...], TOP_K, 1), 0.0)
    sp = []
    for k in range(TOP_K):
        seg_start = jnp.sum(jnp.where(lane == te[:, k:k + 1], soff, 0.0), axis=-1, keepdims=True)
        spk = seg_start + pr[:, TOP_K + k:TOP_K + k + 1].astype(F32)
        sp.append(spk)
        packed = jnp.where(lane == k, spk, packed)
    packed_t = packed.T

    def unsort(stage, c):
        pcol = (lax.broadcasted_iota(I32, (ROWS, ROWS), 1) + c * ROWS).astype(F32)
        sel = pcol == sp[0]
        for k in range(1, TOP_K):
            sel = sel | (pcol == sp[k])
        prow = (lax.broadcasted_iota(I32, (ROWS, ROWS), 0) + c * ROWS).astype(F32)
        gmat = jnp.zeros((ROWS, ROWS), F32)
        for k in range(TOP_K):
            gmat = jnp.where(prow == packed_t[k:k + 1, :], packed_t[TOP_K + k:TOP_K + k + 1, :], gmat)
        row_gate = jnp.sum(gmat, axis=-1, keepdims=True)
        rows = (stage[c * ROWS:(c + 1) * ROWS, :] * row_gate).astype(BF16)
        return _dot(jnp.where(sel, 1.0, 0.0).astype(BF16), rows)

    n_always = BLOCK_ASSIGN // ROWS
    for slot in range(2):
        @pl.when((b & 1) == slot)
        def _(slot=slot):
            out = unsort(stages[slot], 0)
            for c in range(1, n_always):
                out += unsort(stages[slot], c)
            acc[...] = out
            for c in range(n_always, STAGE_ROWS // ROWS):
                @pl.when(c * ROWS < total)
                def _(c=c):
                    acc[...] += unsort(stages[slot], c)

    gate2 = mod_ref[:, 5 * D_MODEL:6 * D_MODEL]
    x = xmid_ref[...] + gate2 * acc[...]
    if final:
        x = _rmsnorm_rows(x, fg_ref[...])
    o_ref[...] = x


def _combine_call(tab, yp, te, tg, pr, xmid, mods_l, final_g, final):
    rowd = pl.BlockSpec((ROWS, D_MODEL), lambda i: (i, 0))
    row128 = pl.BlockSpec((ROWS, LANES), lambda i: (i, 0))
    return pl.pallas_call(
        functools.partial(_combine_body, final),
        grid=(NB_TOK,),
        in_specs=[
            pl.BlockSpec((None, SUB, LANES), lambda i: (i, 0, 0), memory_space=pltpu.SMEM),
            pl.BlockSpec((None, SUB, LANES), lambda i: (jnp.minimum(i + 1, NB_TOK - 1), 0, 0),
                         memory_space=pltpu.SMEM),
            pl.BlockSpec((None, SUB, LANES), lambda i: (i, 0, 0)),
            pl.BlockSpec(memory_space=pl.ANY),
            row128, row128, row128,
            rowd,
            pl.BlockSpec((None, 1, 6 * D_MODEL), lambda i: (_mod_row(i), 0, 0)),
            _full((1, D_MODEL)),
        ],
        out_specs=rowd,
        out_shape=jax.ShapeDtypeStruct((T_ALL, D_MODEL), F32),
        scratch_shapes=[
            pltpu.VMEM((STAGE_ROWS, D_MODEL), F32),
            pltpu.VMEM((STAGE_ROWS, D_MODEL), F32),
            pltpu.VMEM((ROWS, D_MODEL), F32),
            pltpu.SemaphoreType.DMA((2,)),
        ],
        compiler_params=pltpu.CompilerParams(vmem_limit_bytes=VMEM_LIMIT),
        name="moe_combine",
    )(tab, tab, tab, yp, te, tg, pr, xmid, mods_l, final_g)


def kernel(x_prompt, x_sample, cache_na_kv, cache_gqa_kv, c, c_ctx, norm1_g, norm2_g, w_mod, b_mod,
           w_in, w_fourier, na_rpb, q_norm_g, k_norm_g, w_pool, pool_scale, w_out, router_w, router_b,
           w_gu, b_gu, w_down, b_down, final_g):
    def lin_tables(L):
        cl, sl, bc, bs = _dft_tables(L)
        bands, icnt = _pool_tables(L)
        return (jnp.asarray(bc).astype(BF16), jnp.asarray(bs).astype(BF16),
                jnp.asarray(cl).astype(BF16), jnp.asarray(sl).astype(BF16),
                jnp.asarray(bands).astype(BF16), jnp.asarray(icnt))

    tabs_p = lin_tables(SEQ)
    tabs_s = lin_tables(DEC_SEQ)
    cos_np, sin_np = _rope_tables()
    cos_t, sin_t = jnp.asarray(cos_np), jnp.asarray(sin_np)
    gm = jnp.asarray(_group_mean_matrix(GROUP_W)).astype(BF16)

    cond8 = jnp.zeros((SUB, D_MODEL), F32).at[0].set(c_ctx).at[1:1 + DEC_BATCH].set(c)
    mods = _mods_call(cond8, w_mod, b_mod.reshape(DEPTH, 1, 6 * D_MODEL))
    mods = mods.reshape(DEPTH, SUB, 1, 6 * D_MODEL)

    bias_tiles = _nabias_call(na_rpb.reshape(DEPTH * N_HEADS, N_DR * N_DC))
    bias_tiles = bias_tiles.reshape(DEPTH, N_HEADS, N_DR - 1, GRID_W, LANES)

    w_in_bf = w_in.astype(BF16)
    w_out_bf = w_out.astype(BF16)
    w_f_bf = w_fourier.astype(BF16)
    eye4 = jnp.eye(4, dtype=F32)
    w_pool_bd = jnp.einsum('lgce,gh->lgche', w_pool, eye4).reshape(DEPTH, GROUP_W, GROUP_W).astype(BF16)
    qg = jnp.tile(q_norm_g, (1, N_HEADS)).reshape(DEPTH, 1, GROUP_W)
    kg = jnp.tile(k_norm_g, (1, N_HEADS)).reshape(DEPTH, 1, GROUP_W)
    rw_pad = jnp.zeros((DEPTH, D_MODEL, LANES), F32).at[:, :, :N_EXPERTS].set(router_w)
    rw_hi = rw_pad.astype(BF16)
    rw_lo = (rw_pad - rw_hi.astype(F32)).astype(BF16)
    rb_pad = jnp.full((DEPTH, 1, LANES), NEG_INF, F32).at[:, 0, :N_EXPERTS].set(router_b)
    b_gu4 = b_gu.reshape(DEPTH, N_EXPERTS, 1, 2 * D_FF)
    b_down4 = b_down.reshape(DEPTH, N_EXPERTS, 1, D_MODEL)
    final_g2 = final_g.reshape(1, D_MODEL)
    cache_na = cache_na_kv.reshape(DEC_BATCH, DEPTH, 2, PAST_LEN, GROUP_W)
    cache_gqa = cache_gqa_kv.reshape(DEC_BATCH, DEPTH, 2, PAST_LEN, KV_W)

    x = jnp.concatenate([x_prompt.reshape(T_P, D_MODEL), x_sample.reshape(T_S, D_MODEL)], axis=0)
    na_glob = jnp.zeros((BATCH, DEPTH, 2, SEQ, GROUP_W), F32)
    gqa_glob = jnp.zeros((BATCH, DEPTH, 2, SEQ, KV_W), F32)
    for l in range(DEPTH):
        g1 = norm1_g[l].reshape(1, D_MODEL)
        g2 = norm2_g[l].reshape(1, D_MODEL)
        ps = pool_scale[l].reshape(1, GROUP_W)

        uf, naq, gq, up, na_glob, gqa_glob, nakv_s, gkv_s = _k1_call(
            l, x, mods[l], g1, w_in_bf[l], qg[l], kg[l], gm, cos_t, sin_t, na_glob, gqa_glob)

        a_p, d_p = _linmix_call(uf, up, SEQ, 0, BATCH, tabs_p, w_f_bf[l], w_pool_bd[l], ps, "linmix_ctx")
        a_s, d_s = _linmix_call(uf, up, DEC_SEQ, T_P // DEC_SEQ, DEC_BATCH, tabs_s, w_f_bf[l], w_pool_bd[l], ps,
                                "linmix_latent")
        b_p, c_p = _attn_p_call(l, naq, na_glob, gq, gqa_glob)
        b_s = _na_s_call(naq, nakv_s, cache_na[:, l], bias_tiles[l])
        c_s = _gqa_s_call(gq, gkv_s, cache_gqa[:, l])

        xmid, h2, te, tg, pr, cnt = _k3_call(x, (a_p, b_p, c_p, d_p), (a_s, b_s, c_s, d_s), mods[l], w_out_bf[l],
                                             g2, rw_hi[l], rw_lo[l], rb_pad[l])
        tab, ztab, first_blk, n_blk, n_used = _routing_tables(cnt)
        xs = _dispatch_call(tab, ztab, h2, pr)
        yp = _moe_call(l, first_blk, n_blk, n_used, xs, w_gu, b_gu4, w_down, b_down4)
        x = _combine_call(tab, yp, te, tg, pr, xmid, mods[l], final_g2, l == DEPTH - 1)

    y_prompt = x[:T_P].reshape(BATCH, SEQ, D_MODEL)
    y_sample = x[T_P:].reshape(DEC_BATCH, DEC_SEQ, D_MODEL)
    new_na_kv = na_glob.reshape(BATCH, DEPTH, 2, SEQ, N_HEADS, HEAD_DIM)
    new_gqa_kv = gqa_glob.reshape(BATCH, DEPTH, 2, SEQ, 2, HEAD_DIM)
    return (y_prompt, y_sample, new_na_kv, new_gqa_kv)
```

```python
import functools

import numpy as np
import jax
import jax.numpy as jnp
from jax import lax
from jax.experimental import pallas as pl
from jax.experimental.pallas import tpu as pltpu

F32 = jnp.float32
BF16 = jnp.bfloat16
I32 = jnp.int32
U32 = jnp.uint32

D_MODEL = 1024
BATCH = 32
SEQ = 256
DEPTH = 4
DEC_BATCH = 2
DEC_SEQ = 1024
PAST_LEN = 256
GRID_W = 64
GRID_ROWS = DEC_SEQ // GRID_W
HEAD_DIM = 64
GROUP_W = 256
N_HEADS = GROUP_W // HEAD_DIM
KV_W = 128
POOL_WINDOWS = (2, 4, 8, 16)
NA_KH = 8
NA_KW = 16
N_EXPERTS = 32
TOP_K = 4
D_FF = 1024
SWIGLU_LIMIT = 7.0
SWIGLU_ALPHA = 1.702
ROPE_THETA = 10000.0
EPS = 1e-6
NEG_INF = -1e30
PROJ_W = 1792

ROWS = 256
SUB = 8
LANES = 128
ROW_TILES = D_MODEL // LANES
T_P = BATCH * SEQ
T_S = DEC_BATCH * DEC_SEQ
T_ALL = T_P + T_S
NB_P = T_P // ROWS
NB_S = T_S // ROWS
NB_TOK = NB_P + NB_S
SUB_S = DEC_SEQ // ROWS
MOE_ROWS = 512
N_ASSIGN = T_ALL * TOP_K
BLOCK_ASSIGN = ROWS * TOP_K
SEG = 16
SEG_SHIFT = 4
N_BLOCKS = -(-(N_ASSIGN + N_EXPERTS * (SEG - 1 + MOE_ROWS - 1)) // MOE_ROWS)
N_SORTED = N_BLOCKS * MOE_ROWS
STAGE_ROWS = 1792
VMEM_LIMIT = 56 * 1024 * 1024


def _dft_tables(L):
    n = np.arange(L)
    ang = 2.0 * np.pi * ((n[:, None] * n[None, :]) % L) / L
    c64 = np.arange(HEAD_DIM)
    a64 = 2.0 * np.pi * ((c64[:, None] * c64[None, :]) % HEAD_DIM) / HEAD_DIM
    bc = np.kron(np.eye(N_HEADS), np.cos(a64))
    bs = np.kron(np.eye(N_HEADS), np.sin(a64))
    return (np.cos(ang).astype(np.float32), np.sin(ang).astype(np.float32),
            bc.astype(np.float32), bs.astype(np.float32))


def _pool_tables(L):
    t = np.arange(L)
    bands = np.zeros((len(POOL_WINDOWS), L, L), np.float32)
    icnt = np.zeros((L, GROUP_W), np.float64)
    for gi, win in enumerate(POOL_WINDOWS):
        a = np.clip(t - win // 2, 0, L)
        b = np.clip(t + win // 2, 0, L)
        n = t[None, :]
        bands[gi] = ((n >= a[:, None]) & (n < b[:, None])).astype(np.float32)
        icnt[:, gi * 64:(gi + 1) * 64] = (1.0 / (b - a))[:, None]
    return bands, icnt.astype(np.float32)


def _rope_tables():
    t = np.arange(DEC_SEQ)
    row = (t // GRID_W).astype(np.float64)
    col = (t % GRID_W).astype(np.float64)
    n_freq = HEAD_DIM // 4
    inv = ROPE_THETA ** (-np.arange(n_freq, dtype=np.float64) / n_freq)
    ar = row[:, None] * inv[None, :]
    ac = col[:, None] * inv[None, :]
    cos = np.concatenate([np.cos(ar), np.cos(ar), np.cos(ac), np.cos(ac)], axis=1)
    sin = np.concatenate([-np.sin(ar), np.sin(ar), -np.sin(ac), np.sin(ac)], axis=1)
    cos = np.concatenate([np.ones((ROWS, HEAD_DIM)), cos], axis=0)
    sin = np.concatenate([np.zeros((ROWS, HEAD_DIM)), sin], axis=0)
    return (np.tile(cos, (1, N_HEADS)).astype(np.float32), np.tile(sin, (1, N_HEADS)).astype(np.float32))


def _group_mean_matrix(width):
    g = np.kron(np.eye(width // HEAD_DIM), np.ones((HEAD_DIM, HEAD_DIM)))
    return g.astype(np.float32)


def _dot(a, b):
    return jnp.dot(a, b, preferred_element_type=F32)


def _dot_nt(a, b):
    return lax.dot_general(a, b, (((1,), (1,)), ((), ())), preferred_element_type=F32)


def _split_bf16(x):
    hi = x.astype(BF16)
    lo = (x - hi.astype(F32)).astype(BF16)
    return hi, lo


def _rmsnorm_rows(x, g):
    ms = jnp.mean(x * x, axis=-1, keepdims=True)
    return x * lax.rsqrt(ms + EPS) * g


def _softmax_rows(s):
    m = jnp.max(s, axis=-1, keepdims=True)
    p = jnp.exp(s - m)
    return p, jnp.sum(p, axis=-1, keepdims=True)


def _full(shape):
    return pl.BlockSpec(shape, lambda *_: (0,) * len(shape))


def _mod_row(i):
    return jnp.where(i < NB_P, 0, 1 + (i - NB_P) // SUB_S)


def _latent_block(i):
    return jnp.maximum(i - NB_P, 0)


MOD_COLS = 512


def _mods_body(cond_ref, w_ref, b_ref, o_ref):
    c = cond_ref[...]
    s = c * (1.0 / (1.0 + jnp.exp(-c)))
    s_hi, s_lo = _split_bf16(s)
    w_hi, w_lo = _split_bf16(w_ref[...])
    o_ref[...] = _dot(s_hi, w_hi) + _dot(s_lo, w_hi) + _dot(s_hi, w_lo) + b_ref[...]


def _mods_call(cond8, w_mod, b_mod3):
    n_col = 6 * D_MODEL // MOD_COLS
    return pl.pallas_call(
        _mods_body,
        grid=(DEPTH, n_col),
        in_specs=[
            pl.BlockSpec((SUB, D_MODEL), lambda l, j: (0, 0)),
            pl.BlockSpec((None, D_MODEL, MOD_COLS), lambda l, j: (l, 0, j)),
            pl.BlockSpec((None, 1, MOD_COLS), lambda l, j: (l, 0, j)),
        ],
        out_specs=pl.BlockSpec((None, SUB, MOD_COLS), lambda l, j: (l, 0, j)),
        out_shape=jax.ShapeDtypeStruct((DEPTH, SUB, 6 * D_MODEL), F32),
        name="adaln_mods",
    )(cond8, w_mod, b_mod3)


def _head_norm(t, g, gm):
    hi, lo = _split_bf16(t * t)
    msq = (_dot(hi, gm) + _dot(lo, gm)) * (1.0 / HEAD_DIM)
    return t * lax.rsqrt(msq + EPS) * g


def _rope(t, cos, sin):
    w = t.shape[1]
    lane = lax.broadcasted_iota(I32, t.shape, 1)
    first = (lane % 32) < 16
    swapped = jnp.where(first, pltpu.roll(t, w - 16, 1), pltpu.roll(t, 16, 1))
    return t * cos + swapped * sin


def _k1_body(x_ref, mod_ref, g1_ref, win_ref, qg_ref, kg_ref, gm_ref, cos_ref, sin_ref, *rest):
    uf_ref, naq_ref, gq_ref, up_ref, nakv_ref, gkv_ref, nakv_s_ref, gkv_s_ref = rest[-8:]
    i = pl.program_id(0)
    sh1 = mod_ref[:, 0:D_MODEL]
    sc1 = mod_ref[:, D_MODEL:2 * D_MODEL]
    h = _rmsnorm_rows(x_ref[...], g1_ref[...]) * (1.0 + sc1) + sh1
    hb = h.astype(BF16)

    def proj(lo, hi):
        return _dot(hb, win_ref[:, lo:hi])

    uf_ref[...] = proj(0, 256).astype(BF16)
    naq_ref[...] = proj(256, 512).astype(BF16)
    na_k = proj(512, 768)
    na_v = proj(768, 1024)
    gq = _head_norm(proj(1024, 1280), qg_ref[...], gm_ref[...])
    gk = _head_norm(proj(1280, 1408), kg_ref[:, 0:KV_W], gm_ref[0:KV_W, 0:KV_W])
    gv = proj(1408, 1536)
    up_ref[...] = proj(1536, 1792)
    gq = _rope(gq, cos_ref[...], sin_ref[...])
    gk = _rope(gk, cos_ref[:, 0:KV_W], sin_ref[:, 0:KV_W])
    gq_ref[...] = gq.astype(BF16)

    @pl.when(i < NB_P)
    def _():
        nakv_ref[0] = na_k
        nakv_ref[1] = na_v
        gkv_ref[0] = gk
        gkv_ref[1] = gv

    @pl.when(i >= NB_P)
    def _():
        nakv_s_ref[0] = na_k.astype(BF16)
        nakv_s_ref[1] = na_v.astype(BF16)
        gkv_s_ref[0] = gk.astype(BF16)
        gkv_s_ref[1] = gv.astype(BF16)


def _k1_call(layer, x, mods_l, g1, w_in_bf, qg, kg, gm, cos, sin, na_glob, gqa_glob):
    rope_blk = lambda i: (jnp.where(i < NB_P, 0, 1 + (i - NB_P) % SUB_S), 0)
    ctx_blk = lambda i: (jnp.minimum(i, NB_P - 1), layer, 0, 0, 0)
    lat_blk = lambda i: (_latent_block(i) // SUB_S, 0, _latent_block(i) % SUB_S, 0)
    row256 = pl.BlockSpec((ROWS, GROUP_W), lambda i: (i, 0))
    in_specs = [
        pl.BlockSpec((ROWS, D_MODEL), lambda i: (i, 0)),
        pl.BlockSpec((None, 1, 6 * D_MODEL), lambda i: (_mod_row(i), 0, 0)),
        _full((1, D_MODEL)),
        _full((D_MODEL, PROJ_W)),
        _full((1, GROUP_W)),
        _full((1, GROUP_W)),
        _full((GROUP_W, GROUP_W)),
        pl.BlockSpec((ROWS, GROUP_W), rope_blk),
        pl.BlockSpec((ROWS, GROUP_W), rope_blk),
    ]
    in_specs += [pl.BlockSpec(memory_space=pl.ANY)] * 2
    args = [x, mods_l, g1, w_in_bf, qg, kg, gm, cos, sin, na_glob, gqa_glob]
    return pl.pallas_call(
        _k1_body,
        grid=(NB_TOK,),
        in_specs=in_specs,
        out_specs=[
            row256, row256, row256, row256,
            pl.BlockSpec((None, None, 2, ROWS, GROUP_W), ctx_blk),
            pl.BlockSpec((None, None, 2, ROWS, KV_W), ctx_blk),
            pl.BlockSpec((None, 2, ROWS, GROUP_W), lat_blk),
            pl.BlockSpec((None, 2, ROWS, KV_W), lat_blk),
        ],
        out_shape=[
            jax.ShapeDtypeStruct((T_ALL, GROUP_W), BF16),
            jax.ShapeDtypeStruct((T_ALL, GROUP_W), BF16),
            jax.ShapeDtypeStruct((T_ALL, GROUP_W), BF16),
            jax.ShapeDtypeStruct((T_ALL, GROUP_W), F32),
            jax.ShapeDtypeStruct((BATCH, DEPTH, 2, SEQ, GROUP_W), F32),
            jax.ShapeDtypeStruct((BATCH, DEPTH, 2, SEQ, KV_W), F32),
            jax.ShapeDtypeStruct((DEC_BATCH, 2, DEC_SEQ, GROUP_W), BF16),
            jax.ShapeDtypeStruct((DEC_BATCH, 2, DEC_SEQ, KV_W), BF16),
        ],
        input_output_aliases={9: 4, 10: 5},
        compiler_params=pltpu.CompilerParams(vmem_limit_bytes=VMEM_LIMIT),
        name="k1_norm_proj",
    )(*args)


def _linmix_body(seq, uf_ref, up_ref, bc_ref, bs_ref, cl_ref, sl_ref, wf_ref, band_ref, icnt_ref,
                 wp_ref, ps_ref, a_ref, d_ref):
    u = uf_ref[...]
    t1 = _dot(u, bc_ref[...]).astype(BF16)
    t2 = _dot(u, bs_ref[...]).astype(BF16)
    f = (_dot(cl_ref[...], t1) - _dot(sl_ref[...], t2)) * (1.0 / float(np.sqrt(seq * HEAD_DIM)))
    a_ref[...] = _dot(f.astype(BF16), wf_ref[...]).astype(BF16)

    up = up_ref[...]
    up_hi, up_lo = _split_bf16(up)
    group = lax.broadcasted_iota(I32, up.shape, 1) // HEAD_DIM
    win_sum = jnp.zeros_like(up)
    for gi in range(len(POOL_WINDOWS)):
        s = _dot(band_ref[gi], up_hi) + _dot(band_ref[gi], up_lo)
        win_sum = jnp.where(group == gi, s, win_sum)
    delta = win_sum * icnt_ref[...] - up
    d_ref[...] = (_dot(delta.astype(BF16), wp_ref[...]) * ps_ref[...]).astype(BF16)


def _linmix_call(uf, up, seq, first_blk, n_blk, tabs, wf_bf, wp_bd_bf, ps, name):
    bc, bs, cl, sl, bands, icnt = tabs
    blk = pl.BlockSpec((seq, GROUP_W), lambda i: (first_blk + i, 0))
    oblk = pl.BlockSpec((seq, GROUP_W), lambda i: (i, 0))
    return pl.pallas_call(
        functools.partial(_linmix_body, seq),
        grid=(n_blk,),
        in_specs=[
            blk, blk,
            _full((GROUP_W, GROUP_W)), _full((GROUP_W, GROUP_W)),
            _full((seq, seq)), _full((seq, seq)),
            _full((GROUP_W, GROUP_W)),
            _full((len(POOL_WINDOWS), seq, seq)),
            _full((seq, GROUP_W)),
            _full((GROUP_W, GROUP_W)),
            _full((1, GROUP_W)),
        ],
        out_specs=[oblk, oblk],
        out_shape=[jax.ShapeDtypeStruct((n_blk * seq, GROUP_W), BF16)] * 2,
        compiler_params=pltpu.CompilerParams(vmem_limit_bytes=VMEM_LIMIT),
        name=name,
    )(uf, up, bc, bs, cl, sl, wf_bf, bands, icnt, wp_bd_bf, ps)


def _attend(q, k, v):
    s = _dot_nt(q, k) * (HEAD_DIM ** -0.5)
    p, l = _softmax_rows(s)
    return _dot(p.astype(BF16), v) / l


def _attn_p_body(naq_ref, nakv_ref, gq_ref, gkv_ref, b_ref, c_ref):
    for h in range(N_HEADS):
        sl = slice(h * HEAD_DIM, (h + 1) * HEAD_DIM)
        k = nakv_ref[0, :, sl].astype(BF16)
        v = nakv_ref[1, :, sl].astype(BF16)
        b_ref[:, sl] = _attend(naq_ref[:, sl], k, v).astype(BF16)
    for h in range(N_HEADS):
        sl = slice(h * HEAD_DIM, (h + 1) * HEAD_DIM)
        kvh = h // 2
        ksl = slice(kvh * HEAD_DIM, (kvh + 1) * HEAD_DIM)
        k = gkv_ref[0, :, ksl].astype(BF16)
        v = gkv_ref[1, :, ksl].astype(BF16)
        c_ref[:, sl] = _attend(gq_ref[:, sl], k, v).astype(BF16)


def _attn_p_call(layer, naq, na_glob, gq, gqa_glob):
    row = pl.BlockSpec((SEQ, GROUP_W), lambda i: (i, 0))
    return pl.pallas_call(
        _attn_p_body,
        grid=(BATCH,),
        in_specs=[
            row,
            pl.BlockSpec((None, None, 2, SEQ, GROUP_W), lambda i: (i, layer, 0, 0, 0)),
            row,
            pl.BlockSpec((None, None, 2, SEQ, KV_W), lambda i: (i, layer, 0, 0, 0)),
        ],
        out_specs=[row, row],
        out_shape=[jax.ShapeDtypeStruct((T_P, GROUP_W), BF16)] * 2,
        name="attn_ctx",
    )(naq, na_glob, gq, gqa_glob)


N_DR = 2 * NA_KH - 1
N_DC = 2 * NA_KW - 1


def _nabias_body(rpb_ref, o_ref):
    lh = pl.program_id(0)
    q = lax.broadcasted_iota(I32, (GRID_W, LANES), 0)
    j = lax.broadcasted_iota(I32, (GRID_W, LANES), 1)
    k = j % GRID_W
    second = j >= GRID_W
    c0 = jnp.clip(q - NA_KW // 2, 0, GRID_W - NA_KW)
    valid = (k >= c0) & (k < c0 + NA_KW)
    diff = k - q + (NA_KW - 1)
    for dr in range(N_DR - 1):
        acc = jnp.zeros((GRID_W, LANES), F32)
        for dc in range(N_DC):
            val = jnp.where(second, rpb_ref[lh, (dr + 1) * N_DC + dc], rpb_ref[lh, dr * N_DC + dc])
            acc = jnp.where(diff == dc, val, acc)
        o_ref[dr] = jnp.where(valid, acc, NEG_INF)


def _nabias_call(rpb_flat):
    n = rpb_flat.shape[0]
    return pl.pallas_call(
        _nabias_body,
        grid=(n,),
        in_specs=[pl.BlockSpec(memory_space=pltpu.SMEM)],
        out_specs=pl.BlockSpec((None, N_DR - 1, GRID_W, LANES), lambda i: (i, 0, 0, 0)),
        out_shape=jax.ShapeDtypeStruct((n, N_DR - 1, GRID_W, LANES), F32),
        name="na_bias_tiles",
    )(rpb_flat)


N_WIN = NA_KH * GRID_W


def _na_s_body(q_ref, kv_ref, ckv_ref, bias_ref, o_ref):
    r = pl.program_id(1)
    r0 = jnp.clip(r - NA_KH // 2, 0, GRID_ROWS - NA_KH)
    start = pl.multiple_of(r0 * GRID_W, GRID_W)
    dr0 = r0 - r + (NA_KH - 1)
    scale = HEAD_DIM ** -0.5
    for h in range(N_HEADS):
        sl = slice(h * HEAD_DIM, (h + 1) * HEAD_DIM)
        q = q_ref[:, sl]
        kw = kv_ref[0, pl.ds(start, N_WIN), sl]
        vw = kv_ref[1, pl.ds(start, N_WIN), sl]
        kc = ckv_ref[0, :, sl].astype(BF16)
        vc = ckv_ref[1, :, sl].astype(BF16)
        bias = jnp.concatenate([bias_ref[h, dr0 + i] for i in range(0, NA_KH, 2)], axis=1)
        s_win = _dot_nt(q, kw) * scale + bias
        s_ctx = _dot_nt(q, kc) * scale
        s = jnp.concatenate([s_win, s_ctx], axis=1)
        p, l = _softmax_rows(s)
        pb = p.astype(BF16)
        o = _dot(pb[:, 0:N_WIN], vw) + _dot(pb[:, N_WIN:], vc)
        o_ref[:, sl] = (o / l).astype(BF16)


def _na_s_call(naq, nakv_s, cache_na_l, bias_l):
    first = T_P // GRID_W
    return pl.pallas_call(
        _na_s_body,
        grid=(DEC_BATCH, GRID_ROWS),
        in_specs=[
            pl.BlockSpec((GRID_W, GROUP_W), lambda b, r: (first + b * GRID_ROWS + r, 0)),
            pl.BlockSpec((None, 2, DEC_SEQ, GROUP_W), lambda b, r: (b, 0, 0, 0)),
            pl.BlockSpec((None, 2, PAST_LEN, GROUP_W), lambda b, r: (b, 0, 0, 0)),
            _full((N_HEADS, N_DR - 1, GRID_W, LANES)),
        ],
        out_specs=pl.BlockSpec((GRID_W, GROUP_W), lambda b, r: (b * GRID_ROWS + r, 0)),
        out_shape=jax.ShapeDtypeStruct((T_S, GROUP_W), BF16),
        compiler_params=pltpu.CompilerParams(vmem_limit_bytes=VMEM_LIMIT),
        name="attn_na_latent",
    )(naq, nakv_s, cache_na_l, bias_l)


GQ_ROWS = 128


def _gqa_s_body(q_ref, kv_ref, ckv_ref, o_ref):
    scale = HEAD_DIM ** -0.5
    for h in range(N_HEADS):
        sl = slice(h * HEAD_DIM, (h + 1) * HEAD_DIM)
        kvh = h // 2
        ksl = slice(kvh * HEAD_DIM, (kvh + 1) * HEAD_DIM)
        q = q_ref[:, sl]
        kl = kv_ref[0, :, ksl]
        vl = kv_ref[1, :, ksl]
        kc = ckv_ref[0, :, ksl].astype(BF16)
        vc = ckv_ref[1, :, ksl].astype(BF16)
        s = jnp.concatenate([_dot_nt(q, kl), _dot_nt(q, kc)], axis=1) * scale
        p, l = _softmax_rows(s)
        pb = p.astype(BF16)
        o = _dot(pb[:, 0:DEC_SEQ], vl) + _dot(pb[:, DEC_SEQ:], vc)
        o_ref[:, sl] = (o / l).astype(BF16)


def _gqa_s_call(gq, gkv_s, cache_gqa_l):
    nq = DEC_SEQ // GQ_ROWS
    first = T_P // GQ_ROWS
    return pl.pallas_call(
        _gqa_s_body,
        grid=(DEC_BATCH, nq),
        in_specs=[
            pl.BlockSpec((GQ_ROWS, GROUP_W), lambda b, j: (first + b * nq + j, 0)),
            pl.BlockSpec((None, 2, DEC_SEQ, KV_W), lambda b, j: (b, 0, 0, 0)),
            pl.BlockSpec((None, 2, PAST_LEN, KV_W), lambda b, j: (b, 0, 0, 0)),
        ],
        out_specs=pl.BlockSpec((GQ_ROWS, GROUP_W), lambda b, j: (b * nq + j, 0)),
        out_shape=jax.ShapeDtypeStruct((T_S, GROUP_W), BF16),
        compiler_params=pltpu.CompilerParams(vmem_limit_bytes=VMEM_LIMIT),
        name="attn_gqa_latent",
    )(gq, gkv_s, cache_gqa_l)


def _k3_body(x_ref, ap_ref, bp_ref, cp_ref, dp_ref, as_ref, bs_ref, cs_ref, ds_ref, mod_ref, wout_ref, g2_ref,
             rwh_ref, rwl_ref, rb_ref, xmid_ref, h2_ref, te_ref, tg_ref, pr_ref, cnt_ref):
    is_ctx = pl.program_id(0) < NB_P
    acc = jnp.zeros((ROWS, D_MODEL), F32)
    for g, (p_ref, s_ref) in enumerate(((ap_ref, as_ref), (bp_ref, bs_ref), (cp_ref, cs_ref), (dp_ref, ds_ref))):
        mixed = jnp.where(is_ctx, p_ref[...], s_ref[...])
        acc += _dot(mixed, wout_ref[g * GROUP_W:(g + 1) * GROUP_W, :])
    gate1 = mod_ref[:, 2 * D_MODEL:3 * D_MODEL]
    sh2 = mod_ref[:, 3 * D_MODEL:4 * D_MODEL]
    sc2 = mod_ref[:, 4 * D_MODEL:5 * D_MODEL]
    xm = x_ref[...] + gate1 * acc
    xmid_ref[...] = xm
    h2 = _rmsnorm_rows(xm, g2_ref[...]) * (1.0 + sc2) + sh2
    hi, lo = _split_bf16(h2)
    h2_ref[...] = hi
    logits = _dot(hi, rwh_ref[...]) + _dot(lo, rwh_ref[...]) + _dot(hi, rwl_ref[...]) + rb_ref[...]

    lane = lax.broadcasted_iota(I32, logits.shape, 1)
    cur = logits
    vals, idxs = [], []
    for _ in range(TOP_K):
        m = jnp.max(cur, axis=-1, keepdims=True)
        idx = jnp.min(jnp.where(cur == m, lane, LANES), axis=-1, keepdims=True)
        vals.append(m)
        idxs.append(idx)
        cur = jnp.where(lane == idx, -jnp.inf, cur)
    exps = [jnp.exp(v - vals[0]) for v in vals]
    tot = exps[0] + exps[1] + exps[2] + exps[3]
    te = jnp.zeros(logits.shape, I32)
    tg = jnp.zeros(logits.shape, F32)
    for k in range(TOP_K):
        te = jnp.where(lane == k, idxs[k], te)
        tg = jnp.where(lane == k, exps[k] / tot, tg)
    te_ref[...] = te
    tg_ref[...] = tg

    member = jnp.zeros(logits.shape, F32)
    for k in range(TOP_K):
        member = jnp.where(lane == idxs[k], 1.0, member)
    rr = lax.broadcasted_iota(I32, (ROWS, ROWS), 0)
    cc = lax.broadcasted_iota(I32, (ROWS, ROWS), 1)
    earlier = jnp.where(cc < rr, 1.0, 0.0).astype(BF16)
    prefix = _dot(earlier, member.astype(BF16))
    cnt = jnp.sum(member, axis=0, keepdims=True)
    er = lax.broadcasted_iota(I32, (LANES, LANES), 0)
    ec = lax.broadcasted_iota(I32, (LANES, LANES), 1)
    lower = jnp.where(er < ec, 1.0, 0.0).astype(BF16)
    loc = _dot(jnp.broadcast_to(cnt, (SUB, LANES)).astype(BF16), lower)[0:1, :]
    pr = jnp.zeros(logits.shape, F32)
    for k in range(TOP_K):
        hit = lane == idxs[k]
        rank = jnp.sum(jnp.where(hit, prefix, 0.0), axis=-1, keepdims=True)
        start = jnp.sum(jnp.where(hit, loc, 0.0), axis=-1, keepdims=True)
        pr = jnp.where(lane == k, start + rank, pr)
        pr = jnp.where(lane == TOP_K + k, rank, pr)
    pr_ref[...] = pr.astype(I32)
    cnt_ref[...] = cnt.astype(I32)


def _k3_call(x, mix_p, mix_s, mods_l, w_out_bf, g2, rwh, rwl, rb):
    row_p = pl.BlockSpec((ROWS, GROUP_W), lambda i: (jnp.minimum(i, NB_P - 1), 0))
    row_s = pl.BlockSpec((ROWS, GROUP_W), lambda i: (_latent_block(i), 0))
    rowd = pl.BlockSpec((ROWS, D_MODEL), lambda i: (i, 0))
    row128 = pl.BlockSpec((ROWS, LANES), lambda i: (i, 0))
    return pl.pallas_call(
        _k3_body,
        grid=(NB_TOK,),
        in_specs=[
            rowd, row_p, row_p, row_p, row_p, row_s, row_s, row_s, row_s,
            pl.BlockSpec((None, 1, 6 * D_MODEL), lambda i: (_mod_row(i), 0, 0)),
            _full((D_MODEL, D_MODEL)),
            _full((1, D_MODEL)),
            _full((D_MODEL, LANES)), _full((D_MODEL, LANES)), _full((1, LANES)),
        ],
        out_specs=[rowd, rowd, row128, row128, row128,
                   pl.BlockSpec((None, 1, LANES), lambda i: (i, 0, 0))],
        out_shape=[
            jax.ShapeDtypeStruct((T_ALL, D_MODEL), F32),
            jax.ShapeDtypeStruct((T_ALL, D_MODEL), BF16),
            jax.ShapeDtypeStruct((T_ALL, LANES), I32),
            jax.ShapeDtypeStruct((T_ALL, LANES), F32),
            jax.ShapeDtypeStruct((T_ALL, LANES), I32),
            jax.ShapeDtypeStruct((NB_TOK, 1, LANES), I32),
        ],
        compiler_params=pltpu.CompilerParams(vmem_limit_bytes=VMEM_LIMIT),
        name="k3_out_router",
    )(x, *mix_p, *mix_s, mods_l, w_out_bf, g2, rwh, rwl, rb)


TAB_CNT, TAB_LOC, TAB_BASE, TAB_LSTAGE, TAB_SOFF, TAB_TOTAL, TAB_NCHUNK = range(7)


def _routing_tables(cnt):
    c = cnt.reshape(NB_TOK, LANES)[:, :N_EXPERTS]
    counts = jnp.sum(c, axis=0)
    padded = (counts + (SEG - 1) + (MOE_ROWS - 1)) // MOE_ROWS * MOE_ROWS
    pend = jnp.cumsum(padded)
    pstart = pend - padded
    base = pstart[None, :] + jnp.cumsum(c, axis=0) - c
    loc = jnp.cumsum(c, axis=1) - c
    off = base % SUB
    nch = jnp.where(c > 0, (off + c + SEG - 1) // SEG, 0)
    lstage = (jnp.cumsum(nch, axis=1) - nch) * SEG
    soff = lstage + off
    total = jnp.sum(nch, axis=1, keepdims=True) * SEG

    def pad(t):
        return jnp.zeros((NB_TOK, LANES), I32).at[:, :t.shape[1]].set(t.astype(I32))

    tab = jnp.stack([pad(c), pad(loc), pad(base), pad(lstage), pad(soff), pad(total), pad(nch),
                     jnp.zeros((NB_TOK, LANES), I32)], axis=1)
    ztab = jnp.zeros((SUB, LANES), I32).at[0, :N_EXPERTS].set((pstart + counts).astype(I32))
    ztab = ztab.at[1, :N_EXPERTS].set(pend.astype(I32))
    ztab = ztab.at[2, 0].set((pend[-1] // MOE_ROWS).astype(I32))
    first_blk = (pstart // MOE_ROWS).astype(I32)
    n_blk = (padded // MOE_ROWS).astype(I32)
    n_used = (pend[-1] // MOE_ROWS).astype(I32).reshape(1)
    return tab, ztab, first_blk, n_blk, n_used


def _dispatch_body(tab_ref, ztab_ref, h2_ref, pr_ref, xs_ref, stage0, stage1, zbuf, pending, sems):
    b = pl.program_id(0)
    sem = sems.at[0]

    def rows_copy(src, src_row, dst_row, n_rows, sem=sem):
        return pltpu.make_async_copy(
            src.at[pl.ds(pl.multiple_of(src_row * SUB, SUB), n_rows * SUB)],
            xs_ref.at[pl.ds(pl.multiple_of(dst_row * SUB, SUB), n_rows * SUB)], sem)

    def seg_copy(src, src_row, dst_row, sem=sem):
        return rows_copy(src, src_row, dst_row, SEG, sem)

    def drain(n, sem=sem):
        def body(j, carry):
            seg_copy(zbuf, 0, 0, sem).wait()
            return carry
        lax.fori_loop(0, n, body, 0)

    @pl.when(b == 0)
    def _():
        zbuf[...] = jnp.zeros(zbuf.shape, F32)
        stage0[pl.ds(BLOCK_ASSIGN * SUB, SEG * SUB), :] = jnp.zeros((SEG * SUB, LANES), F32)
        stage1[pl.ds(BLOCK_ASSIGN * SUB, SEG * SUB), :] = jnp.zeros((SEG * SUB, LANES), F32)

        def per_expert(e, carry):
            n_seg, n_row = carry
            end = ztab_ref[0, e]
            length = ztab_ref[1, e] - end
            n = lax.shift_right_logical(length, SEG_SHIFT)
            rem = length & (SEG - 1)

            def one(j, c):
                seg_copy(zbuf, 0, end + j * SEG).start()
                return c
            lax.fori_loop(0, n, one, 0)

            def one_row(j, c):
                rows_copy(zbuf, 0, end + n * SEG + j, 1).start()
                return c
            lax.fori_loop(0, rem, one_row, 0)
            return n_seg + n, n_row + rem
        n_seg, n_row = lax.fori_loop(0, N_EXPERTS, per_expert, (0, 0))
        drain(n_seg)

        def row_wait(j, c):
            rows_copy(zbuf, 0, 0, 1).wait()
            return c
        lax.fori_loop(0, n_row, row_wait, 0)

        first_free = ztab_ref[2, 0]

        def fill(j, c):
            rows_copy(zbuf, 0, (first_free + j) * MOE_ROWS, MOE_ROWS).start()
            return c
        lax.fori_loop(0, N_BLOCKS - first_free, fill, 0)

        def fill_wait(j, c):
            rows_copy(zbuf, 0, 0, MOE_ROWS).wait()
            return c
        lax.fori_loop(0, N_BLOCKS - first_free, fill_wait, 0)

    pos_t = pr_ref[...].astype(F32).T
    h2 = h2_ref[...]

    def sort(stage):
        for c in range(BLOCK_ASSIGN // ROWS):
            p = (lax.broadcasted_iota(I32, (ROWS, ROWS), 0) + c * ROWS).astype(F32)
            sel = p == pos_t[0:1, :]
            for k in range(1, TOP_K):
                sel = sel | (p == pos_t[k:k + 1, :])
            rows = _dot(jnp.where(sel, 1.0, 0.0).astype(BF16), h2)
            for s in range(ROW_TILES):
                stage[pl.ds(c * ROWS * SUB + s, ROWS, stride=SUB), :] = rows[:, s * LANES:(s + 1) * LANES]

    def send(stage, sem):
        def per_expert(e, tot):
            loc = tab_ref[TAB_LOC, e]
            base = tab_ref[TAB_BASE, e]
            n = lax.shift_right_logical(tab_ref[TAB_CNT, e] + (SEG - 1), SEG_SHIFT)

            def one(j, carry):
                seg_copy(stage, loc + j * SEG, base + j * SEG, sem).start()
                return carry
            lax.fori_loop(0, n, one, 0)
            return tot + n
        return lax.fori_loop(0, N_EXPERTS, per_expert, 0)

    for slot, stage in enumerate((stage0, stage1)):
        @pl.when((b & 1) == slot)
        def _(stage=stage, slot=slot):
            sort(stage)

            @pl.when(b > 0)
            def _():
                drain(pending[0], sems.at[1 - slot])
            n_sent = send(stage, sems.at[slot])
            pending[0] = n_sent

            @pl.when(b == NB_TOK - 1)
            def _():
                drain(n_sent, sems.at[slot])


def _dispatch_call(tab, ztab, h2, pr):
    return pl.pallas_call(
        _dispatch_body,
        grid=(NB_TOK,),
        in_specs=[
            pl.BlockSpec((None, SUB, LANES), lambda i: (i, 0, 0), memory_space=pltpu.SMEM),
            pl.BlockSpec(memory_space=pltpu.SMEM),
            pl.BlockSpec((ROWS, D_MODEL), lambda i: (i, 0)),
            pl.BlockSpec((ROWS, LANES), lambda i: (i, 0)),
        ],
        out_specs=pl.BlockSpec(memory_space=pl.ANY),
        out_shape=jax.ShapeDtypeStruct((N_SORTED * ROW_TILES, LANES), F32),
        scratch_shapes=[
            pltpu.VMEM(((BLOCK_ASSIGN + SEG) * SUB, LANES), F32),
            pltpu.VMEM(((BLOCK_ASSIGN + SEG) * SUB, LANES), F32),
            pltpu.VMEM((MOE_ROWS * SUB, LANES), F32),
            pltpu.SMEM((1,), I32),
            pltpu.SemaphoreType.DMA((2,)),
        ],
        compiler_params=pltpu.CompilerParams(vmem_limit_bytes=VMEM_LIMIT),
        name="moe_dispatch",
    )(tab, ztab, h2, pr)


HALF_FF = D_FF // 2


def _moe_body(fb_ref, nb_ref, nu_ref, xs_ref, wgu_ref, bgu_ref, wd_ref, bd_ref, yp_ref,
              wgu_bf, wd_bf, wd_scr, xb_scr, act_scr, xbuf0, xbuf1, ybuf0, ybuf1, xsems, ysems):
    e = pl.program_id(0)
    first = fb_ref[e]
    n_blk = nb_ref[e]
    n_used = nu_ref[0]
    xbufs = (xbuf0, xbuf1)
    ybufs = (ybuf0, ybuf1)

    def x_copy(g, slot):
        src = xs_ref.at[pl.ds(pl.multiple_of(g * (MOE_ROWS * SUB), MOE_ROWS * SUB), MOE_ROWS * SUB)]
        return pltpu.make_async_copy(src, xbufs[slot], xsems.at[slot])

    def y_copy(g, slot):
        dst = yp_ref.at[pl.ds(pl.multiple_of(g * MOE_ROWS, MOE_ROWS), MOE_ROWS)]
        return pltpu.make_async_copy(ybufs[slot], dst, ysems.at[slot])

    @pl.when(e == 0)
    def _():
        x_copy(0, 0).start()

    @pl.when(n_blk > 0)
    def _():
        for c in range(4):
            cs = slice(c * 512, (c + 1) * 512)
            wgu_bf[:, cs] = wgu_ref[:, cs].astype(BF16)
        for c in range(ROW_TILES):
            cs = slice(c * LANES, (c + 1) * LANES)
            wd_scr[pl.ds(0, HALF_FF, stride=2), :] = wd_ref[0:HALF_FF, cs]
            wd_scr[pl.ds(1, HALF_FF, stride=2), :] = wd_ref[HALF_FF:D_FF, cs]
            wd_bf[:, cs] = wd_scr[...].astype(BF16)

    def ffn(x_ref, y_ref):
        for s in range(ROW_TILES):
            xb_scr[:, s * LANES:(s + 1) * LANES] = x_ref[pl.ds(s, MOE_ROWS, stride=ROW_TILES), :].astype(BF16)
        even = (lax.broadcasted_iota(I32, (MOE_ROWS, LANES), 1) % 2) == 0
        for c in range(D_FF // GROUP_W):
            ca = slice(c * GROUP_W, (c + 1) * GROUP_W)
            cb = slice(D_FF + c * GROUP_W, D_FF + (c + 1) * GROUP_W)
            ha_full = _dot(xb_scr[...], wgu_bf[:, ca]) + bgu_ref[:, ca]
            hb_full = _dot(xb_scr[...], wgu_bf[:, cb]) + bgu_ref[:, cb]
            for h in range(GROUP_W // LANES):
                ha = ha_full[:, h * LANES:(h + 1) * LANES]
                hb = hb_full[:, h * LANES:(h + 1) * LANES]
                gate = jnp.where(even, ha, pltpu.roll(hb, 1, 1))
                up = jnp.where(even, pltpu.roll(ha, LANES - 1, 1), hb)
                gate = jnp.minimum(gate, SWIGLU_LIMIT)
                up = jnp.clip(up, -SWIGLU_LIMIT, SWIGLU_LIMIT)
                glu = gate * (1.0 / (1.0 + jnp.exp(-gate * SWIGLU_ALPHA)))
                lo = c * GROUP_W + h * LANES
                act_scr[:, lo:lo + LANES] = ((up + 1.0) * glu).astype(BF16)
        for c in range(D_MODEL // GROUP_W):
            cs = slice(c * GROUP_W, (c + 1) * GROUP_W)
            y_ref[:, cs] = _dot(act_scr[...], wd_bf[:, cs]) + bd_ref[:, cs]

    def block(j, carry):
        g = first + j
        for slot in range(2):
            @pl.when((g & 1) == slot)
            def _(slot=slot):
                x_copy(g, slot).wait()

                @pl.when(g + 1 < n_used)
                def _():
                    x_copy(g + 1, 1 - slot).start()
                @pl.when(g >= 2)
                def _():
                    y_copy(g - 2, slot).wait()
                ffn(xbufs[slot], ybufs[slot])
                y_copy(g, slot).start()
        return carry
    lax.fori_loop(0, n_blk, block, 0)

    @pl.when(e == N_EXPERTS - 1)
    def _():
        for slot in range(2):
            @pl.when(((n_used - 1) & 1) == slot)
            def _(slot=slot):
                @pl.when(n_used >= 2)
                def _():
                    y_copy(n_used - 2, 1 - slot).wait()
                y_copy(n_used - 1, slot).wait()
        ybuf0[...] = jnp.zeros(ybuf0.shape, F32)

        def fill(j, carry):
            y_copy(n_used + j, 0).start()
            return carry
        lax.fori_loop(0, N_BLOCKS - n_used, fill, 0)

        def fill_wait(j, carry):
            y_copy(0, 0).wait()
            return carry
        lax.fori_loop(0, N_BLOCKS - n_used, fill_wait, 0)


def _moe_call(layer, first_blk, n_blk, n_used, xs, w_gu, b_gu4, w_down, b_down4):
    wmap = lambda e, fb, nb, nu: (layer, e, 0, 0)
    grid_spec = pltpu.PrefetchScalarGridSpec(
        num_scalar_prefetch=3,
        grid=(N_EXPERTS,),
        in_specs=[
            pl.BlockSpec(memory_space=pl.ANY),
            pl.BlockSpec((None, None, D_MODEL, 2 * D_FF), wmap),
            pl.BlockSpec((None, None, 1, 2 * D_FF), wmap),
            pl.BlockSpec((None, None, D_FF, D_MODEL), wmap),
            pl.BlockSpec((None, None, 1, D_MODEL), wmap),
        ],
        out_specs=pl.BlockSpec(memory_space=pl.ANY),
        scratch_shapes=[
            pltpu.VMEM((D_MODEL, 2 * D_FF), BF16),
            pltpu.VMEM((D_FF, D_MODEL), BF16),
            pltpu.VMEM((D_FF, LANES), F32),
            pltpu.VMEM((MOE_ROWS, D_MODEL), BF16),
            pltpu.VMEM((MOE_ROWS, D_FF), BF16),
            pltpu.VMEM((MOE_ROWS * SUB, LANES), F32),
            pltpu.VMEM((MOE_ROWS * SUB, LANES), F32),
            pltpu.VMEM((MOE_ROWS, D_MODEL), F32),
            pltpu.VMEM((MOE_ROWS, D_MODEL), F32),
            pltpu.SemaphoreType.DMA((2,)),
            pltpu.SemaphoreType.DMA((2,)),
        ],
    )
    return pl.pallas_call(
        _moe_body,
        grid_spec=grid_spec,
        out_shape=jax.ShapeDtypeStruct((N_SORTED, D_MODEL), F32),
        compiler_params=pltpu.CompilerParams(vmem_limit_bytes=VMEM_LIMIT),
        name="moe_experts",
    )(first_blk, n_blk, n_used, xs, w_gu, b_gu4, w_down, b_down4)


def _combine_body(final, tab_ref, tabn_ref, tabv_ref, yp_ref, te_ref, tg_ref, pr_ref, xmid_ref, mod_ref, fg_ref,
                  o_ref, stage0, stage1, acc, sems):
    b = pl.program_id(0)
    stages = (stage0, stage1)

    def chunk_copy(src_row, dst_row, slot):
        return pltpu.make_async_copy(
            yp_ref.at[pl.ds(pl.multiple_of(src_row, SUB), SEG)],
            stages[slot].at[pl.ds(pl.multiple_of(dst_row, SUB), SEG)], sems.at[slot])

    def fetch(tab, slot):
        def per_expert(e, carry):
            base = tab[TAB_BASE, e]
            off = base & (SUB - 1)
            lstage = tab[TAB_LSTAGE, e]

            def one(j, c):
                chunk_copy(base - off + j * SEG, lstage + j * SEG, slot).start()
                return c
            lax.fori_loop(0, tab[TAB_NCHUNK, e], one, 0)
            return carry
        lax.fori_loop(0, N_EXPERTS, per_expert, 0)

        total = tab[TAB_TOTAL, 0]

        def zero_tail(j, c):
            stages[slot][pl.ds(pl.multiple_of(total + j * SEG, SEG), SEG), :] = jnp.zeros((SEG, D_MODEL), F32)
            return c
        lax.fori_loop(0, lax.shift_right_logical(STAGE_ROWS - total, SEG_SHIFT), zero_tail, 0)

    @pl.when(b == 0)
    def _():
        fetch(tab_ref, 0)

    for slot in range(2):
        @pl.when(((b & 1) == slot) & (b + 1 < NB_TOK))
        def _(slot=slot):
            fetch(tabn_ref, 1 - slot)

    total = tab_ref[TAB_TOTAL, 0]
    for slot in range(2):
        @pl.when((b & 1) == slot)
        def _(slot=slot):
            def drain(j, c):
                chunk_copy(0, 0, slot).wait()
                return c
            lax.fori_loop(0, lax.shift_right_logical(total, SEG_SHIFT), drain, 0)

    lane = lax.broadcasted_iota(I32, (ROWS, LANES), 1)
    te = te_ref[...]
    pr = pr_ref[...]
    soff = tabv_ref[TAB_SOFF:TAB_SOFF + 1, :].astype(F32)
    packed = jnp.where((lane >= TOP_K) & (lane < 2 * TOP_K), pltpu.roll(tg_ref[...], TOP_K, 1), 0.0)
    sp = []
    for k in range(TOP_K):
        seg_start = jnp.sum(jnp.where(lane == te[:, k:k + 1], soff, 0.0), axis=-1, keepdims=True)
        spk = seg_start + pr[:, TOP_K + k:TOP_K + k + 1].astype(F32)
        sp.append(spk)
        packed = jnp.where(lane == k, spk, packed)
    packed_t = packed.T

    def unsort(stage, c):
        pcol = (lax.broadcasted_iota(I32, (ROWS, ROWS), 1) + c * ROWS).astype(F32)
        sel = pcol == sp[0]
        for k in range(1, TOP_K):
            sel = sel | (pcol == sp[k])
        prow = (lax.broadcasted_iota(I32, (ROWS, ROWS), 0) + c * ROWS).astype(F32)
        gmat = jnp.zeros((ROWS, ROWS), F32)
        for k in range(TOP_K):
            gmat = jnp.where(prow == packed_t[k:k + 1, :], packed_t[TOP_K + k:TOP_K + k + 1, :], gmat)
        row_gate = jnp.sum(gmat, axis=-1, keepdims=True)
        rows = (stage[c * ROWS:(c + 1) * ROWS, :] * row_gate).astype(BF16)
        return _dot(jnp.where(sel, 1.0, 0.0).astype(BF16), rows)

    n_always = BLOCK_ASSIGN // ROWS
    for slot in range(2):
        @pl.when((b & 1) == slot)
        def _(slot=slot):
            out = unsort(stages[slot], 0)
            for c in range(1, n_always):
                out += unsort(stages[slot], c)
            acc[...] = out
            for c in range(n_always, STAGE_ROWS // ROWS):
                @pl.when(c * ROWS < total)
                def _(c=c):
                    acc[...] += unsort(stages[slot], c)

    gate2 = mod_ref[:, 5 * D_MODEL:6 * D_MODEL]
    x = xmid_ref[...] + gate2 * acc[...]
    if final:
        x = _rmsnorm_rows(x, fg_ref[...])
    o_ref[...] = x


def _combine_call(tab, yp, te, tg, pr, xmid, mods_l, final_g, final):
    rowd = pl.BlockSpec((ROWS, D_MODEL), lambda i: (i, 0))
    row128 = pl.BlockSpec((ROWS, LANES), lambda i: (i, 0))
    return pl.pallas_call(
        functools.partial(_combine_body, final),
        grid=(NB_TOK,),
        in_specs=[
            pl.BlockSpec((None, SUB, LANES), lambda i: (i, 0, 0), memory_space=pltpu.SMEM),
            pl.BlockSpec((None, SUB, LANES), lambda i: (jnp.minimum(i + 1, NB_TOK - 1), 0, 0),
                         memory_space=pltpu.SMEM),
            pl.BlockSpec((None, SUB, LANES), lambda i: (i, 0, 0)),
            pl.BlockSpec(memory_space=pl.ANY),
            row128, row128, row128,
            rowd,
            pl.BlockSpec((None, 1, 6 * D_MODEL), lambda i: (_mod_row(i), 0, 0)),
            _full((1, D_MODEL)),
        ],
        out_specs=rowd,
        out_shape=jax.ShapeDtypeStruct((T_ALL, D_MODEL), F32),
        scratch_shapes=[
            pltpu.VMEM((STAGE_ROWS, D_MODEL), F32),
            pltpu.VMEM((STAGE_ROWS, D_MODEL), F32),
            pltpu.VMEM((ROWS, D_MODEL), F32),
            pltpu.SemaphoreType.DMA((2,)),
        ],
        compiler_params=pltpu.CompilerParams(vmem_limit_bytes=VMEM_LIMIT),
        name="moe_combine",
    )(tab, tab, tab, yp, te, tg, pr, xmid, mods_l, final_g)


def kernel(x_prompt, x_sample, cache_na_kv, cache_gqa_kv, c, c_ctx, norm1_g, norm2_g, w_mod, b_mod,
           w_in, w_fourier, na_rpb, q_norm_g, k_norm_g, w_pool, pool_scale, w_out, router_w, router_b,
           w_gu, b_gu, w_down, b_down, final_g):
    def lin_tables(L):
        cl, sl, bc, bs = _dft_tables(L)
        bands, icnt = _pool_tables(L)
        return (jnp.asarray(bc).astype(BF16), jnp.asarray(bs).astype(BF16),
                jnp.asarray(cl).astype(BF16), jnp.asarray(sl).astype(BF16),
                jnp.asarray(bands).astype(BF16), jnp.asarray(icnt))

    tabs_p = lin_tables(SEQ)
    tabs_s = lin_tables(DEC_SEQ)
    cos_np, sin_np = _rope_tables()
    cos_t, sin_t = jnp.asarray(cos_np), jnp.asarray(sin_np)
    gm = jnp.asarray(_group_mean_matrix(GROUP_W)).astype(BF16)

    cond8 = jnp.zeros((SUB, D_MODEL), F32).at[0].set(c_ctx).at[1:1 + DEC_BATCH].set(c)
    mods = _mods_call(cond8, w_mod, b_mod.reshape(DEPTH, 1, 6 * D_MODEL))
    mods = mods.reshape(DEPTH, SUB, 1, 6 * D_MODEL)

    bias_tiles = _nabias_call(na_rpb.reshape(DEPTH * N_HEADS, N_DR * N_DC))
    bias_tiles = bias_tiles.reshape(DEPTH, N_HEADS, N_DR - 1, GRID_W, LANES)

    w_in_bf = w_in.astype(BF16)
    w_out_bf = w_out.astype(BF16)
    w_f_bf = w_fourier.astype(BF16)
    eye4 = jnp.eye(4, dtype=F32)
    w_pool_bd = jnp.einsum('lgce,gh->lgche', w_pool, eye4).reshape(DEPTH, GROUP_W, GROUP_W).astype(BF16)
    qg = jnp.tile(q_norm_g, (1, N_HEADS)).reshape(DEPTH, 1, GROUP_W)
    kg = jnp.tile(k_norm_g, (1, N_HEADS)).reshape(DEPTH, 1, GROUP_W)
    rw_pad = jnp.zeros((DEPTH, D_MODEL, LANES), F32).at[:, :, :N_EXPERTS].set(router_w)
    rw_hi = rw_pad.astype(BF16)
    rw_lo = (rw_pad - rw_hi.astype(F32)).astype(BF16)
    rb_pad = jnp.full((DEPTH, 1, LANES), NEG_INF, F32).at[:, 0, :N_EXPERTS].set(router_b)
    b_gu4 = b_gu.reshape(DEPTH, N_EXPERTS, 1, 2 * D_FF)
    b_down4 = b_down.reshape(DEPTH, N_EXPERTS, 1, D_MODEL)
    final_g2 = final_g.reshape(1, D_MODEL)
    cache_na = cache_na_kv.reshape(DEC_BATCH, DEPTH, 2, PAST_LEN, GROUP_W)
    cache_gqa = cache_gqa_kv.reshape(DEC_BATCH, DEPTH, 2, PAST_LEN, KV_W)

    x = jnp.concatenate([x_prompt.reshape(T_P, D_MODEL), x_sample.reshape(T_S, D_MODEL)], axis=0)
    na_glob = jnp.zeros((BATCH, DEPTH, 2, SEQ, GROUP_W), F32)
    gqa_glob = jnp.zeros((BATCH, DEPTH, 2, SEQ, KV_W), F32)
    for l in range(DEPTH):
        g1 = norm1_g[l].reshape(1, D_MODEL)
        g2 = norm2_g[l].reshape(1, D_MODEL)
        ps = pool_scale[l].reshape(1, GROUP_W)

        uf, naq, gq, up, na_glob, gqa_glob, nakv_s, gkv_s = _k1_call(
            l, x, mods[l], g1, w_in_bf[l], qg[l], kg[l], gm, cos_t, sin_t, na_glob, gqa_glob)

        a_p, d_p = _linmix_call(uf, up, SEQ, 0, BATCH, tabs_p, w_f_bf[l], w_pool_bd[l], ps, "linmix_ctx")
        a_s, d_s = _linmix_call(uf, up, DEC_SEQ, T_P // DEC_SEQ, DEC_BATCH, tabs_s, w_f_bf[l], w_pool_bd[l], ps,
                                "linmix_latent")
        b_p, c_p = _attn_p_call(l, naq, na_glob, gq, gqa_glob)
        b_s = _na_s_call(naq, nakv_s, cache_na[:, l], bias_tiles[l])
        c_s = _gqa_s_call(gq, gkv_s, cache_gqa[:, l])

        xmid, h2, te, tg, pr, cnt = _k3_call(x, (a_p, b_p, c_p, d_p), (a_s, b_s, c_s, d_s), mods[l], w_out_bf[l],
                                             g2, rw_hi[l], rw_lo[l], rb_pad[l])
        tab, ztab, first_blk, n_blk, n_used = _routing_tables(cnt)
        xs = _dispatch_call(tab, ztab, h2, pr)
        yp = _moe_call(l, first_blk, n_blk, n_used, xs, w_gu, b_gu4, w_down, b_down4)
        x = _combine_call(tab, yp, te, tg, pr, xmid, mods[l], final_g2, l == DEPTH - 1)

    y_prompt = x[:T_P].reshape(BATCH, SEQ, D_MODEL)
    y_sample = x[T_P:].reshape(DEC_BATCH, DEC_SEQ, D_MODEL)
    new_na_kv = na_glob.reshape(BATCH, DEPTH, 2, SEQ, N_HEADS, HEAD_DIM)
    new_gqa_kv = gqa_glob.reshape(BATCH, DEPTH, 2, SEQ, 2, HEAD_DIM)
    return (y_prompt, y_sample, new_na_kv, new_gqa_kv)
```

```python
import functools

import numpy as np
import jax
import jax.numpy as jnp
from jax import lax
from jax.experimental import pallas as pl
from jax.experimental.pallas import tpu as pltpu

F32 = jnp.float32
BF16 = jnp.bfloat16
I32 = jnp.int32
U32 = jnp.uint32

D_MODEL = 1024
BATCH = 32
SEQ = 256
DEPTH = 4
DEC_BATCH = 2
DEC_SEQ = 1024
PAST_LEN = 256
GRID_W = 64
GRID_ROWS = DEC_SEQ // GRID_W
HEAD_DIM = 64
GROUP_W = 256
N_HEADS = GROUP_W // HEAD_DIM
KV_W = 128
POOL_WINDOWS = (2, 4, 8, 16)
NA_KH = 8
NA_KW = 16
N_EXPERTS = 32
TOP_K = 4
D_FF = 1024
SWIGLU_LIMIT = 7.0
SWIGLU_ALPHA = 1.702
ROPE_THETA = 10000.0
EPS = 1e-6
NEG_INF = -1e30
PROJ_W = 1792

ROWS = 256
SUB = 8
LANES = 128
ROW_TILES = D_MODEL // LANES
T_P = BATCH * SEQ
T_S = DEC_BATCH * DEC_SEQ
T_ALL = T_P + T_S
NB_P = T_P // ROWS
NB_S = T_S // ROWS
NB_TOK = NB_P + NB_S
SUB_S = DEC_SEQ // ROWS
MOE_ROWS = 512
N_ASSIGN = T_ALL * TOP_K
BLOCK_ASSIGN = ROWS * TOP_K
SEG = 16
SEG_SHIFT = 4
N_BLOCKS = -(-(N_ASSIGN + N_EXPERTS * (SEG - 1 + MOE_ROWS - 1)) // MOE_ROWS)
N_SORTED = N_BLOCKS * MOE_ROWS
STAGE_ROWS = 1792
VMEM_LIMIT = 56 * 1024 * 1024


def _dft_tables(L):
    n = np.arange(L)
    ang = 2.0 * np.pi * ((n[:, None] * n[None, :]) % L) / L
    c64 = np.arange(HEAD_DIM)
    a64 = 2.0 * np.pi * ((c64[:, None] * c64[None, :]) % HEAD_DIM) / HEAD_DIM
    bc = np.kron(np.eye(N_HEADS), np.cos(a64))
    bs = np.kron(np.eye(N_HEADS), np.sin(a64))
    return (np.cos(ang).astype(np.float32), np.sin(ang).astype(np.float32),
            bc.astype(np.float32), bs.astype(np.float32))


def _pool_tables(L):
    t = np.arange(L)
    bands = np.zeros((len(POOL_WINDOWS), L, L), np.float32)
    icnt = np.zeros((L, GROUP_W), np.float64)
    for gi, win in enumerate(POOL_WINDOWS):
        a = np.clip(t - win // 2, 0, L)
        b = np.clip(t + win // 2, 0, L)
        n = t[None, :]
        bands[gi] = ((n >= a[:, None]) & (n < b[:, None])).astype(np.float32)
        icnt[:, gi * 64:(gi + 1) * 64] = (1.0 / (b - a))[:, None]
    return bands, icnt.astype(np.float32)


def _rope_tables():
    t = np.arange(DEC_SEQ)
    row = (t // GRID_W).astype(np.float64)
    col = (t % GRID_W).astype(np.float64)
    n_freq = HEAD_DIM // 4
    inv = ROPE_THETA ** (-np.arange(n_freq, dtype=np.float64) / n_freq)
    ar = row[:, None] * inv[None, :]
    ac = col[:, None] * inv[None, :]
    cos = np.concatenate([np.cos(ar), np.cos(ar), np.cos(ac), np.cos(ac)], axis=1)
    sin = np.concatenate([-np.sin(ar), np.sin(ar), -np.sin(ac), np.sin(ac)], axis=1)
    cos = np.concatenate([np.ones((ROWS, HEAD_DIM)), cos], axis=0)
    sin = np.concatenate([np.zeros((ROWS, HEAD_DIM)), sin], axis=0)
    return (np.tile(cos, (1, N_HEADS)).astype(np.float32), np.tile(sin, (1, N_HEADS)).astype(np.float32))


def _group_mean_matrix(width):
    g = np.kron(np.eye(width // HEAD_DIM), np.ones((HEAD_DIM, HEAD_DIM)))
    return g.astype(np.float32)


def _dot(a, b):
    return jnp.dot(a, b, preferred_element_type=F32)


def _dot_nt(a, b):
    return lax.dot_general(a, b, (((1,), (1,)), ((), ())), preferred_element_type=F32)


def _split_bf16(x):
    hi = x.astype(BF16)
    lo = (x - hi.astype(F32)).astype(BF16)
    return hi, lo


def _rmsnorm_rows(x, g):
    ms = jnp.mean(x * x, axis=-1, keepdims=True)
    return x * lax.rsqrt(ms + EPS) * g


def _softmax_rows(s):
    m = jnp.max(s, axis=-1, keepdims=True)
    p = jnp.exp(s - m)
    return p, jnp.sum(p, axis=-1, keepdims=True)


def _full(shape):
    return pl.BlockSpec(shape, lambda *_: (0,) * len(shape))


def _mod_row(i):
    return jnp.where(i < NB_P, 0, 1 + (i - NB_P) // SUB_S)


def _latent_block(i):
    return jnp.maximum(i - NB_P, 0)


MOD_COLS = 512


def _mods_body(cond_ref, w_ref, b_ref, o_ref):
    c = cond_ref[...]
    s = c * (1.0 / (1.0 + jnp.exp(-c)))
    s_hi, s_lo = _split_bf16(s)
    w_hi, w_lo = _split_bf16(w_ref[...])
    o_ref[...] = _dot(s_hi, w_hi) + _dot(s_lo, w_hi) + _dot(s_hi, w_lo) + b_ref[...]


def _mods_call(cond8, w_mod, b_mod3):
    n_col = 6 * D_MODEL // MOD_COLS
    return pl.pallas_call(
        _mods_body,
        grid=(DEPTH, n_col),
        in_specs=[
            pl.BlockSpec((SUB, D_MODEL), lambda l, j: (0, 0)),
            pl.BlockSpec((None, D_MODEL, MOD_COLS), lambda l, j: (l, 0, j)),
            pl.BlockSpec((None, 1, MOD_COLS), lambda l, j: (l, 0, j)),
        ],
        out_specs=pl.BlockSpec((None, SUB, MOD_COLS), lambda l, j: (l, 0, j)),
        out_shape=jax.ShapeDtypeStruct((DEPTH, SUB, 6 * D_MODEL), F32),
        name="adaln_mods",
    )(cond8, w_mod, b_mod3)


def _head_norm(t, g, gm):
    hi, lo = _split_bf16(t * t)
    msq = (_dot(hi, gm) + _dot(lo, gm)) * (1.0 / HEAD_DIM)
    return t * lax.rsqrt(msq + EPS) * g


def _rope(t, cos, sin):
    w = t.shape[1]
    lane = lax.broadcasted_iota(I32, t.shape, 1)
    first = (lane % 32) < 16
    swapped = jnp.where(first, pltpu.roll(t, w - 16, 1), pltpu.roll(t, 16, 1))
    return t * cos + swapped * sin


def _k1_body(x_ref, mod_ref, g1_ref, win_ref, qg_ref, kg_ref, gm_ref, cos_ref, sin_ref, *rest):
    uf_ref, naq_ref, gq_ref, up_ref, nakv_ref, gkv_ref, nakv_s_ref, gkv_s_ref = rest[-8:]
    i = pl.program_id(0)
    sh1 = mod_ref[:, 0:D_MODEL]
    sc1 = mod_ref[:, D_MODEL:2 * D_MODEL]
    h = _rmsnorm_rows(x_ref[...], g1_ref[...]) * (1.0 + sc1) + sh1
    hb = h.astype(BF16)

    def proj(lo, hi):
        return _dot(hb, win_ref[:, lo:hi])

    uf_ref[...] = proj(0, 256).astype(BF16)
    naq_ref[...] = proj(256, 512).astype(BF16)
    na_k = proj(512, 768)
    na_v = proj(768, 1024)
    gq = _head_norm(proj(1024, 1280), qg_ref[...], gm_ref[...])
    gk = _head_norm(proj(1280, 1408), kg_ref[:, 0:KV_W], gm_ref[0:KV_W, 0:KV_W])
    gv = proj(1408, 1536)
    up_ref[...] = proj(1536, 1792)
    gq = _rope(gq, cos_ref[...], sin_ref[...])
    gk = _rope(gk, cos_ref[:, 0:KV_W], sin_ref[:, 0:KV_W])
    gq_ref[...] = gq.astype(BF16)

    @pl.when(i < NB_P)
    def _():
        nakv_ref[0] = na_k
        nakv_ref[1] = na_v
        gkv_ref[0] = gk
        gkv_ref[1] = gv

    @pl.when(i >= NB_P)
    def _():
        nakv_s_ref[0] = na_k.astype(BF16)
        nakv_s_ref[1] = na_v.astype(BF16)
        gkv_s_ref[0] = gk.astype(BF16)
        gkv_s_ref[1] = gv.astype(BF16)


def _k1_call(layer, x, mods_l, g1, w_in_bf, qg, kg, gm, cos, sin, na_glob, gqa_glob):
    rope_blk = lambda i: (jnp.where(i < NB_P, 0, 1 + (i - NB_P) % SUB_S), 0)
    ctx_blk = lambda i: (jnp.minimum(i, NB_P - 1), layer, 0, 0, 0)
    lat_blk = lambda i: (_latent_block(i) // SUB_S, 0, _latent_block(i) % SUB_S, 0)
    row256 = pl.BlockSpec((ROWS, GROUP_W), lambda i: (i, 0))
    in_specs = [
        pl.BlockSpec((ROWS, D_MODEL), lambda i: (i, 0)),
        pl.BlockSpec((None, 1, 6 * D_MODEL), lambda i: (_mod_row(i), 0, 0)),
        _full((1, D_MODEL)),
        _full((D_MODEL, PROJ_W)),
        _full((1, GROUP_W)),
        _full((1, GROUP_W)),
        _full((GROUP_W, GROUP_W)),
        pl.BlockSpec((ROWS, GROUP_W), rope_blk),
        pl.BlockSpec((ROWS, GROUP_W), rope_blk),
    ]
    in_specs += [pl.BlockSpec(memory_space=pl.ANY)] * 2
    args = [x, mods_l, g1, w_in_bf, qg, kg, gm, cos, sin, na_glob, gqa_glob]
    return pl.pallas_call(
        _k1_body,
        grid=(NB_TOK,),
        in_specs=in_specs,
        out_specs=[
            row256, row256, row256, row256,
            pl.BlockSpec((None, None, 2, ROWS, GROUP_W), ctx_blk),
            pl.BlockSpec((None, None, 2, ROWS, KV_W), ctx_blk),
            pl.BlockSpec((None, 2, ROWS, GROUP_W), lat_blk),
            pl.BlockSpec((None, 2, ROWS, KV_W), lat_blk),
        ],
        out_shape=[
            jax.ShapeDtypeStruct((T_ALL, GROUP_W), BF16),
            jax.ShapeDtypeStruct((T_ALL, GROUP_W), BF16),
            jax.ShapeDtypeStruct((T_ALL, GROUP_W), BF16),
            jax.ShapeDtypeStruct((T_ALL, GROUP_W), F32),
            jax.ShapeDtypeStruct((BATCH, DEPTH, 2, SEQ, GROUP_W), F32),
            jax.ShapeDtypeStruct((BATCH, DEPTH, 2, SEQ, KV_W), F32),
            jax.ShapeDtypeStruct((DEC_BATCH, 2, DEC_SEQ, GROUP_W), BF16),
            jax.ShapeDtypeStruct((DEC_BATCH, 2, DEC_SEQ, KV_W), BF16),
        ],
        input_output_aliases={9: 4, 10: 5},
        compiler_params=pltpu.CompilerParams(vmem_limit_bytes=VMEM_LIMIT),
        name="k1_norm_proj",
    )(*args)


def _linmix_body(seq, uf_ref, up_ref, bc_ref, bs_ref, cl_ref, sl_ref, wf_ref, band_ref, icnt_ref,
                 wp_ref, ps_ref, a_ref, d_ref):
    u = uf_ref[...]
    t1 = _dot(u, bc_ref[...]).astype(BF16)
    t2 = _dot(u, bs_ref[...]).astype(BF16)
    f = (_dot(cl_ref[...], t1) - _dot(sl_ref[...], t2)) * (1.0 / float(np.sqrt(seq * HEAD_DIM)))
    a_ref[...] = _dot(f.astype(BF16), wf_ref[...]).astype(BF16)

    up = up_ref[...]
    up_hi, up_lo = _split_bf16(up)
    group = lax.broadcasted_iota(I32, up.shape, 1) // HEAD_DIM
    win_sum = jnp.zeros_like(up)
    for gi in range(len(POOL_WINDOWS)):
        s = _dot(band_ref[gi], up_hi) + _dot(band_ref[gi], up_lo)
        win_sum = jnp.where(group == gi, s, win_sum)
    delta = win_sum * icnt_ref[...] - up
    d_ref[...] = (_dot(delta.astype(BF16), wp_ref[...]) * ps_ref[...]).astype(BF16)


def _linmix_call(uf, up, seq, first_blk, n_blk, tabs, wf_bf, wp_bd_bf, ps, name):
    bc, bs, cl, sl, bands, icnt = tabs
    blk = pl.BlockSpec((seq, GROUP_W), lambda i: (first_blk + i, 0))
    oblk = pl.BlockSpec((seq, GROUP_W), lambda i: (i, 0))
    return pl.pallas_call(
        functools.partial(_linmix_body, seq),
        grid=(n_blk,),
        in_specs=[
            blk, blk,
            _full((GROUP_W, GROUP_W)), _full((GROUP_W, GROUP_W)),
            _full((seq, seq)), _full((seq, seq)),
            _full((GROUP_W, GROUP_W)),
            _full((len(POOL_WINDOWS), seq, seq)),
            _full((seq, GROUP_W)),
            _full((GROUP_W, GROUP_W)),
            _full((1, GROUP_W)),
        ],
        out_specs=[oblk, oblk],
        out_shape=[jax.ShapeDtypeStruct((n_blk * seq, GROUP_W), BF16)] * 2,
        compiler_params=pltpu.CompilerParams(vmem_limit_bytes=VMEM_LIMIT),
        name=name,
    )(uf, up, bc, bs, cl, sl, wf_bf, bands, icnt, wp_bd_bf, ps)


def _attend(q, k, v):
    s = _dot_nt(q, k) * (HEAD_DIM ** -0.5)
    p, l = _softmax_rows(s)
    return _dot(p.astype(BF16), v) / l


def _attn_p_body(naq_ref, nakv_ref, gq_ref, gkv_ref, b_ref, c_ref):
    for h in range(N_HEADS):
        sl = slice(h * HEAD_DIM, (h + 1) * HEAD_DIM)
        k = nakv_ref[0, :, sl].astype(BF16)
        v = nakv_ref[1, :, sl].astype(BF16)
        b_ref[:, sl] = _attend(naq_ref[:, sl], k, v).astype(BF16)
    for h in range(N_HEADS):
        sl = slice(h * HEAD_DIM, (h + 1) * HEAD_DIM)
        kvh = h // 2
        ksl = slice(kvh * HEAD_DIM, (kvh + 1) * HEAD_DIM)
        k = gkv_ref[0, :, ksl].astype(BF16)
        v = gkv_ref[1, :, ksl].astype(BF16)
        c_ref[:, sl] = _attend(gq_ref[:, sl], k, v).astype(BF16)


def _attn_p_call(layer, naq, na_glob, gq, gqa_glob):
    row = pl.BlockSpec((SEQ, GROUP_W), lambda i: (i, 0))
    return pl.pallas_call(
        _attn_p_body,
        grid=(BATCH,),
        in_specs=[
            row,
            pl.BlockSpec((None, None, 2, SEQ, GROUP_W), lambda i: (i, layer, 0, 0, 0)),
            row,
            pl.BlockSpec((None, None, 2, SEQ, KV_W), lambda i: (i, layer, 0, 0, 0)),
        ],
        out_specs=[row, row],
        out_shape=[jax.ShapeDtypeStruct((T_P, GROUP_W), BF16)] * 2,
        name="attn_ctx",
    )(naq, na_glob, gq, gqa_glob)


N_DR = 2 * NA_KH - 1
N_DC = 2 * NA_KW - 1


def _nabias_body(rpb_ref, o_ref):
    lh = pl.program_id(0)
    q = lax.broadcasted_iota(I32, (GRID_W, LANES), 0)
    j = lax.broadcasted_iota(I32, (GRID_W, LANES), 1)
    k = j % GRID_W
    second = j >= GRID_W
    c0 = jnp.clip(q - NA_KW // 2, 0, GRID_W - NA_KW)
    valid = (k >= c0) & (k < c0 + NA_KW)
    diff = k - q + (NA_KW - 1)
    for dr in range(N_DR - 1):
        acc = jnp.zeros((GRID_W, LANES), F32)
        for dc in range(N_DC):
            val = jnp.where(second, rpb_ref[lh, (dr + 1) * N_DC + dc], rpb_ref[lh, dr * N_DC + dc])
            acc = jnp.where(diff == dc, val, acc)
        o_ref[dr] = jnp.where(valid, acc, NEG_INF)


def _nabias_call(rpb_flat):
    n = rpb_flat.shape[0]
    return pl.pallas_call(
        _nabias_body,
        grid=(n,),
        in_specs=[pl.BlockSpec(memory_space=pltpu.SMEM)],
        out_specs=pl.BlockSpec((None, N_DR - 1, GRID_W, LANES), lambda i: (i, 0, 0, 0)),
        out_shape=jax.ShapeDtypeStruct((n, N_DR - 1, GRID_W, LANES), F32),
        name="na_bias_tiles",
    )(rpb_flat)


N_WIN = NA_KH * GRID_W


def _na_s_body(q_ref, kv_ref, ckv_ref, bias_ref, o_ref):
    r = pl.program_id(1)
    r0 = jnp.clip(r - NA_KH // 2, 0, GRID_ROWS - NA_KH)
    start = pl.multiple_of(r0 * GRID_W, GRID_W)
    dr0 = r0 - r + (NA_KH - 1)
    scale = HEAD_DIM ** -0.5
    for h in range(N_HEADS):
        sl = slice(h * HEAD_DIM, (h + 1) * HEAD_DIM)
        q = q_ref[:, sl]
        kw = kv_ref[0, pl.ds(start, N_WIN), sl]
        vw = kv_ref[1, pl.ds(start, N_WIN), sl]
        kc = ckv_ref[0, :, sl].astype(BF16)
        vc = ckv_ref[1, :, sl].astype(BF16)
        bias = jnp.concatenate([bias_ref[h, dr0 + i] for i in range(0, NA_KH, 2)], axis=1)
        s_win = _dot_nt(q, kw) * scale + bias
        s_ctx = _dot_nt(q, kc) * scale
        s = jnp.concatenate([s_win, s_ctx], axis=1)
        p, l = _softmax_rows(s)
        pb = p.astype(BF16)
        o = _dot(pb[:, 0:N_WIN], vw) + _dot(pb[:, N_WIN:], vc)
        o_ref[:, sl] = (o / l).astype(BF16)


def _na_s_call(naq, nakv_s, cache_na_l, bias_l):
    first = T_P // GRID_W
    return pl.pallas_call(
        _na_s_body,
        grid=(DEC_BATCH, GRID_ROWS),
        in_specs=[
            pl.BlockSpec((GRID_W, GROUP_W), lambda b, r: (first + b * GRID_ROWS + r, 0)),
            pl.BlockSpec((None, 2, DEC_SEQ, GROUP_W), lambda b, r: (b, 0, 0, 0)),
            pl.BlockSpec((None, 2, PAST_LEN, GROUP_W), lambda b, r: (b, 0, 0, 0)),
            _full((N_HEADS, N_DR - 1, GRID_W, LANES)),
        ],
        out_specs=pl.BlockSpec((GRID_W, GROUP_W), lambda b, r: (b * GRID_ROWS + r, 0)),
        out_shape=jax.ShapeDtypeStruct((T_S, GROUP_W), BF16),
        compiler_params=pltpu.CompilerParams(vmem_limit_bytes=VMEM_LIMIT),
        name="attn_na_latent",
    )(naq, nakv_s, cache_na_l, bias_l)


GQ_ROWS = 128


def _gqa_s_body(q_ref, kv_ref, ckv_ref, o_ref):
    scale = HEAD_DIM ** -0.5
    for h in range(N_HEADS):
        sl = slice(h * HEAD_DIM, (h + 1) * HEAD_DIM)
        kvh = h // 2
        ksl = slice(kvh * HEAD_DIM, (kvh + 1) * HEAD_DIM)
        q = q_ref[:, sl]
        kl = kv_ref[0, :, ksl]
        vl = kv_ref[1, :, ksl]
        kc = ckv_ref[0, :, ksl].astype(BF16)
        vc = ckv_ref[1, :, ksl].astype(BF16)
        s = jnp.concatenate([_dot_nt(q, kl), _dot_nt(q, kc)], axis=1) * scale
        p, l = _softmax_rows(s)
        pb = p.astype(BF16)
        o = _dot(pb[:, 0:DEC_SEQ], vl) + _dot(pb[:, DEC_SEQ:], vc)
        o_ref[:, sl] = (o / l).astype(BF16)


def _gqa_s_call(gq, gkv_s, cache_gqa_l):
    nq = DEC_SEQ // GQ_ROWS
    first = T_P // GQ_ROWS
    return pl.pallas_call(
        _gqa_s_body,
        grid=(DEC_BATCH, nq),
        in_specs=[
            pl.BlockSpec((GQ_ROWS, GROUP_W), lambda b, j: (first + b * nq + j, 0)),
            pl.BlockSpec((None, 2, DEC_SEQ, KV_W), lambda b, j: (b, 0, 0, 0)),
            pl.BlockSpec((None, 2, PAST_LEN, KV_W), lambda b, j: (b, 0, 0, 0)),
        ],
        out_specs=pl.BlockSpec((GQ_ROWS, GROUP_W), lambda b, j: (b * nq + j, 0)),
        out_shape=jax.ShapeDtypeStruct((T_S, GROUP_W), BF16),
        compiler_params=pltpu.CompilerParams(vmem_limit_bytes=VMEM_LIMIT),
        name="attn_gqa_latent",
    )(gq, gkv_s, cache_gqa_l)


def _k3_body(x_ref, ap_ref, bp_ref, cp_ref, dp_ref, as_ref, bs_ref, cs_ref, ds_ref, mod_ref, wout_ref, g2_ref,
             rwh_ref, rwl_ref, rb_ref, xmid_ref, h2_ref, te_ref, tg_ref, pr_ref, cnt_ref):
    is_ctx = pl.program_id(0) < NB_P
    acc = jnp.zeros((ROWS, D_MODEL), F32)
    for g, (p_ref, s_ref) in enumerate(((ap_ref, as_ref), (bp_ref, bs_ref), (cp_ref, cs_ref), (dp_ref, ds_ref))):
        mixed = jnp.where(is_ctx, p_ref[...], s_ref[...])
        acc += _dot(mixed, wout_ref[g * GROUP_W:(g + 1) * GROUP_W, :])
    gate1 = mod_ref[:, 2 * D_MODEL:3 * D_MODEL]
    sh2 = mod_ref[:, 3 * D_MODEL:4 * D_MODEL]
    sc2 = mod_ref[:, 4 * D_MODEL:5 * D_MODEL]
    xm = x_ref[...] + gate1 * acc
    xmid_ref[...] = xm
    h2 = _rmsnorm_rows(xm, g2_ref[...]) * (1.0 + sc2) + sh2
    hi, lo = _split_bf16(h2)
    h2_ref[...] = hi
    logits = _dot(hi, rwh_ref[...]) + _dot(lo, rwh_ref[...]) + _dot(hi, rwl_ref[...]) + rb_ref[...]

    lane = lax.broadcasted_iota(I32, logits.shape, 1)
    cur = logits
    vals, idxs = [], []
    for _ in range(TOP_K):
        m = jnp.max(cur, axis=-1, keepdims=True)
        idx = jnp.min(jnp.where(cur == m, lane, LANES), axis=-1, keepdims=True)
        vals.append(m)
        idxs.append(idx)
        cur = jnp.where(lane == idx, -jnp.inf, cur)
    exps = [jnp.exp(v - vals[0]) for v in vals]
    tot = exps[0] + exps[1] + exps[2] + exps[3]
    te = jnp.zeros(logits.shape, I32)
    tg = jnp.zeros(logits.shape, F32)
    for k in range(TOP_K):
        te = jnp.where(lane == k, idxs[k], te)
        tg = jnp.where(lane == k, exps[k] / tot, tg)
    te_ref[...] = te
    tg_ref[...] = tg

    member = jnp.zeros(logits.shape, F32)
    for k in range(TOP_K):
        member = jnp.where(lane == idxs[k], 1.0, member)
    rr = lax.broadcasted_iota(I32, (ROWS, ROWS), 0)
    cc = lax.broadcasted_iota(I32, (ROWS, ROWS), 1)
    earlier = jnp.where(cc < rr, 1.0, 0.0).astype(BF16)
    prefix = _dot(earlier, member.astype(BF16))
    cnt = jnp.sum(member, axis=0, keepdims=True)
    er = lax.broadcasted_iota(I32, (LANES, LANES), 0)
    ec = lax.broadcasted_iota(I32, (LANES, LANES), 1)
    lower = jnp.where(er < ec, 1.0, 0.0).astype(BF16)
    loc = _dot(jnp.broadcast_to(cnt, (SUB, LANES)).astype(BF16), lower)[0:1, :]
    pr = jnp.zeros(logits.shape, F32)
    for k in range(TOP_K):
        hit = lane == idxs[k]
        rank = jnp.sum(jnp.where(hit, prefix, 0.0), axis=-1, keepdims=True)
        start = jnp.sum(jnp.where(hit, loc, 0.0), axis=-1, keepdims=True)
        pr = jnp.where(lane == k, start + rank, pr)
        pr = jnp.where(lane == TOP_K + k, rank, pr)
    pr_ref[...] = pr.astype(I32)
    cnt_ref[...] = cnt.astype(I32)


def _k3_call(x, mix_p, mix_s, mods_l, w_out_bf, g2, rwh, rwl, rb):
    row_p = pl.BlockSpec((ROWS, GROUP_W), lambda i: (jnp.minimum(i, NB_P - 1), 0))
    row_s = pl.BlockSpec((ROWS, GROUP_W), lambda i: (_latent_block(i), 0))
    rowd = pl.BlockSpec((ROWS, D_MODEL), lambda i: (i, 0))
    row128 = pl.BlockSpec((ROWS, LANES), lambda i: (i, 0))
    return pl.pallas_call(
        _k3_body,
        grid=(NB_TOK,),
        in_specs=[
            rowd, row_p, row_p, row_p, row_p, row_s, row_s, row_s, row_s,
            pl.BlockSpec((None, 1, 6 * D_MODEL), lambda i: (_mod_row(i), 0, 0)),
            _full((D_MODEL, D_MODEL)),
            _full((1, D_MODEL)),
            _full((D_MODEL, LANES)), _full((D_MODEL, LANES)), _full((1, LANES)),
        ],
        out_specs=[rowd, rowd, row128, row128, row128,
                   pl.BlockSpec((None, 1, LANES), lambda i: (i, 0, 0))],
        out_shape=[
            jax.ShapeDtypeStruct((T_ALL, D_MODEL), F32),
            jax.ShapeDtypeStruct((T_ALL, D_MODEL), BF16),
            jax.ShapeDtypeStruct((T_ALL, LANES), I32),
            jax.ShapeDtypeStruct((T_ALL, LANES), F32),
            jax.ShapeDtypeStruct((T_ALL, LANES), I32),
            jax.ShapeDtypeStruct((NB_TOK, 1, LANES), I32),
        ],
        compiler_params=pltpu.CompilerParams(vmem_limit_bytes=VMEM_LIMIT),
        name="k3_out_router",
    )(x, *mix_p, *mix_s, mods_l, w_out_bf, g2, rwh, rwl, rb)


TAB_CNT, TAB_LOC, TAB_BASE, TAB_LSTAGE, TAB_SOFF, TAB_TOTAL, TAB_NCHUNK = range(7)


def _routing_tables(cnt):
    c = cnt.reshape(NB_TOK, LANES)[:, :N_EXPERTS]
    counts = jnp.sum(c, axis=0)
    padded = (counts + (SEG - 1) + (MOE_ROWS - 1)) // MOE_ROWS * MOE_ROWS
    pend = jnp.cumsum(padded)
    pstart = pend - padded
    base = pstart[None, :] + jnp.cumsum(c, axis=0) - c
    loc = jnp.cumsum(c, axis=1) - c
    off = base % SUB
    nch = jnp.where(c > 0, (off + c + SEG - 1) // SEG, 0)
    lstage = (jnp.cumsum(nch, axis=1) - nch) * SEG
    soff = lstage + off
    total = jnp.sum(nch, axis=1, keepdims=True) * SEG

    def pad(t):
        return jnp.zeros((NB_TOK, LANES), I32).at[:, :t.shape[1]].set(t.astype(I32))

    tab = jnp.stack([pad(c), pad(loc), pad(base), pad(lstage), pad(soff), pad(total), pad(nch),
                     jnp.zeros((NB_TOK, LANES), I32)], axis=1)
    ztab = jnp.zeros((SUB, LANES), I32).at[0, :N_EXPERTS].set((pstart + counts).astype(I32))
    ztab = ztab.at[1, :N_EXPERTS].set(pend.astype(I32))
    ztab = ztab.at[2, 0].set((pend[-1] // MOE_ROWS).astype(I32))
    first_blk = (pstart // MOE_ROWS).astype(I32)
    n_blk = (padded // MOE_ROWS).astype(I32)
    n_used = (pend[-1] // MOE_ROWS).astype(I32).reshape(1)
    return tab, ztab, first_blk, n_blk, counts.astype(I32), n_used


def _dispatch_body(tab_ref, ztab_ref, h2_ref, pr_ref, xs_ref, stage0, stage1, zbuf, pending, sems):
    b = pl.program_id(0)
    sem = sems.at[0]

    def rows_copy(src, src_row, dst_row, n_rows, sem=sem):
        return pltpu.make_async_copy(
            src.at[pl.ds(pl.multiple_of(src_row * SUB, SUB), n_rows * SUB)],
            xs_ref.at[pl.ds(pl.multiple_of(dst_row * SUB, SUB), n_rows * SUB)], sem)

    def seg_copy(src, src_row, dst_row, sem=sem):
        return rows_copy(src, src_row, dst_row, SEG, sem)

    def drain(n, sem=sem):
        def body(j, carry):
            seg_copy(zbuf, 0, 0, sem).wait()
            return carry
        lax.fori_loop(0, n, body, 0)

    @pl.when(b == 0)
    def _():
        zbuf[...] = jnp.zeros(zbuf.shape, F32)
        stage0[pl.ds(BLOCK_ASSIGN * SUB, SEG * SUB), :] = jnp.zeros((SEG * SUB, LANES), F32)
        stage1[pl.ds(BLOCK_ASSIGN * SUB, SEG * SUB), :] = jnp.zeros((SEG * SUB, LANES), F32)

        def per_expert(e, carry):
            n_seg, n_row = carry
            end = ztab_ref[0, e]
            length = ztab_ref[1, e] - end
            n = lax.shift_right_logical(length, SEG_SHIFT)
            rem = length & (SEG - 1)

            def one(j, c):
                seg_copy(zbuf, 0, end + j * SEG).start()
                return c
            lax.fori_loop(0, n, one, 0)

            def one_row(j, c):
                rows_copy(zbuf, 0, end + n * SEG + j, 1).start()
                return c
            lax.fori_loop(0, rem, one_row, 0)
            return n_seg + n, n_row + rem
        n_seg, n_row = lax.fori_loop(0, N_EXPERTS, per_expert, (0, 0))
        drain(n_seg)

        def row_wait(j, c):
            rows_copy(zbuf, 0, 0, 1).wait()
            return c
        lax.fori_loop(0, n_row, row_wait, 0)

        first_free = ztab_ref[2, 0]

        def fill(j, c):
            rows_copy(zbuf, 0, (first_free + j) * MOE_ROWS, MOE_ROWS).start()
            return c
        lax.fori_loop(0, N_BLOCKS - first_free, fill, 0)

        def fill_wait(j, c):
            rows_copy(zbuf, 0, 0, MOE_ROWS).wait()
            return c
        lax.fori_loop(0, N_BLOCKS - first_free, fill_wait, 0)

    pos_t = pr_ref[...].astype(F32).T
    h2 = h2_ref[...]

    def sort(stage):
        for c in range(BLOCK_ASSIGN // ROWS):
            p = (lax.broadcasted_iota(I32, (ROWS, ROWS), 0) + c * ROWS).astype(F32)
            sel = p == pos_t[0:1, :]
            for k in range(1, TOP_K):
                sel = sel | (p == pos_t[k:k + 1, :])
            rows = _dot(jnp.where(sel, 1.0, 0.0).astype(BF16), h2)
            for s in range(ROW_TILES):
                stage[pl.ds(c * ROWS * SUB + s, ROWS, stride=SUB), :] = rows[:, s * LANES:(s + 1) * LANES]

    def send(stage, sem):
        def per_expert(e, tot):
            loc = tab_ref[TAB_LOC, e]
            base = tab_ref[TAB_BASE, e]
            n = lax.shift_right_logical(tab_ref[TAB_CNT, e] + (SEG - 1), SEG_SHIFT)

            def one(j, carry):
                seg_copy(stage, loc + j * SEG, base + j * SEG, sem).start()
                return carry
            lax.fori_loop(0, n, one, 0)
            return tot + n
        return lax.fori_loop(0, N_EXPERTS, per_expert, 0)

    for slot, stage in enumerate((stage0, stage1)):
        @pl.when((b & 1) == slot)
        def _(stage=stage, slot=slot):
            sort(stage)

            @pl.when(b > 0)
            def _():
                drain(pending[0], sems.at[1 - slot])
            n_sent = send(stage, sems.at[slot])
            pending[0] = n_sent

            @pl.when(b == NB_TOK - 1)
            def _():
                drain(n_sent, sems.at[slot])


def _dispatch_call(tab, ztab, h2, pr):
    return pl.pallas_call(
        _dispatch_body,
        grid=(NB_TOK,),
        in_specs=[
            pl.BlockSpec((None, SUB, LANES), lambda i: (i, 0, 0), memory_space=pltpu.SMEM),
            pl.BlockSpec(memory_space=pltpu.SMEM),
            pl.BlockSpec((ROWS, D_MODEL), lambda i: (i, 0)),
            pl.BlockSpec((ROWS, LANES), lambda i: (i, 0)),
        ],
        out_specs=pl.BlockSpec(memory_space=pl.ANY),
        out_shape=jax.ShapeDtypeStruct((N_SORTED * ROW_TILES, LANES), F32),
        scratch_shapes=[
            pltpu.VMEM(((BLOCK_ASSIGN + SEG) * SUB, LANES), F32),
            pltpu.VMEM(((BLOCK_ASSIGN + SEG) * SUB, LANES), F32),
            pltpu.VMEM((MOE_ROWS * SUB, LANES), F32),
            pltpu.SMEM((1,), I32),
            pltpu.SemaphoreType.DMA((2,)),
        ],
        compiler_params=pltpu.CompilerParams(vmem_limit_bytes=VMEM_LIMIT),
        name="moe_dispatch",
    )(tab, ztab, h2, pr)


HALF_FF = D_FF // 2


def _moe_body(fb_ref, nb_ref, nv_ref, nu_ref, xs_ref, wgu_ref, bgu_ref, wd_ref, bd_ref, yp_ref,
              wgu_bf, wd_bf, wd_scr, xb_scr, act_scr, xbuf0, xbuf1, ybuf0, ybuf1, xsems, ysems):
    e = pl.program_id(0)
    first = fb_ref[e]
    n_blk = nb_ref[e]
    n_used = nu_ref[0]
    xbufs = (xbuf0, xbuf1)
    ybufs = (ybuf0, ybuf1)

    def x_copy(g, slot):
        src = xs_ref.at[pl.ds(pl.multiple_of(g * (MOE_ROWS * SUB), MOE_ROWS * SUB), MOE_ROWS * SUB)]
        return pltpu.make_async_copy(src, xbufs[slot], xsems.at[slot])

    def y_copy(g, slot):
        dst = yp_ref.at[pl.ds(pl.multiple_of(g * MOE_ROWS, MOE_ROWS), MOE_ROWS)]
        return pltpu.make_async_copy(ybufs[slot], dst, ysems.at[slot])

    @pl.when(e == 0)
    def _():
        x_copy(0, 0).start()

    @pl.when(n_blk > 0)
    def _():
        for c in range(4):
            cs = slice(c * 512, (c + 1) * 512)
            wgu_bf[:, cs] = wgu_ref[:, cs].astype(BF16)
        for c in range(ROW_TILES):
            cs = slice(c * LANES, (c + 1) * LANES)
            wd_scr[pl.ds(0, HALF_FF, stride=2), :] = wd_ref[0:HALF_FF, cs]
            wd_scr[pl.ds(1, HALF_FF, stride=2), :] = wd_ref[HALF_FF:D_FF, cs]
            wd_bf[:, cs] = wd_scr[...].astype(BF16)

    def ffn(x_ref, y_ref, m):
        for s in range(ROW_TILES):
            xb_scr[0:m, s * LANES:(s + 1) * LANES] = x_ref[pl.ds(s, m, stride=ROW_TILES), :].astype(BF16)
        even = (lax.broadcasted_iota(I32, (m, LANES), 1) % 2) == 0
        for c in range(D_FF // GROUP_W):
            ca = slice(c * GROUP_W, (c + 1) * GROUP_W)
            cb = slice(D_FF + c * GROUP_W, D_FF + (c + 1) * GROUP_W)
            ha_full = _dot(xb_scr[0:m, :], wgu_bf[:, ca]) + bgu_ref[:, ca]
            hb_full = _dot(xb_scr[0:m, :], wgu_bf[:, cb]) + bgu_ref[:, cb]
            for h in range(GROUP_W // LANES):
                ha = ha_full[:, h * LANES:(h + 1) * LANES]
                hb = hb_full[:, h * LANES:(h + 1) * LANES]
                gate = jnp.where(even, ha, pltpu.roll(hb, 1, 1))
                up = jnp.where(even, pltpu.roll(ha, LANES - 1, 1), hb)
                gate = jnp.minimum(gate, SWIGLU_LIMIT)
                up = jnp.clip(up, -SWIGLU_LIMIT, SWIGLU_LIMIT)
                glu = gate * (1.0 / (1.0 + jnp.exp(-gate * SWIGLU_ALPHA)))
                lo = c * GROUP_W + h * LANES
                act_scr[0:m, lo:lo + LANES] = ((up + 1.0) * glu).astype(BF16)
        for c in range(D_MODEL // GROUP_W):
            cs = slice(c * GROUP_W, (c + 1) * GROUP_W)
            y_ref[0:m, cs] = _dot(act_scr[0:m, :], wd_bf[:, cs]) + bd_ref[:, cs]
        if m < MOE_ROWS:
            y_ref[m:MOE_ROWS, :] = jnp.zeros((MOE_ROWS - m, D_MODEL), F32)

    def block(j, carry):
        g = first + j
        for slot in range(2):
            @pl.when((g & 1) == slot)
            def _(slot=slot):
                x_copy(g, slot).wait()

                @pl.when(g + 1 < n_used)
                def _():
                    x_copy(g + 1, 1 - slot).start()
                @pl.when(g >= 2)
                def _():
                    y_copy(g - 2, slot).wait()
                valid = nv_ref[e] - j * MOE_ROWS

                @pl.when(valid > MOE_ROWS // 2)
                def _():
                    ffn(xbufs[slot], ybufs[slot], MOE_ROWS)

                @pl.when(valid <= MOE_ROWS // 2)
                def _():
                    ffn(xbufs[slot], ybufs[slot], MOE_ROWS // 2)
                y_copy(g, slot).start()
        return carry
    lax.fori_loop(0, n_blk, block, 0)

    @pl.when(e == N_EXPERTS - 1)
    def _():
        for slot in range(2):
            @pl.when(((n_used - 1) & 1) == slot)
            def _(slot=slot):
                @pl.when(n_used >= 2)
                def _():
                    y_copy(n_used - 2, 1 - slot).wait()
                y_copy(n_used - 1, slot).wait()
        ybuf0[...] = jnp.zeros(ybuf0.shape, F32)

        def fill(j, carry):
            y_copy(n_used + j, 0).start()
            return carry
        lax.fori_loop(0, N_BLOCKS - n_used, fill, 0)

        def fill_wait(j, carry):
            y_copy(0, 0).wait()
            return carry
        lax.fori_loop(0, N_BLOCKS - n_used, fill_wait, 0)


def _moe_call(layer, first_blk, n_blk, n_valid, n_used, xs, w_gu, b_gu4, w_down, b_down4):
    wmap = lambda e, fb, nb, nv, nu: (layer, e, 0, 0)
    grid_spec = pltpu.PrefetchScalarGridSpec(
        num_scalar_prefetch=4,
        grid=(N_EXPERTS,),
        in_specs=[
            pl.BlockSpec(memory_space=pl.ANY),
            pl.BlockSpec((None, None, D_MODEL, 2 * D_FF), wmap),
            pl.BlockSpec((None, None, 1, 2 * D_FF), wmap),
            pl.BlockSpec((None, None, D_FF, D_MODEL), wmap),
            pl.BlockSpec((None, None, 1, D_MODEL), wmap),
        ],
        out_specs=pl.BlockSpec(memory_space=pl.ANY),
        scratch_shapes=[
            pltpu.VMEM((D_MODEL, 2 * D_FF), BF16),
            pltpu.VMEM((D_FF, D_MODEL), BF16),
            pltpu.VMEM((D_FF, LANES), F32),
            pltpu.VMEM((MOE_ROWS, D_MODEL), BF16),
            pltpu.VMEM((MOE_ROWS, D_FF), BF16),
            pltpu.VMEM((MOE_ROWS * SUB, LANES), F32),
            pltpu.VMEM((MOE_ROWS * SUB, LANES), F32),
            pltpu.VMEM((MOE_ROWS, D_MODEL), F32),
            pltpu.VMEM((MOE_ROWS, D_MODEL), F32),
            pltpu.SemaphoreType.DMA((2,)),
            pltpu.SemaphoreType.DMA((2,)),
        ],
    )
    return pl.pallas_call(
        _moe_body,
        grid_spec=grid_spec,
        out_shape=jax.ShapeDtypeStruct((N_SORTED, D_MODEL), F32),
        compiler_params=pltpu.CompilerParams(vmem_limit_bytes=VMEM_LIMIT),
        name="moe_experts",
    )(first_blk, n_blk, n_valid, n_used, xs, w_gu, b_gu4, w_down, b_down4)


def _combine_body(final, tab_ref, tabn_ref, tabv_ref, yp_ref, te_ref, tg_ref, pr_ref, xmid_ref, mod_ref, fg_ref,
                  *rest):
    outs = rest[:-4]
    stage0, stage1, acc, sems = rest[-4:]
    b = pl.program_id(0)
    stages = (stage0, stage1)

    def chunk_copy(src_row, dst_row, slot):
        return pltpu.make_async_copy(
            yp_ref.at[pl.ds(pl.multiple_of(src_row, SUB), SEG)],
            stages[slot].at[pl.ds(pl.multiple_of(dst_row, SUB), SEG)], sems.at[slot])

    def fetch(tab, slot):
        def per_expert(e, carry):
            base = tab[TAB_BASE, e]
            off = base & (SUB - 1)
            lstage = tab[TAB_LSTAGE, e]

            def one(j, c):
                chunk_copy(base - off + j * SEG, lstage + j * SEG, slot).start()
                return c
            lax.fori_loop(0, tab[TAB_NCHUNK, e], one, 0)
            return carry
        lax.fori_loop(0, N_EXPERTS, per_expert, 0)

        total = tab[TAB_TOTAL, 0]

        def zero_tail(j, c):
            stages[slot][pl.ds(pl.multiple_of(total + j * SEG, SEG), SEG), :] = jnp.zeros((SEG, D_MODEL), F32)
            return c
        lax.fori_loop(0, lax.shift_right_logical(STAGE_ROWS - total, SEG_SHIFT), zero_tail, 0)

    @pl.when(b == 0)
    def _():
        fetch(tab_ref, 0)

    for slot in range(2):
        @pl.when(((b & 1) == slot) & (b + 1 < NB_TOK))
        def _(slot=slot):
            fetch(tabn_ref, 1 - slot)

    total = tab_ref[TAB_TOTAL, 0]
    for slot in range(2):
        @pl.when((b & 1) == slot)
        def _(slot=slot):
            def drain(j, c):
                chunk_copy(0, 0, slot).wait()
                return c
            lax.fori_loop(0, lax.shift_right_logical(total, SEG_SHIFT), drain, 0)

    lane = lax.broadcasted_iota(I32, (ROWS, LANES), 1)
    te = te_ref[...]
    pr = pr_ref[...]
    soff = tabv_ref[TAB_SOFF:TAB_SOFF + 1, :].astype(F32)
    packed = jnp.where((lane >= TOP_K) & (lane < 2 * TOP_K), pltpu.roll(tg_ref[...], TOP_K, 1), 0.0)
    sp = []
    for k in range(TOP_K):
        seg_start = jnp.sum(jnp.where(lane == te[:, k:k + 1], soff, 0.0), axis=-1, keepdims=True)
        spk = seg_start + pr[:, TOP_K + k:TOP_K + k + 1].astype(F32)
        sp.append(spk)
        packed = jnp.where(lane == k, spk, packed)
    packed_t = packed.T

    def unsort(stage, c):
        pcol = (lax.broadcasted_iota(I32, (ROWS, ROWS), 1) + c * ROWS).astype(F32)
        sel = pcol == sp[0]
        for k in range(1, TOP_K):
            sel = sel | (pcol == sp[k])
        prow = (lax.broadcasted_iota(I32, (ROWS, ROWS), 0) + c * ROWS).astype(F32)
        gmat = jnp.zeros((ROWS, ROWS), F32)
        for k in range(TOP_K):
            gmat = jnp.where(prow == packed_t[k:k + 1, :], packed_t[TOP_K + k:TOP_K + k + 1, :], gmat)
        row_gate = jnp.sum(gmat, axis=-1, keepdims=True)
        rows = (stage[c * ROWS:(c + 1) * ROWS, :] * row_gate).astype(BF16)
        return _dot(jnp.where(sel, 1.0, 0.0).astype(BF16), rows)

    n_always = BLOCK_ASSIGN // ROWS
    for slot in range(2):
        @pl.when((b & 1) == slot)
        def _(slot=slot):
            out = unsort(stages[slot], 0)
            for c in range(1, n_always):
                out += unsort(stages[slot], c)
            acc[...] = out
            for c in range(n_always, STAGE_ROWS // ROWS):
                @pl.when(c * ROWS < total)
                def _(c=c):
                    acc[...] += unsort(stages[slot], c)

    gate2 = mod_ref[:, 5 * D_MODEL:6 * D_MODEL]
    x = xmid_ref[...] + gate2 * acc[...]
    if final:
        x = _rmsnorm_rows(x, fg_ref[...])
        out_ctx, out_lat = outs

        @pl.when(b < NB_P)
        def _():
            out_ctx[...] = x

        @pl.when(b >= NB_P)
        def _():
            out_lat[...] = x
    else:
        outs[0][...] = x


def _combine_call(tab, yp, te, tg, pr, xmid, mods_l, final_g, final):
    rowd = pl.BlockSpec((ROWS, D_MODEL), lambda i: (i, 0))
    row128 = pl.BlockSpec((ROWS, LANES), lambda i: (i, 0))
    if final:
        out_specs = [pl.BlockSpec((ROWS, D_MODEL), lambda i: (jnp.minimum(i, NB_P - 1), 0)),
                     pl.BlockSpec((ROWS, D_MODEL), lambda i: (_latent_block(i), 0))]
        out_shape = [jax.ShapeDtypeStruct((T_P, D_MODEL), F32), jax.ShapeDtypeStruct((T_S, D_MODEL), F32)]
    else:
        out_specs = rowd
        out_shape = jax.ShapeDtypeStruct((T_ALL, D_MODEL), F32)
    return pl.pallas_call(
        functools.partial(_combine_body, final),
        grid=(NB_TOK,),
        in_specs=[
            pl.BlockSpec((None, SUB, LANES), lambda i: (i, 0, 0), memory_space=pltpu.SMEM),
            pl.BlockSpec((None, SUB, LANES), lambda i: (jnp.minimum(i + 1, NB_TOK - 1), 0, 0),
                         memory_space=pltpu.SMEM),
            pl.BlockSpec((None, SUB, LANES), lambda i: (i, 0, 0)),
            pl.BlockSpec(memory_space=pl.ANY),
            row128, row128, row128,
            rowd,
            pl.BlockSpec((None, 1, 6 * D_MODEL), lambda i: (_mod_row(i), 0, 0)),
            _full((1, D_MODEL)),
        ],
        out_specs=out_specs,
        out_shape=out_shape,
        scratch_shapes=[
            pltpu.VMEM((STAGE_ROWS, D_MODEL), F32),
            pltpu.VMEM((STAGE_ROWS, D_MODEL), F32),
            pltpu.VMEM((ROWS, D_MODEL), F32),
            pltpu.SemaphoreType.DMA((2,)),
        ],
        compiler_params=pltpu.CompilerParams(vmem_limit_bytes=VMEM_LIMIT),
        name="moe_combine",
    )(tab, tab, tab, yp, te, tg, pr, xmid, mods_l, final_g)


def kernel(x_prompt, x_sample, cache_na_kv, cache_gqa_kv, c, c_ctx, norm1_g, norm2_g, w_mod, b_mod,
           w_in, w_fourier, na_rpb, q_norm_g, k_norm_g, w_pool, pool_scale, w_out, router_w, router_b,
           w_gu, b_gu, w_down, b_down, final_g):
    def lin_tables(L):
        cl, sl, bc, bs = _dft_tables(L)
        bands, icnt = _pool_tables(L)
        return (jnp.asarray(bc).astype(BF16), jnp.asarray(bs).astype(BF16),
                jnp.asarray(cl).astype(BF16), jnp.asarray(sl).astype(BF16),
                jnp.asarray(bands).astype(BF16), jnp.asarray(icnt))

    tabs_p = lin_tables(SEQ)
    tabs_s = lin_tables(DEC_SEQ)
    cos_np, sin_np = _rope_tables()
    cos_t, sin_t = jnp.asarray(cos_np), jnp.asarray(sin_np)
    gm = jnp.asarray(_group_mean_matrix(GROUP_W)).astype(BF16)

    cond8 = jnp.zeros((SUB, D_MODEL), F32).at[0].set(c_ctx).at[1:1 + DEC_BATCH].set(c)
    mods = _mods_call(cond8, w_mod, b_mod.reshape(DEPTH, 1, 6 * D_MODEL))
    mods = mods.reshape(DEPTH, SUB, 1, 6 * D_MODEL)

    bias_tiles = _nabias_call(na_rpb.reshape(DEPTH * N_HEADS, N_DR * N_DC))
    bias_tiles = bias_tiles.reshape(DEPTH, N_HEADS, N_DR - 1, GRID_W, LANES)

    w_in_bf = w_in.astype(BF16)
    w_out_bf = w_out.astype(BF16)
    w_f_bf = w_fourier.astype(BF16)
    eye4 = jnp.eye(4, dtype=F32)
    w_pool_bd = jnp.einsum('lgce,gh->lgche', w_pool, eye4).reshape(DEPTH, GROUP_W, GROUP_W).astype(BF16)
    qg = jnp.tile(q_norm_g, (1, N_HEADS)).reshape(DEPTH, 1, GROUP_W)
    kg = jnp.tile(k_norm_g, (1, N_HEADS)).reshape(DEPTH, 1, GROUP_W)
    rw_pad = jnp.zeros((DEPTH, D_MODEL, LANES), F32).at[:, :, :N_EXPERTS].set(router_w)
    rw_hi = rw_pad.astype(BF16)
    rw_lo = (rw_pad - rw_hi.astype(F32)).astype(BF16)
    rb_pad = jnp.full((DEPTH, 1, LANES), NEG_INF, F32).at[:, 0, :N_EXPERTS].set(router_b)
    b_gu4 = b_gu.reshape(DEPTH, N_EXPERTS, 1, 2 * D_FF)
    b_down4 = b_down.reshape(DEPTH, N_EXPERTS, 1, D_MODEL)
    final_g2 = final_g.reshape(1, D_MODEL)
    cache_na = cache_na_kv.reshape(DEC_BATCH, DEPTH, 2, PAST_LEN, GROUP_W)
    cache_gqa = cache_gqa_kv.reshape(DEC_BATCH, DEPTH, 2, PAST_LEN, KV_W)

    x = jnp.concatenate([x_prompt.reshape(T_P, D_MODEL), x_sample.reshape(T_S, D_MODEL)], axis=0)
    na_glob = jnp.zeros((BATCH, DEPTH, 2, SEQ, GROUP_W), F32)
    gqa_glob = jnp.zeros((BATCH, DEPTH, 2, SEQ, KV_W), F32)
    for l in range(DEPTH):
        g1 = norm1_g[l].reshape(1, D_MODEL)
        g2 = norm2_g[l].reshape(1, D_MODEL)
        ps = pool_scale[l].reshape(1, GROUP_W)

        uf, naq, gq, up, na_glob, gqa_glob, nakv_s, gkv_s = _k1_call(
            l, x, mods[l], g1, w_in_bf[l], qg[l], kg[l], gm, cos_t, sin_t, na_glob, gqa_glob)

        a_p, d_p = _linmix_call(uf, up, SEQ, 0, BATCH, tabs_p, w_f_bf[l], w_pool_bd[l], ps, "linmix_ctx")
        a_s, d_s = _linmix_call(uf, up, DEC_SEQ, T_P // DEC_SEQ, DEC_BATCH, tabs_s, w_f_bf[l], w_pool_bd[l], ps,
                                "linmix_latent")
        b_p, c_p = _attn_p_call(l, naq, na_glob, gq, gqa_glob)
        b_s = _na_s_call(naq, nakv_s, cache_na[:, l], bias_tiles[l])
        c_s = _gqa_s_call(gq, gkv_s, cache_gqa[:, l])

        xmid, h2, te, tg, pr, cnt = _k3_call(x, (a_p, b_p, c_p, d_p), (a_s, b_s, c_s, d_s), mods[l], w_out_bf[l],
                                             g2, rw_hi[l], rw_lo[l], rb_pad[l])
        tab, ztab, first_blk, n_blk, n_valid, n_used = _routing_tables(cnt)
        xs = _dispatch_call(tab, ztab, h2, pr)
        yp = _moe_call(l, first_blk, n_blk, n_valid, n_used, xs, w_gu, b_gu4, w_down, b_down4)
        x = _combine_call(tab, yp, te, tg, pr, xmid, mods[l], final_g2, l == DEPTH - 1)

    y_ctx, y_lat = x
    y_prompt = y_ctx.reshape(BATCH, SEQ, D_MODEL)
    y_sample = y_lat.reshape(DEC_BATCH, DEC_SEQ, D_MODEL)
    new_na_kv = na_glob.reshape(BATCH, DEPTH, 2, SEQ, N_HEADS, HEAD_DIM)
    new_gqa_kv = gqa_glob.reshape(BATCH, DEPTH, 2, SEQ, 2, HEAD_DIM)
    return (y_prompt, y_sample, new_na_kv, new_gqa_kv)
```

```python
import functools

import numpy as np
import jax
import jax.numpy as jnp
from jax import lax
from jax.experimental import pallas as pl
from jax.experimental.pallas import tpu as pltpu

F32 = jnp.float32
BF16 = jnp.bfloat16
I32 = jnp.int32
U32 = jnp.uint32

D_MODEL = 1024
BATCH = 32
SEQ = 256
DEPTH = 4
DEC_BATCH = 2
DEC_SEQ = 1024
PAST_LEN = 256
GRID_W = 64
GRID_ROWS = DEC_SEQ // GRID_W
HEAD_DIM = 64
GROUP_W = 256
N_HEADS = GROUP_W // HEAD_DIM
KV_W = 128
POOL_WINDOWS = (2, 4, 8, 16)
NA_KH = 8
NA_KW = 16
N_EXPERTS = 32
TOP_K = 4
D_FF = 1024
SWIGLU_LIMIT = 7.0
SWIGLU_ALPHA = 1.702
ROPE_THETA = 10000.0
EPS = 1e-6
NEG_INF = -1e30
PROJ_W = 1792

ROWS = 256
SUB = 8
LANES = 128
ROW_TILES = D_MODEL // LANES
T_P = BATCH * SEQ
T_S = DEC_BATCH * DEC_SEQ
T_ALL = T_P + T_S
NB_P = T_P // ROWS
NB_S = T_S // ROWS
NB_TOK = NB_P + NB_S
SUB_S = DEC_SEQ // ROWS
MOE_ROWS = 512
N_ASSIGN = T_ALL * TOP_K
BLOCK_ASSIGN = ROWS * TOP_K
SEG = 16
SEG_SHIFT = 4
N_BLOCKS = -(-(N_ASSIGN + N_EXPERTS * (SEG - 1 + MOE_ROWS - 1)) // MOE_ROWS)
N_SORTED = N_BLOCKS * MOE_ROWS
STAGE_ROWS = 1792
VMEM_LIMIT = 56 * 1024 * 1024


def _dft_tables(L):
    n = np.arange(L)
    ang = 2.0 * np.pi * ((n[:, None] * n[None, :]) % L) / L
    c64 = np.arange(HEAD_DIM)
    a64 = 2.0 * np.pi * ((c64[:, None] * c64[None, :]) % HEAD_DIM) / HEAD_DIM
    bc = np.kron(np.eye(N_HEADS), np.cos(a64))
    bs = np.kron(np.eye(N_HEADS), np.sin(a64))
    return (np.cos(ang).astype(np.float32), np.sin(ang).astype(np.float32),
            bc.astype(np.float32), bs.astype(np.float32))


def _pool_tables(L):
    t = np.arange(L)
    bands = np.zeros((len(POOL_WINDOWS), L, L), np.float32)
    icnt = np.zeros((L, GROUP_W), np.float64)
    for gi, win in enumerate(POOL_WINDOWS):
        a = np.clip(t - win // 2, 0, L)
        b = np.clip(t + win // 2, 0, L)
        n = t[None, :]
        bands[gi] = ((n >= a[:, None]) & (n < b[:, None])).astype(np.float32)
        icnt[:, gi * 64:(gi + 1) * 64] = (1.0 / (b - a))[:, None]
    return bands, icnt.astype(np.float32)


def _rope_tables():
    t = np.arange(DEC_SEQ)
    row = (t // GRID_W).astype(np.float64)
    col = (t % GRID_W).astype(np.float64)
    n_freq = HEAD_DIM // 4
    inv = ROPE_THETA ** (-np.arange(n_freq, dtype=np.float64) / n_freq)
    ar = row[:, None] * inv[None, :]
    ac = col[:, None] * inv[None, :]
    cos = np.concatenate([np.cos(ar), np.cos(ar), np.cos(ac), np.cos(ac)], axis=1)
    sin = np.concatenate([-np.sin(ar), np.sin(ar), -np.sin(ac), np.sin(ac)], axis=1)
    cos = np.concatenate([np.ones((ROWS, HEAD_DIM)), cos], axis=0)
    sin = np.concatenate([np.zeros((ROWS, HEAD_DIM)), sin], axis=0)
    return (np.tile(cos, (1, N_HEADS)).astype(np.float32), np.tile(sin, (1, N_HEADS)).astype(np.float32))


def _group_mean_matrix(width):
    g = np.kron(np.eye(width // HEAD_DIM), np.ones((HEAD_DIM, HEAD_DIM)))
    return g.astype(np.float32)


def _dot(a, b):
    return jnp.dot(a, b, preferred_element_type=F32)


def _dot_nt(a, b):
    return lax.dot_general(a, b, (((1,), (1,)), ((), ())), preferred_element_type=F32)


def _split_bf16(x):
    hi = x.astype(BF16)
    lo = (x - hi.astype(F32)).astype(BF16)
    return hi, lo


def _rmsnorm_rows(x, g):
    ms = jnp.mean(x * x, axis=-1, keepdims=True)
    return x * lax.rsqrt(ms + EPS) * g


def _softmax_rows(s):
    m = jnp.max(s, axis=-1, keepdims=True)
    p = jnp.exp(s - m)
    return p, jnp.sum(p, axis=-1, keepdims=True)


def _full(shape):
    return pl.BlockSpec(shape, lambda *_: (0,) * len(shape))


def _mod_row(i):
    return jnp.where(i < NB_P, 0, 1 + (i - NB_P) // SUB_S)


def _latent_block(i):
    return jnp.maximum(i - NB_P, 0)


MOD_COLS = 512


def _mods_body(cond_ref, w_ref, b_ref, o_ref):
    c = cond_ref[...]
    s = c * (1.0 / (1.0 + jnp.exp(-c)))
    s_hi, s_lo = _split_bf16(s)
    w_hi, w_lo = _split_bf16(w_ref[...])
    o_ref[...] = _dot(s_hi, w_hi) + _dot(s_lo, w_hi) + _dot(s_hi, w_lo) + b_ref[...]


def _mods_call(cond8, w_mod, b_mod3):
    n_col = 6 * D_MODEL // MOD_COLS
    return pl.pallas_call(
        _mods_body,
        grid=(DEPTH, n_col),
        in_specs=[
            pl.BlockSpec((SUB, D_MODEL), lambda l, j: (0, 0)),
            pl.BlockSpec((None, D_MODEL, MOD_COLS), lambda l, j: (l, 0, j)),
            pl.BlockSpec((None, 1, MOD_COLS), lambda l, j: (l, 0, j)),
        ],
        out_specs=pl.BlockSpec((None, SUB, MOD_COLS), lambda l, j: (l, 0, j)),
        out_shape=jax.ShapeDtypeStruct((DEPTH, SUB, 6 * D_MODEL), F32),
        name="adaln_mods",
    )(cond8, w_mod, b_mod3)


def _head_norm(t, g, gm):
    hi, lo = _split_bf16(t * t)
    msq = (_dot(hi, gm) + _dot(lo, gm)) * (1.0 / HEAD_DIM)
    return t * lax.rsqrt(msq + EPS) * g


def _rope(t, cos, sin):
    w = t.shape[1]
    lane = lax.broadcasted_iota(I32, t.shape, 1)
    first = (lane % 32) < 16
    swapped = jnp.where(first, pltpu.roll(t, w - 16, 1), pltpu.roll(t, 16, 1))
    return t * cos + swapped * sin


def _k1_body(x_ref, mod_ref, g1_ref, win_ref, qg_ref, kg_ref, gm_ref, cos_ref, sin_ref, *rest):
    uf_ref, naq_ref, gq_ref, up_ref, nakv_ref, gkv_ref, nakv_s_ref, gkv_s_ref = rest[-8:]
    i = pl.program_id(0)
    sh1 = mod_ref[:, 0:D_MODEL]
    sc1 = mod_ref[:, D_MODEL:2 * D_MODEL]
    h = _rmsnorm_rows(x_ref[...], g1_ref[...]) * (1.0 + sc1) + sh1
    hb = h.astype(BF16)

    def proj(lo, hi):
        return _dot(hb, win_ref[:, lo:hi])

    uf_ref[...] = proj(0, 256).astype(BF16)
    naq_ref[...] = proj(256, 512).astype(BF16)
    na_k = proj(512, 768)
    na_v = proj(768, 1024)
    gq = _head_norm(proj(1024, 1280), qg_ref[...], gm_ref[...])
    gk = _head_norm(proj(1280, 1408), kg_ref[:, 0:KV_W], gm_ref[0:KV_W, 0:KV_W])
    gv = proj(1408, 1536)
    up_ref[...] = proj(1536, 1792)
    gq = _rope(gq, cos_ref[...], sin_ref[...])
    gk = _rope(gk, cos_ref[:, 0:KV_W], sin_ref[:, 0:KV_W])
    gq_ref[...] = gq.astype(BF16)

    @pl.when(i < NB_P)
    def _():
        nakv_ref[0] = na_k
        nakv_ref[1] = na_v
        gkv_ref[0] = gk
        gkv_ref[1] = gv

    @pl.when(i >= NB_P)
    def _():
        nakv_s_ref[0] = na_k.astype(BF16)
        nakv_s_ref[1] = na_v.astype(BF16)
        gkv_s_ref[0] = gk.astype(BF16)
        gkv_s_ref[1] = gv.astype(BF16)


def _k1_call(layer, x, mods_l, g1, w_in_bf, qg, kg, gm, cos, sin, na_glob, gqa_glob):
    rope_blk = lambda i: (jnp.where(i < NB_P, 0, 1 + (i - NB_P) % SUB_S), 0)
    ctx_blk = lambda i: (jnp.minimum(i, NB_P - 1), layer, 0, 0, 0)
    lat_blk = lambda i: (_latent_block(i) // SUB_S, 0, _latent_block(i) % SUB_S, 0)
    row256 = pl.BlockSpec((ROWS, GROUP_W), lambda i: (i, 0))
    in_specs = [
        pl.BlockSpec((ROWS, D_MODEL), lambda i: (i, 0)),
        pl.BlockSpec((None, 1, 6 * D_MODEL), lambda i: (_mod_row(i), 0, 0)),
        _full((1, D_MODEL)),
        _full((D_MODEL, PROJ_W)),
        _full((1, GROUP_W)),
        _full((1, GROUP_W)),
        _full((GROUP_W, GROUP_W)),
        pl.BlockSpec((ROWS, GROUP_W), rope_blk),
        pl.BlockSpec((ROWS, GROUP_W), rope_blk),
    ]
    in_specs += [pl.BlockSpec(memory_space=pl.ANY)] * 2
    args = [x, mods_l, g1, w_in_bf, qg, kg, gm, cos, sin, na_glob, gqa_glob]
    return pl.pallas_call(
        _k1_body,
        grid=(NB_TOK,),
        in_specs=in_specs,
        out_specs=[
            row256, row256, row256, row256,
            pl.BlockSpec((None, None, 2, ROWS, GROUP_W), ctx_blk),
            pl.BlockSpec((None, None, 2, ROWS, KV_W), ctx_blk),
            pl.BlockSpec((None, 2, ROWS, GROUP_W), lat_blk),
            pl.BlockSpec((None, 2, ROWS, KV_W), lat_blk),
        ],
        out_shape=[
            jax.ShapeDtypeStruct((T_ALL, GROUP_W), BF16),
            jax.ShapeDtypeStruct((T_ALL, GROUP_W), BF16),
            jax.ShapeDtypeStruct((T_ALL, GROUP_W), BF16),
            jax.ShapeDtypeStruct((T_ALL, GROUP_W), F32),
            jax.ShapeDtypeStruct((BATCH, DEPTH, 2, SEQ, GROUP_W), F32),
            jax.ShapeDtypeStruct((BATCH, DEPTH, 2, SEQ, KV_W), F32),
            jax.ShapeDtypeStruct((DEC_BATCH, 2, DEC_SEQ, GROUP_W), BF16),
            jax.ShapeDtypeStruct((DEC_BATCH, 2, DEC_SEQ, KV_W), BF16),
        ],
        input_output_aliases={9: 4, 10: 5},
        compiler_params=pltpu.CompilerParams(vmem_limit_bytes=VMEM_LIMIT),
        name="k1_norm_proj",
    )(*args)


def _linmix_body(seq, uf_ref, up_ref, bc_ref, bs_ref, cl_ref, sl_ref, wf_ref, band_ref, icnt_ref,
                 wp_ref, ps_ref, a_ref, d_ref):
    u = uf_ref[...]
    t1 = _dot(u, bc_ref[...]).astype(BF16)
    t2 = _dot(u, bs_ref[...]).astype(BF16)
    f = (_dot(cl_ref[...], t1) - _dot(sl_ref[...], t2)) * (1.0 / float(np.sqrt(seq * HEAD_DIM)))
    a_ref[...] = _dot(f.astype(BF16), wf_ref[...]).astype(BF16)

    up = up_ref[...]
    up_hi, up_lo = _split_bf16(up)
    group = lax.broadcasted_iota(I32, up.shape, 1) // HEAD_DIM
    win_sum = jnp.zeros_like(up)
    for gi in range(len(POOL_WINDOWS)):
        s = _dot(band_ref[gi], up_hi) + _dot(band_ref[gi], up_lo)
        win_sum = jnp.where(group == gi, s, win_sum)
    delta = win_sum * icnt_ref[...] - up
    d_ref[...] = (_dot(delta.astype(BF16), wp_ref[...]) * ps_ref[...]).astype(BF16)


def _linmix_call(uf, up, seq, first_blk, n_blk, tabs, wf_bf, wp_bd_bf, ps, name):
    bc, bs, cl, sl, bands, icnt = tabs
    blk = pl.BlockSpec((seq, GROUP_W), lambda i: (first_blk + i, 0))
    oblk = pl.BlockSpec((seq, GROUP_W), lambda i: (i, 0))
    return pl.pallas_call(
        functools.partial(_linmix_body, seq),
        grid=(n_blk,),
        in_specs=[
            blk, blk,
            _full((GROUP_W, GROUP_W)), _full((GROUP_W, GROUP_W)),
            _full((seq, seq)), _full((seq, seq)),
            _full((GROUP_W, GROUP_W)),
            _full((len(POOL_WINDOWS), seq, seq)),
            _full((seq, GROUP_W)),
            _full((GROUP_W, GROUP_W)),
            _full((1, GROUP_W)),
        ],
        out_specs=[oblk, oblk],
        out_shape=[jax.ShapeDtypeStruct((n_blk * seq, GROUP_W), BF16)] * 2,
        compiler_params=pltpu.CompilerParams(vmem_limit_bytes=VMEM_LIMIT),
        name=name,
    )(uf, up, bc, bs, cl, sl, wf_bf, bands, icnt, wp_bd_bf, ps)


def _attend(q, k, v):
    s = _dot_nt(q, k) * (HEAD_DIM ** -0.5)
    p, l = _softmax_rows(s)
    return _dot(p.astype(BF16), v) / l


def _attn_p_body(naq_ref, nakv_ref, gq_ref, gkv_ref, b_ref, c_ref):
    for h in range(N_HEADS):
        sl = slice(h * HEAD_DIM, (h + 1) * HEAD_DIM)
        k = nakv_ref[0, :, sl].astype(BF16)
        v = nakv_ref[1, :, sl].astype(BF16)
        b_ref[:, sl] = _attend(naq_ref[:, sl], k, v).astype(BF16)
    for h in range(N_HEADS):
        sl = slice(h * HEAD_DIM, (h + 1) * HEAD_DIM)
        kvh = h // 2
        ksl = slice(kvh * HEAD_DIM, (kvh + 1) * HEAD_DIM)
        k = gkv_ref[0, :, ksl].astype(BF16)
        v = gkv_ref[1, :, ksl].astype(BF16)
        c_ref[:, sl] = _attend(gq_ref[:, sl], k, v).astype(BF16)


def _attn_p_call(layer, naq, na_glob, gq, gqa_glob):
    row = pl.BlockSpec((SEQ, GROUP_W), lambda i: (i, 0))
    return pl.pallas_call(
        _attn_p_body,
        grid=(BATCH,),
        in_specs=[
            row,
            pl.BlockSpec((None, None, 2, SEQ, GROUP_W), lambda i: (i, layer, 0, 0, 0)),
            row,
            pl.BlockSpec((None, None, 2, SEQ, KV_W), lambda i: (i, layer, 0, 0, 0)),
        ],
        out_specs=[row, row],
        out_shape=[jax.ShapeDtypeStruct((T_P, GROUP_W), BF16)] * 2,
        name="attn_ctx",
    )(naq, na_glob, gq, gqa_glob)


N_DR = 2 * NA_KH - 1
N_DC = 2 * NA_KW - 1


def _nabias_body(rpb_ref, o_ref):
    lh = pl.program_id(0)
    q = lax.broadcasted_iota(I32, (GRID_W, LANES), 0)
    j = lax.broadcasted_iota(I32, (GRID_W, LANES), 1)
    k = j % GRID_W
    second = j >= GRID_W
    c0 = jnp.clip(q - NA_KW // 2, 0, GRID_W - NA_KW)
    valid = (k >= c0) & (k < c0 + NA_KW)
    diff = k - q + (NA_KW - 1)
    for dr in range(N_DR - 1):
        acc = jnp.zeros((GRID_W, LANES), F32)
        for dc in range(N_DC):
            val = jnp.where(second, rpb_ref[lh, (dr + 1) * N_DC + dc], rpb_ref[lh, dr * N_DC + dc])
            acc = jnp.where(diff == dc, val, acc)
        o_ref[dr] = jnp.where(valid, acc, NEG_INF)


def _nabias_call(rpb_flat):
    n = rpb_flat.shape[0]
    return pl.pallas_call(
        _nabias_body,
        grid=(n,),
        in_specs=[pl.BlockSpec(memory_space=pltpu.SMEM)],
        out_specs=pl.BlockSpec((None, N_DR - 1, GRID_W, LANES), lambda i: (i, 0, 0, 0)),
        out_shape=jax.ShapeDtypeStruct((n, N_DR - 1, GRID_W, LANES), F32),
        name="na_bias_tiles",
    )(rpb_flat)


N_WIN = NA_KH * GRID_W


def _na_s_body(q_ref, kv_ref, ckv_ref, bias_ref, o_ref):
    r = pl.program_id(1)
    r0 = jnp.clip(r - NA_KH // 2, 0, GRID_ROWS - NA_KH)
    start = pl.multiple_of(r0 * GRID_W, GRID_W)
    dr0 = r0 - r + (NA_KH - 1)
    scale = HEAD_DIM ** -0.5
    for h in range(N_HEADS):
        sl = slice(h * HEAD_DIM, (h + 1) * HEAD_DIM)
        q = q_ref[:, sl]
        kw = kv_ref[0, pl.ds(start, N_WIN), sl]
        vw = kv_ref[1, pl.ds(start, N_WIN), sl]
        kc = ckv_ref[0, :, sl].astype(BF16)
        vc = ckv_ref[1, :, sl].astype(BF16)
        bias = jnp.concatenate([bias_ref[h, dr0 + i] for i in range(0, NA_KH, 2)], axis=1)
        s_win = _dot_nt(q, kw) * scale + bias
        s_ctx = _dot_nt(q, kc) * scale
        s = jnp.concatenate([s_win, s_ctx], axis=1)
        p, l = _softmax_rows(s)
        pb = p.astype(BF16)
        o = _dot(pb[:, 0:N_WIN], vw) + _dot(pb[:, N_WIN:], vc)
        o_ref[:, sl] = (o / l).astype(BF16)


def _na_s_call(naq, nakv_s, cache_na_l, bias_l):
    first = T_P // GRID_W
    return pl.pallas_call(
        _na_s_body,
        grid=(DEC_BATCH, GRID_ROWS),
        in_specs=[
            pl.BlockSpec((GRID_W, GROUP_W), lambda b, r: (first + b * GRID_ROWS + r, 0)),
            pl.BlockSpec((None, 2, DEC_SEQ, GROUP_W), lambda b, r: (b, 0, 0, 0)),
            pl.BlockSpec((None, 2, PAST_LEN, GROUP_W), lambda b, r: (b, 0, 0, 0)),
            _full((N_HEADS, N_DR - 1, GRID_W, LANES)),
        ],
        out_specs=pl.BlockSpec((GRID_W, GROUP_W), lambda b, r: (b * GRID_ROWS + r, 0)),
        out_shape=jax.ShapeDtypeStruct((T_S, GROUP_W), BF16),
        compiler_params=pltpu.CompilerParams(vmem_limit_bytes=VMEM_LIMIT),
        name="attn_na_latent",
    )(naq, nakv_s, cache_na_l, bias_l)


GQ_ROWS = 128


def _gqa_s_body(q_ref, kv_ref, ckv_ref, o_ref):
    scale = HEAD_DIM ** -0.5
    for h in range(N_HEADS):
        sl = slice(h * HEAD_DIM, (h + 1) * HEAD_DIM)
        kvh = h // 2
        ksl = slice(kvh * HEAD_DIM, (kvh + 1) * HEAD_DIM)
        q = q_ref[:, sl]
        kl = kv_ref[0, :, ksl]
        vl = kv_ref[1, :, ksl]
        kc = ckv_ref[0, :, ksl].astype(BF16)
        vc = ckv_ref[1, :, ksl].astype(BF16)
        s = jnp.concatenate([_dot_nt(q, kl), _dot_nt(q, kc)], axis=1) * scale
        p, l = _softmax_rows(s)
        pb = p.astype(BF16)
        o = _dot(pb[:, 0:DEC_SEQ], vl) + _dot(pb[:, DEC_SEQ:], vc)
        o_ref[:, sl] = (o / l).astype(BF16)


def _gqa_s_call(gq, gkv_s, cache_gqa_l):
    nq = DEC_SEQ // GQ_ROWS
    first = T_P // GQ_ROWS
    return pl.pallas_call(
        _gqa_s_body,
        grid=(DEC_BATCH, nq),
        in_specs=[
            pl.BlockSpec((GQ_ROWS, GROUP_W), lambda b, j: (first + b * nq + j, 0)),
            pl.BlockSpec((None, 2, DEC_SEQ, KV_W), lambda b, j: (b, 0, 0, 0)),
            pl.BlockSpec((None, 2, PAST_LEN, KV_W), lambda b, j: (b, 0, 0, 0)),
        ],
        out_specs=pl.BlockSpec((GQ_ROWS, GROUP_W), lambda b, j: (b * nq + j, 0)),
        out_shape=jax.ShapeDtypeStruct((T_S, GROUP_W), BF16),
        compiler_params=pltpu.CompilerParams(vmem_limit_bytes=VMEM_LIMIT),
        name="attn_gqa_latent",
    )(gq, gkv_s, cache_gqa_l)


def _k3_body(x_ref, ap_ref, bp_ref, cp_ref, dp_ref, as_ref, bs_ref, cs_ref, ds_ref, mod_ref, wout_ref, g2_ref,
             rwh_ref, rwl_ref, rb_ref, xmid_ref, h2_ref, te_ref, tg_ref, pr_ref, cnt_ref):
    is_ctx = pl.program_id(0) < NB_P
    acc = jnp.zeros((ROWS, D_MODEL), F32)
    for g, (p_ref, s_ref) in enumerate(((ap_ref, as_ref), (bp_ref, bs_ref), (cp_ref, cs_ref), (dp_ref, ds_ref))):
        mixed = jnp.where(is_ctx, p_ref[...], s_ref[...])
        acc += _dot(mixed, wout_ref[g * GROUP_W:(g + 1) * GROUP_W, :])
    gate1 = mod_ref[:, 2 * D_MODEL:3 * D_MODEL]
    sh2 = mod_ref[:, 3 * D_MODEL:4 * D_MODEL]
    sc2 = mod_ref[:, 4 * D_MODEL:5 * D_MODEL]
    xm = x_ref[...] + gate1 * acc
    xmid_ref[...] = xm
    h2 = _rmsnorm_rows(xm, g2_ref[...]) * (1.0 + sc2) + sh2
    hi, lo = _split_bf16(h2)
    h2_ref[...] = hi
    logits = _dot(hi, rwh_ref[...]) + _dot(lo, rwh_ref[...]) + _dot(hi, rwl_ref[...]) + rb_ref[...]

    lane = lax.broadcasted_iota(I32, logits.shape, 1)
    cur = logits
    vals, idxs = [], []
    for _ in range(TOP_K):
        m = jnp.max(cur, axis=-1, keepdims=True)
        idx = jnp.min(jnp.where(cur == m, lane, LANES), axis=-1, keepdims=True)
        vals.append(m)
        idxs.append(idx)
        cur = jnp.where(lane == idx, -jnp.inf, cur)
    exps = [jnp.exp(v - vals[0]) for v in vals]
    tot = exps[0] + exps[1] + exps[2] + exps[3]
    te = jnp.zeros(logits.shape, I32)
    tg = jnp.zeros(logits.shape, F32)
    for k in range(TOP_K):
        te = jnp.where(lane == k, idxs[k], te)
        tg = jnp.where(lane == k, exps[k] / tot, tg)
    te_ref[...] = te
    tg_ref[...] = tg

    member = jnp.zeros(logits.shape, F32)
    for k in range(TOP_K):
        member = jnp.where(lane == idxs[k], 1.0, member)
    rr = lax.broadcasted_iota(I32, (ROWS, ROWS), 0)
    cc = lax.broadcasted_iota(I32, (ROWS, ROWS), 1)
    earlier = jnp.where(cc < rr, 1.0, 0.0).astype(BF16)
    prefix = _dot(earlier, member.astype(BF16))
    cnt = jnp.sum(member, axis=0, keepdims=True)
    er = lax.broadcasted_iota(I32, (LANES, LANES), 0)
    ec = lax.broadcasted_iota(I32, (LANES, LANES), 1)
    lower = jnp.where(er < ec, 1.0, 0.0).astype(BF16)
    loc = _dot(jnp.broadcast_to(cnt, (SUB, LANES)).astype(BF16), lower)[0:1, :]
    pr = jnp.zeros(logits.shape, F32)
    for k in range(TOP_K):
        hit = lane == idxs[k]
        rank = jnp.sum(jnp.where(hit, prefix, 0.0), axis=-1, keepdims=True)
        start = jnp.sum(jnp.where(hit, loc, 0.0), axis=-1, keepdims=True)
        pr = jnp.where(lane == k, start + rank, pr)
        pr = jnp.where(lane == TOP_K + k, rank, pr)
    pr_ref[...] = pr.astype(I32)
    cnt_ref[...] = cnt.astype(I32)


def _k3_call(x, mix_p, mix_s, mods_l, w_out_bf, g2, rwh, rwl, rb):
    row_p = pl.BlockSpec((ROWS, GROUP_W), lambda i: (jnp.minimum(i, NB_P - 1), 0))
    row_s = pl.BlockSpec((ROWS, GROUP_W), lambda i: (_latent_block(i), 0))
    rowd = pl.BlockSpec((ROWS, D_MODEL), lambda i: (i, 0))
    row128 = pl.BlockSpec((ROWS, LANES), lambda i: (i, 0))
    return pl.pallas_call(
        _k3_body,
        grid=(NB_TOK,),
        in_specs=[
            rowd, row_p, row_p, row_p, row_p, row_s, row_s, row_s, row_s,
            pl.BlockSpec((None, 1, 6 * D_MODEL), lambda i: (_mod_row(i), 0, 0)),
            _full((D_MODEL, D_MODEL)),
            _full((1, D_MODEL)),
            _full((D_MODEL, LANES)), _full((D_MODEL, LANES)), _full((1, LANES)),
        ],
        out_specs=[rowd, rowd, row128, row128, row128,
                   pl.BlockSpec((None, 1, LANES), lambda i: (i, 0, 0))],
        out_shape=[
            jax.ShapeDtypeStruct((T_ALL, D_MODEL), F32),
            jax.ShapeDtypeStruct((T_ALL, D_MODEL), BF16),
            jax.ShapeDtypeStruct((T_ALL, LANES), I32),
            jax.ShapeDtypeStruct((T_ALL, LANES), F32),
            jax.ShapeDtypeStruct((T_ALL, LANES), I32),
            jax.ShapeDtypeStruct((NB_TOK, 1, LANES), I32),
        ],
        compiler_params=pltpu.CompilerParams(vmem_limit_bytes=VMEM_LIMIT),
        name="k3_out_router",
    )(x, *mix_p, *mix_s, mods_l, w_out_bf, g2, rwh, rwl, rb)


TAB_CNT, TAB_LOC, TAB_BASE, TAB_LSTAGE, TAB_SOFF, TAB_TOTAL, TAB_NCHUNK = range(7)


def _routing_tables(cnt):
    c = cnt.reshape(NB_TOK, LANES)[:, :N_EXPERTS]
    counts = jnp.sum(c, axis=0)
    padded = (counts + (SEG - 1) + (MOE_ROWS - 1)) // MOE_ROWS * MOE_ROWS
    pend = jnp.cumsum(padded)
    pstart = pend - padded
    base = pstart[None, :] + jnp.cumsum(c, axis=0) - c
    loc = jnp.cumsum(c, axis=1) - c
    off = base % SUB
    nch = jnp.where(c > 0, (off + c + SEG - 1) // SEG, 0)
    lstage = (jnp.cumsum(nch, axis=1) - nch) * SEG
    soff = lstage + off
    total = jnp.sum(nch, axis=1, keepdims=True) * SEG

    def pad(t):
        return jnp.zeros((NB_TOK, LANES), I32).at[:, :t.shape[1]].set(t.astype(I32))

    tab = jnp.stack([pad(c), pad(loc), pad(base), pad(lstage), pad(soff), pad(total), pad(nch),
                     jnp.zeros((NB_TOK, LANES), I32)], axis=1)
    ztab = jnp.zeros((SUB, LANES), I32).at[0, :N_EXPERTS].set((pstart + counts).astype(I32))
    ztab = ztab.at[1, :N_EXPERTS].set(pend.astype(I32))
    ztab = ztab.at[2, 0].set((pend[-1] // MOE_ROWS).astype(I32))
    first_blk = (pstart // MOE_ROWS).astype(I32)
    n_blk = (padded // MOE_ROWS).astype(I32)
    n_used = (pend[-1] // MOE_ROWS).astype(I32).reshape(1)
    return tab, ztab, first_blk, n_blk, counts.astype(I32), n_used


def _dispatch_body(tab_ref, ztab_ref, h2_ref, pr_ref, xs_ref, stage0, stage1, zbuf, pending, sems):
    b = pl.program_id(0)
    sem = sems.at[0]

    def rows_copy(src, src_row, dst_row, n_rows, sem=sem):
        return pltpu.make_async_copy(
            src.at[pl.ds(pl.multiple_of(src_row * SUB, SUB), n_rows * SUB)],
            xs_ref.at[pl.ds(pl.multiple_of(dst_row * SUB, SUB), n_rows * SUB)], sem)

    def seg_copy(src, src_row, dst_row, sem=sem):
        return rows_copy(src, src_row, dst_row, SEG, sem)

    def drain(n, sem=sem):
        def body(j, carry):
            seg_copy(zbuf, 0, 0, sem).wait()
            return carry
        lax.fori_loop(0, n, body, 0)

    @pl.when(b == 0)
    def _():
        zbuf[...] = jnp.zeros(zbuf.shape, F32)
        stage0[pl.ds(BLOCK_ASSIGN * SUB, SEG * SUB), :] = jnp.zeros((SEG * SUB, LANES), F32)
        stage1[pl.ds(BLOCK_ASSIGN * SUB, SEG * SUB), :] = jnp.zeros((SEG * SUB, LANES), F32)

        def per_expert(e, carry):
            n_seg, n_row = carry
            end = ztab_ref[0, e]
            length = ztab_ref[1, e] - end
            n = lax.shift_right_logical(length, SEG_SHIFT)
            rem = length & (SEG - 1)

            def one(j, c):
                seg_copy(zbuf, 0, end + j * SEG).start()
                return c
            lax.fori_loop(0, n, one, 0)

            def one_row(j, c):
                rows_copy(zbuf, 0, end + n * SEG + j, 1).start()
                return c
            lax.fori_loop(0, rem, one_row, 0)
            return n_seg + n, n_row + rem
        n_seg, n_row = lax.fori_loop(0, N_EXPERTS, per_expert, (0, 0))
        drain(n_seg)

        def row_wait(j, c):
            rows_copy(zbuf, 0, 0, 1).wait()
            return c
        lax.fori_loop(0, n_row, row_wait, 0)

        first_free = ztab_ref[2, 0]

        def fill(j, c):
            rows_copy(zbuf, 0, (first_free + j) * MOE_ROWS, MOE_ROWS).start()
            return c
        lax.fori_loop(0, N_BLOCKS - first_free, fill, 0)

        def fill_wait(j, c):
            rows_copy(zbuf, 0, 0, MOE_ROWS).wait()
            return c
        lax.fori_loop(0, N_BLOCKS - first_free, fill_wait, 0)

    pos_t = pr_ref[...].astype(F32).T
    h2 = h2_ref[...]

    def sort(stage):
        for c in range(BLOCK_ASSIGN // ROWS):
            p = (lax.broadcasted_iota(I32, (ROWS, ROWS), 0) + c * ROWS).astype(F32)
            sel = p == pos_t[0:1, :]
            for k in range(1, TOP_K):
                sel = sel | (p == pos_t[k:k + 1, :])
            rows = _dot(jnp.where(sel, 1.0, 0.0).astype(BF16), h2)
            for s in range(ROW_TILES):
                stage[pl.ds(c * ROWS * SUB + s, ROWS, stride=SUB), :] = rows[:, s * LANES:(s + 1) * LANES]

    def send(stage, sem):
        def per_expert(e, tot):
            loc = tab_ref[TAB_LOC, e]
            base = tab_ref[TAB_BASE, e]
            n = lax.shift_right_logical(tab_ref[TAB_CNT, e] + (SEG - 1), SEG_SHIFT)

            def one(j, carry):
                seg_copy(stage, loc + j * SEG, base + j * SEG, sem).start()
                return carry
            lax.fori_loop(0, n, one, 0)
            return tot + n
        return lax.fori_loop(0, N_EXPERTS, per_expert, 0)

    for slot, stage in enumerate((stage0, stage1)):
        @pl.when((b & 1) == slot)
        def _(stage=stage, slot=slot):
            sort(stage)

            @pl.when(b > 0)
            def _():
                drain(pending[0], sems.at[1 - slot])
            n_sent = send(stage, sems.at[slot])
            pending[0] = n_sent

            @pl.when(b == NB_TOK - 1)
            def _():
                drain(n_sent, sems.at[slot])


def _dispatch_call(tab, ztab, h2, pr):
    return pl.pallas_call(
        _dispatch_body,
        grid=(NB_TOK,),
        in_specs=[
            pl.BlockSpec((None, SUB, LANES), lambda i: (i, 0, 0), memory_space=pltpu.SMEM),
            pl.BlockSpec(memory_space=pltpu.SMEM),
            pl.BlockSpec((ROWS, D_MODEL), lambda i: (i, 0)),
            pl.BlockSpec((ROWS, LANES), lambda i: (i, 0)),
        ],
        out_specs=pl.BlockSpec(memory_space=pl.ANY),
        out_shape=jax.ShapeDtypeStruct((N_SORTED * ROW_TILES, LANES), F32),
        scratch_shapes=[
            pltpu.VMEM(((BLOCK_ASSIGN + SEG) * SUB, LANES), F32),
            pltpu.VMEM(((BLOCK_ASSIGN + SEG) * SUB, LANES), F32),
            pltpu.VMEM((MOE_ROWS * SUB, LANES), F32),
            pltpu.SMEM((1,), I32),
            pltpu.SemaphoreType.DMA((2,)),
        ],
        compiler_params=pltpu.CompilerParams(vmem_limit_bytes=VMEM_LIMIT),
        name="moe_dispatch",
    )(tab, ztab, h2, pr)


HALF_FF = D_FF // 2


def _moe_body(fb_ref, nb_ref, nv_ref, nu_ref, xs_ref, wgu_ref, bgu_ref, wd_ref, bd_ref, yp_ref,
              wgu_bf, wd_bf, wd_scr, xb_scr, act_scr, xbuf0, xbuf1, ybuf0, ybuf1, xsems, ysems):
    e = pl.program_id(0)
    first = fb_ref[e]
    n_blk = nb_ref[e]
    n_used = nu_ref[0]
    xbufs = (xbuf0, xbuf1)
    ybufs = (ybuf0, ybuf1)

    def x_copy(g, slot):
        src = xs_ref.at[pl.ds(pl.multiple_of(g * (MOE_ROWS * SUB), MOE_ROWS * SUB), MOE_ROWS * SUB)]
        return pltpu.make_async_copy(src, xbufs[slot], xsems.at[slot])

    def y_copy(g, slot):
        dst = yp_ref.at[pl.ds(pl.multiple_of(g * MOE_ROWS, MOE_ROWS), MOE_ROWS)]
        return pltpu.make_async_copy(ybufs[slot], dst, ysems.at[slot])

    @pl.when(e == 0)
    def _():
        x_copy(0, 0).start()

    @pl.when(n_blk > 0)
    def _():
        for c in range(4):
            cs = slice(c * 512, (c + 1) * 512)
            wgu_bf[:, cs] = wgu_ref[:, cs].astype(BF16)
        for c in range(ROW_TILES):
            cs = slice(c * LANES, (c + 1) * LANES)
            wd_scr[pl.ds(0, HALF_FF, stride=2), :] = wd_ref[0:HALF_FF, cs]
            wd_scr[pl.ds(1, HALF_FF, stride=2), :] = wd_ref[HALF_FF:D_FF, cs]
            wd_bf[:, cs] = wd_scr[...].astype(BF16)

    def ffn(x_ref, y_ref, m):
        for s in range(ROW_TILES):
            xb_scr[0:m, s * LANES:(s + 1) * LANES] = x_ref[pl.ds(s, m, stride=ROW_TILES), :].astype(BF16)
        even = (lax.broadcasted_iota(I32, (m, LANES), 1) % 2) == 0
        for c in range(D_FF // GROUP_W):
            ca = slice(c * GROUP_W, (c + 1) * GROUP_W)
            cb = slice(D_FF + c * GROUP_W, D_FF + (c + 1) * GROUP_W)
            ha_full = _dot(xb_scr[0:m, :], wgu_bf[:, ca]) + bgu_ref[:, ca]
            hb_full = _dot(xb_scr[0:m, :], wgu_bf[:, cb]) + bgu_ref[:, cb]
            for h in range(GROUP_W // LANES):
                ha = ha_full[:, h * LANES:(h + 1) * LANES]
                hb = hb_full[:, h * LANES:(h + 1) * LANES]
                gate = jnp.where(even, ha, pltpu.roll(hb, 1, 1))
                up = jnp.where(even, pltpu.roll(ha, LANES - 1, 1), hb)
                gate = jnp.minimum(gate, SWIGLU_LIMIT)
                up = jnp.clip(up, -SWIGLU_LIMIT, SWIGLU_LIMIT)
                glu = gate * (1.0 / (1.0 + jnp.exp(-gate * SWIGLU_ALPHA)))
                lo = c * GROUP_W + h * LANES
                act_scr[0:m, lo:lo + LANES] = ((up + 1.0) * glu).astype(BF16)
        for c in range(D_MODEL // GROUP_W):
            cs = slice(c * GROUP_W, (c + 1) * GROUP_W)
            y_ref[0:m, cs] = _dot(act_scr[0:m, :], wd_bf[:, cs]) + bd_ref[:, cs]
        if m < MOE_ROWS:
            y_ref[m:MOE_ROWS, :] = jnp.zeros((MOE_ROWS - m, D_MODEL), F32)

    def block(j, carry):
        g = first + j
        for slot in range(2):
            @pl.when((g & 1) == slot)
            def _(slot=slot):
                x_copy(g, slot).wait()

                @pl.when(g + 1 < n_used)
                def _():
                    x_copy(g + 1, 1 - slot).start()
                @pl.when(g >= 2)
                def _():
                    y_copy(g - 2, slot).wait()
                valid = nv_ref[e] - j * MOE_ROWS
                quarter = MOE_ROWS // 4
                for q in range(1, 5):
                    above = valid > (q - 1) * quarter if q > 1 else True
                    below = valid <= q * quarter if q < 4 else True

                    @pl.when(jnp.logical_and(above, below))
                    def _(q=q):
                        ffn(xbufs[slot], ybufs[slot], q * quarter)
                y_copy(g, slot).start()
        return carry
    lax.fori_loop(0, n_blk, block, 0)

    @pl.when(e == N_EXPERTS - 1)
    def _():
        for slot in range(2):
            @pl.when(((n_used - 1) & 1) == slot)
            def _(slot=slot):
                @pl.when(n_used >= 2)
                def _():
                    y_copy(n_used - 2, 1 - slot).wait()
                y_copy(n_used - 1, slot).wait()
        ybuf0[...] = jnp.zeros(ybuf0.shape, F32)

        def fill(j, carry):
            y_copy(n_used + j, 0).start()
            return carry
        lax.fori_loop(0, N_BLOCKS - n_used, fill, 0)

        def fill_wait(j, carry):
            y_copy(0, 0).wait()
            return carry
        lax.fori_loop(0, N_BLOCKS - n_used, fill_wait, 0)


def _moe_call(layer, first_blk, n_blk, n_valid, n_used, xs, w_gu, b_gu4, w_down, b_down4):
    wmap = lambda e, fb, nb, nv, nu: (layer, e, 0, 0)
    grid_spec = pltpu.PrefetchScalarGridSpec(
        num_scalar_prefetch=4,
        grid=(N_EXPERTS,),
        in_specs=[
            pl.BlockSpec(memory_space=pl.ANY),
            pl.BlockSpec((None, None, D_MODEL, 2 * D_FF), wmap),
            pl.BlockSpec((None, None, 1, 2 * D_FF), wmap),
            pl.BlockSpec((None, None, D_FF, D_MODEL), wmap),
            pl.BlockSpec((None, None, 1, D_MODEL), wmap),
        ],
        out_specs=pl.BlockSpec(memory_space=pl.ANY),
        scratch_shapes=[
            pltpu.VMEM((D_MODEL, 2 * D_FF), BF16),
            pltpu.VMEM((D_FF, D_MODEL), BF16),
            pltpu.VMEM((D_FF, LANES), F32),
            pltpu.VMEM((MOE_ROWS, D_MODEL), BF16),
            pltpu.VMEM((MOE_ROWS, D_FF), BF16),
            pltpu.VMEM((MOE_ROWS * SUB, LANES), F32),
            pltpu.VMEM((MOE_ROWS * SUB, LANES), F32),
            pltpu.VMEM((MOE_ROWS, D_MODEL), F32),
            pltpu.VMEM((MOE_ROWS, D_MODEL), F32),
            pltpu.SemaphoreType.DMA((2,)),
            pltpu.SemaphoreType.DMA((2,)),
        ],
    )
    return pl.pallas_call(
        _moe_body,
        grid_spec=grid_spec,
        out_shape=jax.ShapeDtypeStruct((N_SORTED, D_MODEL), F32),
        compiler_params=pltpu.CompilerParams(vmem_limit_bytes=VMEM_LIMIT),
        name="moe_experts",
    )(first_blk, n_blk, n_valid, n_used, xs, w_gu, b_gu4, w_down, b_down4)


def _combine_body(final, tab_ref, tabn_ref, tabv_ref, yp_ref, te_ref, tg_ref, pr_ref, xmid_ref, mod_ref, fg_ref,
                  *rest):
    outs = rest[:-4]
    stage0, stage1, acc, sems = rest[-4:]
    b = pl.program_id(0)
    stages = (stage0, stage1)

    def chunk_copy(src_row, dst_row, slot):
        return pltpu.make_async_copy(
            yp_ref.at[pl.ds(pl.multiple_of(src_row, SUB), SEG)],
            stages[slot].at[pl.ds(pl.multiple_of(dst_row, SUB), SEG)], sems.at[slot])

    def fetch(tab, slot):
        def per_expert(e, carry):
            base = tab[TAB_BASE, e]
            off = base & (SUB - 1)
            lstage = tab[TAB_LSTAGE, e]

            def one(j, c):
                chunk_copy(base - off + j * SEG, lstage + j * SEG, slot).start()
                return c
            lax.fori_loop(0, tab[TAB_NCHUNK, e], one, 0)
            return carry
        lax.fori_loop(0, N_EXPERTS, per_expert, 0)

        total = tab[TAB_TOTAL, 0]

        def zero_tail(j, c):
            stages[slot][pl.ds(pl.multiple_of(total + j * SEG, SEG), SEG), :] = jnp.zeros((SEG, D_MODEL), F32)
            return c
        lax.fori_loop(0, lax.shift_right_logical(STAGE_ROWS - total, SEG_SHIFT), zero_tail, 0)

    @pl.when(b == 0)
    def _():
        fetch(tab_ref, 0)

    for slot in range(2):
        @pl.when(((b & 1) == slot) & (b + 1 < NB_TOK))
        def _(slot=slot):
            fetch(tabn_ref, 1 - slot)

    total = tab_ref[TAB_TOTAL, 0]
    for slot in range(2):
        @pl.when((b & 1) == slot)
        def _(slot=slot):
            def drain(j, c):
                chunk_copy(0, 0, slot).wait()
                return c
            lax.fori_loop(0, lax.shift_right_logical(total, SEG_SHIFT), drain, 0)

    lane = lax.broadcasted_iota(I32, (ROWS, LANES), 1)
    te = te_ref[...]
    pr = pr_ref[...]
    soff = tabv_ref[TAB_SOFF:TAB_SOFF + 1, :].astype(F32)
    packed = jnp.where((lane >= TOP_K) & (lane < 2 * TOP_K), pltpu.roll(tg_ref[...], TOP_K, 1), 0.0)
    sp = []
    for k in range(TOP_K):
        seg_start = jnp.sum(jnp.where(lane == te[:, k:k + 1], soff, 0.0), axis=-1, keepdims=True)
        spk = seg_start + pr[:, TOP_K + k:TOP_K + k + 1].astype(F32)
        sp.append(spk)
        packed = jnp.where(lane == k, spk, packed)
    packed_t = packed.T

    def unsort(stage, c):
        pcol = (lax.broadcasted_iota(I32, (ROWS, ROWS), 1) + c * ROWS).astype(F32)
        sel = pcol == sp[0]
        for k in range(1, TOP_K):
            sel = sel | (pcol == sp[k])
        prow = (lax.broadcasted_iota(I32, (ROWS, ROWS), 0) + c * ROWS).astype(F32)
        gmat = jnp.zeros((ROWS, ROWS), F32)
        for k in range(TOP_K):
            gmat = jnp.where(prow == packed_t[k:k + 1, :], packed_t[TOP_K + k:TOP_K + k + 1, :], gmat)
        row_gate = jnp.sum(gmat, axis=-1, keepdims=True)
        rows = (stage[c * ROWS:(c + 1) * ROWS, :] * row_gate).astype(BF16)
        return _dot(jnp.where(sel, 1.0, 0.0).astype(BF16), rows)

    n_always = BLOCK_ASSIGN // ROWS
    for slot in range(2):
        @pl.when((b & 1) == slot)
        def _(slot=slot):
            out = unsort(stages[slot], 0)
            for c in range(1, n_always):
                out += unsort(stages[slot], c)
            acc[...] = out
            for c in range(n_always, STAGE_ROWS // ROWS):
                @pl.when(c * ROWS < total)
                def _(c=c):
                    acc[...] += unsort(stages[slot], c)

    gate2 = mod_ref[:, 5 * D_MODEL:6 * D_MODEL]
    x = xmid_ref[...] + gate2 * acc[...]
    if final:
        x = _rmsnorm_rows(x, fg_ref[...])
        out_ctx, out_lat = outs

        @pl.when(b < NB_P)
        def _():
            out_ctx[...] = x

        @pl.when(b >= NB_P)
        def _():
            out_lat[...] = x
    else:
        outs[0][...] = x


def _combine_call(tab, yp, te, tg, pr, xmid, mods_l, final_g, final):
    rowd = pl.BlockSpec((ROWS, D_MODEL), lambda i: (i, 0))
    row128 = pl.BlockSpec((ROWS, LANES), lambda i: (i, 0))
    if final:
        out_specs = [pl.BlockSpec((ROWS, D_MODEL), lambda i: (jnp.minimum(i, NB_P - 1), 0)),
                     pl.BlockSpec((ROWS, D_MODEL), lambda i: (_latent_block(i), 0))]
        out_shape = [jax.ShapeDtypeStruct((T_P, D_MODEL), F32), jax.ShapeDtypeStruct((T_S, D_MODEL), F32)]
    else:
        out_specs = rowd
        out_shape = jax.ShapeDtypeStruct((T_ALL, D_MODEL), F32)
    return pl.pallas_call(
        functools.partial(_combine_body, final),
        grid=(NB_TOK,),
        in_specs=[
            pl.BlockSpec((None, SUB, LANES), lambda i: (i, 0, 0), memory_space=pltpu.SMEM),
            pl.BlockSpec((None, SUB, LANES), lambda i: (jnp.minimum(i + 1, NB_TOK - 1), 0, 0),
                         memory_space=pltpu.SMEM),
            pl.BlockSpec((None, SUB, LANES), lambda i: (i, 0, 0)),
            pl.BlockSpec(memory_space=pl.ANY),
            row128, row128, row128,
            rowd,
            pl.BlockSpec((None, 1, 6 * D_MODEL), lambda i: (_mod_row(i), 0, 0)),
            _full((1, D_MODEL)),
        ],
        out_specs=out_specs,
        out_shape=out_shape,
        scratch_shapes=[
            pltpu.VMEM((STAGE_ROWS, D_MODEL), F32),
            pltpu.VMEM((STAGE_ROWS, D_MODEL), F32),
            pltpu.VMEM((ROWS, D_MODEL), F32),
            pltpu.SemaphoreType.DMA((2,)),
        ],
        compiler_params=pltpu.CompilerParams(vmem_limit_bytes=VMEM_LIMIT),
        name="moe_combine",
    )(tab, tab, tab, yp, te, tg, pr, xmid, mods_l, final_g)


def kernel(x_prompt, x_sample, cache_na_kv, cache_gqa_kv, c, c_ctx, norm1_g, norm2_g, w_mod, b_mod,
           w_in, w_fourier, na_rpb, q_norm_g, k_norm_g, w_pool, pool_scale, w_out, router_w, router_b,
           w_gu, b_gu, w_down, b_down, final_g):
    def lin_tables(L):
        cl, sl, bc, bs = _dft_tables(L)
        bands, icnt = _pool_tables(L)
        return (jnp.asarray(bc).astype(BF16), jnp.asarray(bs).astype(BF16),
                jnp.asarray(cl).astype(BF16), jnp.asarray(sl).astype(BF16),
                jnp.asarray(bands).astype(BF16), jnp.asarray(icnt))

    tabs_p = lin_tables(SEQ)
    tabs_s = lin_tables(DEC_SEQ)
    cos_np, sin_np = _rope_tables()
    cos_t, sin_t = jnp.asarray(cos_np), jnp.asarray(sin_np)
    gm = jnp.asarray(_group_mean_matrix(GROUP_W)).astype(BF16)

    cond8 = jnp.zeros((SUB, D_MODEL), F32).at[0].set(c_ctx).at[1:1 + DEC_BATCH].set(c)
    mods = _mods_call(cond8, w_mod, b_mod.reshape(DEPTH, 1, 6 * D_MODEL))
    mods = mods.reshape(DEPTH, SUB, 1, 6 * D_MODEL)

    bias_tiles = _nabias_call(na_rpb.reshape(DEPTH * N_HEADS, N_DR * N_DC))
    bias_tiles = bias_tiles.reshape(DEPTH, N_HEADS, N_DR - 1, GRID_W, LANES)

    w_in_bf = w_in.astype(BF16)
    w_out_bf = w_out.astype(BF16)
    w_f_bf = w_fourier.astype(BF16)
    eye4 = jnp.eye(4, dtype=F32)
    w_pool_bd = jnp.einsum('lgce,gh->lgche', w_pool, eye4).reshape(DEPTH, GROUP_W, GROUP_W).astype(BF16)
    qg = jnp.tile(q_norm_g, (1, N_HEADS)).reshape(DEPTH, 1, GROUP_W)
    kg = jnp.tile(k_norm_g, (1, N_HEADS)).reshape(DEPTH, 1, GROUP_W)
    rw_pad = jnp.zeros((DEPTH, D_MODEL, LANES), F32).at[:, :, :N_EXPERTS].set(router_w)
    rw_hi = rw_pad.astype(BF16)
    rw_lo = (rw_pad - rw_hi.astype(F32)).astype(BF16)
    rb_pad = jnp.full((DEPTH, 1, LANES), NEG_INF, F32).at[:, 0, :N_EXPERTS].set(router_b)
    b_gu4 = b_gu.reshape(DEPTH, N_EXPERTS, 1, 2 * D_FF)
    b_down4 = b_down.reshape(DEPTH, N_EXPERTS, 1, D_MODEL)
    final_g2 = final_g.reshape(1, D_MODEL)
    cache_na = cache_na_kv.reshape(DEC_BATCH, DEPTH, 2, PAST_LEN, GROUP_W)
    cache_gqa = cache_gqa_kv.reshape(DEC_BATCH, DEPTH, 2, PAST_LEN, KV_W)

    x = jnp.concatenate([x_prompt.reshape(T_P, D_MODEL), x_sample.reshape(T_S, D_MODEL)], axis=0)
    na_glob = jnp.zeros((BATCH, DEPTH, 2, SEQ, GROUP_W), F32)
    gqa_glob = jnp.zeros((BATCH, DEPTH, 2, SEQ, KV_W), F32)
    for l in range(DEPTH):
        g1 = norm1_g[l].reshape(1, D_MODEL)
        g2 = norm2_g[l].reshape(1, D_MODEL)
        ps = pool_scale[l].reshape(1, GROUP_W)

        uf, naq, gq, up, na_glob, gqa_glob, nakv_s, gkv_s = _k1_call(
            l, x, mods[l], g1, w_in_bf[l], qg[l], kg[l], gm, cos_t, sin_t, na_glob, gqa_glob)

        a_p, d_p = _linmix_call(uf, up, SEQ, 0, BATCH, tabs_p, w_f_bf[l], w_pool_bd[l], ps, "linmix_ctx")
        a_s, d_s = _linmix_call(uf, up, DEC_SEQ, T_P // DEC_SEQ, DEC_BATCH, tabs_s, w_f_bf[l], w_pool_bd[l], ps,
                                "linmix_latent")
        b_p, c_p = _attn_p_call(l, naq, na_glob, gq, gqa_glob)
        b_s = _na_s_call(naq, nakv_s, cache_na[:, l], bias_tiles[l])
        c_s = _gqa_s_call(gq, gkv_s, cache_gqa[:, l])

        xmid, h2, te, tg, pr, cnt = _k3_call(x, (a_p, b_p, c_p, d_p), (a_s, b_s, c_s, d_s), mods[l], w_out_bf[l],
                                             g2, rw_hi[l], rw_lo[l], rb_pad[l])
        tab, ztab, first_blk, n_blk, n_valid, n_used = _routing_tables(cnt)
        xs = _dispatch_call(tab, ztab, h2, pr)
        yp = _moe_call(l, first_blk, n_blk, n_valid, n_used, xs, w_gu, b_gu4, w_down, b_down4)
        x = _combine_call(tab, yp, te, tg, pr, xmid, mods[l], final_g2, l == DEPTH - 1)

    y_ctx, y_lat = x
    y_prompt = y_ctx.reshape(BATCH, SEQ, D_MODEL)
    y_sample = y_lat.reshape(DEC_BATCH, DEC_SEQ, D_MODEL)
    new_na_kv = na_glob.reshape(BATCH, DEPTH, 2, SEQ, N_HEADS, HEAD_DIM)
    new_gqa_kv = gqa_glob.reshape(BATCH, DEPTH, 2, SEQ, 2, HEAD_DIM)
    return (y_prompt, y_sample, new_na_kv, new_gqa_kv)
```

```python
import functools

import numpy as np
import jax
import jax.numpy as jnp
from jax import lax
from jax.experimental import pallas as pl
from jax.experimental.pallas import tpu as pltpu

F32 = jnp.float32
BF16 = jnp.bfloat16
I32 = jnp.int32
U32 = jnp.uint32

D_MODEL = 1024
BATCH = 32
SEQ = 256
DEPTH = 4
DEC_BATCH = 2
DEC_SEQ = 1024
PAST_LEN = 256
GRID_W = 64
GRID_ROWS = DEC_SEQ // GRID_W
HEAD_DIM = 64
GROUP_W = 256
N_HEADS = GROUP_W // HEAD_DIM
KV_W = 128
POOL_WINDOWS = (2, 4, 8, 16)
NA_KH = 8
NA_KW = 16
N_EXPERTS = 32
TOP_K = 4
D_FF = 1024
SWIGLU_LIMIT = 7.0
SWIGLU_ALPHA = 1.702
ROPE_THETA = 10000.0
EPS = 1e-6
NEG_INF = -1e30
PROJ_W = 1792

ROWS = 256
SUB = 8
LANES = 128
ROW_TILES = D_MODEL // LANES
T_P = BATCH * SEQ
T_S = DEC_BATCH * DEC_SEQ
T_ALL = T_P + T_S
NB_P = T_P // ROWS
NB_S = T_S // ROWS
NB_TOK = NB_P + NB_S
SUB_S = DEC_SEQ // ROWS
MOE_ROWS = 512
N_ASSIGN = T_ALL * TOP_K
BLOCK_ASSIGN = ROWS * TOP_K
SEG = 16
SEG_SHIFT = 4
N_BLOCKS = -(-(N_ASSIGN + N_EXPERTS * (SEG - 1 + MOE_ROWS - 1)) // MOE_ROWS)
N_SORTED = N_BLOCKS * MOE_ROWS
STAGE_ROWS = 1792
VMEM_LIMIT = 56 * 1024 * 1024


def _dft_tables(L):
    n = np.arange(L)
    ang = 2.0 * np.pi * ((n[:, None] * n[None, :]) % L) / L
    c64 = np.arange(HEAD_DIM)
    a64 = 2.0 * np.pi * ((c64[:, None] * c64[None, :]) % HEAD_DIM) / HEAD_DIM
    bc = np.kron(np.eye(N_HEADS), np.cos(a64))
    bs = np.kron(np.eye(N_HEADS), np.sin(a64))
    return (np.cos(ang).astype(np.float32), np.sin(ang).astype(np.float32),
            bc.astype(np.float32), bs.astype(np.float32))


def _pool_tables(L):
    t = np.arange(L)
    bands = np.zeros((len(POOL_WINDOWS), L, L), np.float32)
    icnt = np.zeros((L, GROUP_W), np.float64)
    for gi, win in enumerate(POOL_WINDOWS):
        a = np.clip(t - win // 2, 0, L)
        b = np.clip(t + win // 2, 0, L)
        n = t[None, :]
        bands[gi] = ((n >= a[:, None]) & (n < b[:, None])).astype(np.float32)
        icnt[:, gi * 64:(gi + 1) * 64] = (1.0 / (b - a))[:, None]
    return bands, icnt.astype(np.float32)


def _rope_tables():
    t = np.arange(DEC_SEQ)
    row = (t // GRID_W).astype(np.float64)
    col = (t % GRID_W).astype(np.float64)
    n_freq = HEAD_DIM // 4
    inv = ROPE_THETA ** (-np.arange(n_freq, dtype=np.float64) / n_freq)
    ar = row[:, None] * inv[None, :]
    ac = col[:, None] * inv[None, :]
    cos = np.concatenate([np.cos(ar), np.cos(ar), np.cos(ac), np.cos(ac)], axis=1)
    sin = np.concatenate([-np.sin(ar), np.sin(ar), -np.sin(ac), np.sin(ac)], axis=1)
    cos = np.concatenate([np.ones((ROWS, HEAD_DIM)), cos], axis=0)
    sin = np.concatenate([np.zeros((ROWS, HEAD_DIM)), sin], axis=0)
    return (np.tile(cos, (1, N_HEADS)).astype(np.float32), np.tile(sin, (1, N_HEADS)).astype(np.float32))


def _group_mean_matrix(width):
    g = np.kron(np.eye(width // HEAD_DIM), np.ones((HEAD_DIM, HEAD_DIM)))
    return g.astype(np.float32)


def _dot(a, b):
    return jnp.dot(a, b, preferred_element_type=F32)


def _dot_nt(a, b):
    return lax.dot_general(a, b, (((1,), (1,)), ((), ())), preferred_element_type=F32)


def _split_bf16(x):
    hi = x.astype(BF16)
    lo = (x - hi.astype(F32)).astype(BF16)
    return hi, lo


def _rmsnorm_rows(x, g):
    ms = jnp.mean(x * x, axis=-1, keepdims=True)
    return x * lax.rsqrt(ms + EPS) * g


def _softmax_rows(s):
    m = jnp.max(s, axis=-1, keepdims=True)
    p = jnp.exp(s - m)
    return p, jnp.sum(p, axis=-1, keepdims=True)


def _full(shape):
    return pl.BlockSpec(shape, lambda *_: (0,) * len(shape))


def _mod_row(i):
    return jnp.where(i < NB_P, 0, 1 + (i - NB_P) // SUB_S)


def _latent_block(i):
    return jnp.maximum(i - NB_P, 0)


MOD_COLS = 512


def _mods_body(cond_ref, w_ref, b_ref, o_ref):
    c = cond_ref[...]
    s = c * (1.0 / (1.0 + jnp.exp(-c)))
    s_hi, s_lo = _split_bf16(s)
    w_hi, w_lo = _split_bf16(w_ref[...])
    o_ref[...] = _dot(s_hi, w_hi) + _dot(s_lo, w_hi) + _dot(s_hi, w_lo) + b_ref[...]


def _mods_call(cond8, w_mod, b_mod3):
    n_col = 6 * D_MODEL // MOD_COLS
    return pl.pallas_call(
        _mods_body,
        grid=(DEPTH, n_col),
        in_specs=[
            pl.BlockSpec((SUB, D_MODEL), lambda l, j: (0, 0)),
            pl.BlockSpec((None, D_MODEL, MOD_COLS), lambda l, j: (l, 0, j)),
            pl.BlockSpec((None, 1, MOD_COLS), lambda l, j: (l, 0, j)),
        ],
        out_specs=pl.BlockSpec((None, SUB, MOD_COLS), lambda l, j: (l, 0, j)),
        out_shape=jax.ShapeDtypeStruct((DEPTH, SUB, 6 * D_MODEL), F32),
        name="adaln_mods",
    )(cond8, w_mod, b_mod3)


def _head_norm(t, g, gm):
    hi, lo = _split_bf16(t * t)
    msq = (_dot(hi, gm) + _dot(lo, gm)) * (1.0 / HEAD_DIM)
    return t * lax.rsqrt(msq + EPS) * g


def _rope(t, cos, sin):
    w = t.shape[1]
    lane = lax.broadcasted_iota(I32, t.shape, 1)
    first = (lane % 32) < 16
    swapped = jnp.where(first, pltpu.roll(t, w - 16, 1), pltpu.roll(t, 16, 1))
    return t * cos + swapped * sin


def _k1_body(x_ref, mod_ref, g1_ref, win_ref, qg_ref, kg_ref, gm_ref, cos_ref, sin_ref, *rest):
    uf_ref, naq_ref, gq_ref, up_ref, nakv_ref, gkv_ref, nakv_s_ref, gkv_s_ref = rest[-8:]
    i = pl.program_id(0)
    sh1 = mod_ref[:, 0:D_MODEL]
    sc1 = mod_ref[:, D_MODEL:2 * D_MODEL]
    h = _rmsnorm_rows(x_ref[...], g1_ref[...]) * (1.0 + sc1) + sh1
    hb = h.astype(BF16)

    def proj(lo, hi):
        return _dot(hb, win_ref[:, lo:hi])

    uf_ref[...] = proj(0, 256).astype(BF16)
    naq_ref[...] = proj(256, 512).astype(BF16)
    na_k = proj(512, 768)
    na_v = proj(768, 1024)
    gq = _head_norm(proj(1024, 1280), qg_ref[...], gm_ref[...])
    gk = _head_norm(proj(1280, 1408), kg_ref[:, 0:KV_W], gm_ref[0:KV_W, 0:KV_W])
    gv = proj(1408, 1536)
    up_ref[...] = proj(1536, 1792)
    gq = _rope(gq, cos_ref[...], sin_ref[...])
    gk = _rope(gk, cos_ref[:, 0:KV_W], sin_ref[:, 0:KV_W])
    gq_ref[...] = gq.astype(BF16)

    @pl.when(i < NB_P)
    def _():
        nakv_ref[0] = na_k
        nakv_ref[1] = na_v
        gkv_ref[0] = gk
        gkv_ref[1] = gv

    @pl.when(i >= NB_P)
    def _():
        nakv_s_ref[0] = na_k.astype(BF16)
        nakv_s_ref[1] = na_v.astype(BF16)
        gkv_s_ref[0] = gk.astype(BF16)
        gkv_s_ref[1] = gv.astype(BF16)


def _k1_call(layer, x, mods_l, g1, w_in_bf, qg, kg, gm, cos, sin, na_glob, gqa_glob):
    rope_blk = lambda i: (jnp.where(i < NB_P, 0, 1 + (i - NB_P) % SUB_S), 0)
    ctx_blk = lambda i: (jnp.minimum(i, NB_P - 1), layer, 0, 0, 0)
    lat_blk = lambda i: (_latent_block(i) // SUB_S, 0, _latent_block(i) % SUB_S, 0)
    row256 = pl.BlockSpec((ROWS, GROUP_W), lambda i: (i, 0))
    in_specs = [
        pl.BlockSpec((ROWS, D_MODEL), lambda i: (i, 0)),
        pl.BlockSpec((None, 1, 6 * D_MODEL), lambda i: (_mod_row(i), 0, 0)),
        _full((1, D_MODEL)),
        _full((D_MODEL, PROJ_W)),
        _full((1, GROUP_W)),
        _full((1, GROUP_W)),
        _full((GROUP_W, GROUP_W)),
        pl.BlockSpec((ROWS, GROUP_W), rope_blk),
        pl.BlockSpec((ROWS, GROUP_W), rope_blk),
    ]
    in_specs += [pl.BlockSpec(memory_space=pl.ANY)] * 2
    args = [x, mods_l, g1, w_in_bf, qg, kg, gm, cos, sin, na_glob, gqa_glob]
    return pl.pallas_call(
        _k1_body,
        grid=(NB_TOK,),
        in_specs=in_specs,
        out_specs=[
            row256, row256, row256, row256,
            pl.BlockSpec((None, None, 2, ROWS, GROUP_W), ctx_blk),
            pl.BlockSpec((None, None, 2, ROWS, KV_W), ctx_blk),
            pl.BlockSpec((None, 2, ROWS, GROUP_W), lat_blk),
            pl.BlockSpec((None, 2, ROWS, KV_W), lat_blk),
        ],
        out_shape=[
            jax.ShapeDtypeStruct((T_ALL, GROUP_W), BF16),
            jax.ShapeDtypeStruct((T_ALL, GROUP_W), BF16),
            jax.ShapeDtypeStruct((T_ALL, GROUP_W), BF16),
            jax.ShapeDtypeStruct((T_ALL, GROUP_W), F32),
            jax.ShapeDtypeStruct((BATCH, DEPTH, 2, SEQ, GROUP_W), F32),
            jax.ShapeDtypeStruct((BATCH, DEPTH, 2, SEQ, KV_W), F32),
            jax.ShapeDtypeStruct((DEC_BATCH, 2, DEC_SEQ, GROUP_W), BF16),
            jax.ShapeDtypeStruct((DEC_BATCH, 2, DEC_SEQ, KV_W), BF16),
        ],
        input_output_aliases={9: 4, 10: 5},
        compiler_params=pltpu.CompilerParams(vmem_limit_bytes=VMEM_LIMIT),
        name="k1_norm_proj",
    )(*args)


def _linmix_body(seq, uf_ref, up_ref, bc_ref, bs_ref, cl_ref, sl_ref, wf_ref, band_ref, icnt_ref,
                 wp_ref, ps_ref, a_ref, d_ref):
    u = uf_ref[...]
    t1 = _dot(u, bc_ref[...]).astype(BF16)
    t2 = _dot(u, bs_ref[...]).astype(BF16)
    f = (_dot(cl_ref[...], t1) - _dot(sl_ref[...], t2)) * (1.0 / float(np.sqrt(seq * HEAD_DIM)))
    a_ref[...] = _dot(f.astype(BF16), wf_ref[...]).astype(BF16)

    up = up_ref[...]
    up_hi, up_lo = _split_bf16(up)
    group = lax.broadcasted_iota(I32, up.shape, 1) // HEAD_DIM
    win_sum = jnp.zeros_like(up)
    for gi in range(len(POOL_WINDOWS)):
        s = _dot(band_ref[gi], up_hi) + _dot(band_ref[gi], up_lo)
        win_sum = jnp.where(group == gi, s, win_sum)
    delta = win_sum * icnt_ref[...] - up
    d_ref[...] = (_dot(delta.astype(BF16), wp_ref[...]) * ps_ref[...]).astype(BF16)


def _linmix_call(uf, up, seq, first_blk, n_blk, tabs, wf_bf, wp_bd_bf, ps, name):
    bc, bs, cl, sl, bands, icnt = tabs
    blk = pl.BlockSpec((seq, GROUP_W), lambda i: (first_blk + i, 0))
    oblk = pl.BlockSpec((seq, GROUP_W), lambda i: (i, 0))
    return pl.pallas_call(
        functools.partial(_linmix_body, seq),
        grid=(n_blk,),
        in_specs=[
            blk, blk,
            _full((GROUP_W, GROUP_W)), _full((GROUP_W, GROUP_W)),
            _full((seq, seq)), _full((seq, seq)),
            _full((GROUP_W, GROUP_W)),
            _full((len(POOL_WINDOWS), seq, seq)),
            _full((seq, GROUP_W)),
            _full((GROUP_W, GROUP_W)),
            _full((1, GROUP_W)),
        ],
        out_specs=[oblk, oblk],
        out_shape=[jax.ShapeDtypeStruct((n_blk * seq, GROUP_W), BF16)] * 2,
        compiler_params=pltpu.CompilerParams(vmem_limit_bytes=VMEM_LIMIT),
        name=name,
    )(uf, up, bc, bs, cl, sl, wf_bf, bands, icnt, wp_bd_bf, ps)


def _attend(q, k, v):
    s = _dot_nt(q, k) * (HEAD_DIM ** -0.5)
    p, l = _softmax_rows(s)
    return _dot(p.astype(BF16), v) / l


def _attn_p_body(naq_ref, nakv_ref, gq_ref, gkv_ref, b_ref, c_ref):
    for h in range(N_HEADS):
        sl = slice(h * HEAD_DIM, (h + 1) * HEAD_DIM)
        k = nakv_ref[0, :, sl].astype(BF16)
        v = nakv_ref[1, :, sl].astype(BF16)
        b_ref[:, sl] = _attend(naq_ref[:, sl], k, v).astype(BF16)
    for h in range(N_HEADS):
        sl = slice(h * HEAD_DIM, (h + 1) * HEAD_DIM)
        kvh = h // 2
        ksl = slice(kvh * HEAD_DIM, (kvh + 1) * HEAD_DIM)
        k = gkv_ref[0, :, ksl].astype(BF16)
        v = gkv_ref[1, :, ksl].astype(BF16)
        c_ref[:, sl] = _attend(gq_ref[:, sl], k, v).astype(BF16)


def _attn_p_call(layer, naq, na_glob, gq, gqa_glob):
    row = pl.BlockSpec((SEQ, GROUP_W), lambda i: (i, 0))
    return pl.pallas_call(
        _attn_p_body,
        grid=(BATCH,),
        in_specs=[
            row,
            pl.BlockSpec((None, None, 2, SEQ, GROUP_W), lambda i: (i, layer, 0, 0, 0)),
            row,
            pl.BlockSpec((None, None, 2, SEQ, KV_W), lambda i: (i, layer, 0, 0, 0)),
        ],
        out_specs=[row, row],
        out_shape=[jax.ShapeDtypeStruct((T_P, GROUP_W), BF16)] * 2,
        name="attn_ctx",
    )(naq, na_glob, gq, gqa_glob)


N_DR = 2 * NA_KH - 1
N_DC = 2 * NA_KW - 1


def _nabias_body(rpb_ref, o_ref):
    lh = pl.program_id(0)
    q = lax.broadcasted_iota(I32, (GRID_W, LANES), 0)
    j = lax.broadcasted_iota(I32, (GRID_W, LANES), 1)
    k = j % GRID_W
    second = j >= GRID_W
    c0 = jnp.clip(q - NA_KW // 2, 0, GRID_W - NA_KW)
    valid = (k >= c0) & (k < c0 + NA_KW)
    diff = k - q + (NA_KW - 1)
    for dr in range(N_DR - 1):
        acc = jnp.zeros((GRID_W, LANES), F32)
        for dc in range(N_DC):
            val = jnp.where(second, rpb_ref[lh, (dr + 1) * N_DC + dc], rpb_ref[lh, dr * N_DC + dc])
            acc = jnp.where(diff == dc, val, acc)
        o_ref[dr] = jnp.where(valid, acc, NEG_INF)


def _nabias_call(rpb_flat):
    n = rpb_flat.shape[0]
    return pl.pallas_call(
        _nabias_body,
        grid=(n,),
        in_specs=[pl.BlockSpec(memory_space=pltpu.SMEM)],
        out_specs=pl.BlockSpec((None, N_DR - 1, GRID_W, LANES), lambda i: (i, 0, 0, 0)),
        out_shape=jax.ShapeDtypeStruct((n, N_DR - 1, GRID_W, LANES), F32),
        name="na_bias_tiles",
    )(rpb_flat)


N_WIN = NA_KH * GRID_W


def _na_s_body(q_ref, kv_ref, ckv_ref, bias_ref, o_ref):
    r = pl.program_id(1)
    r0 = jnp.clip(r - NA_KH // 2, 0, GRID_ROWS - NA_KH)
    start = pl.multiple_of(r0 * GRID_W, GRID_W)
    dr0 = r0 - r + (NA_KH - 1)
    scale = HEAD_DIM ** -0.5
    for h in range(N_HEADS):
        sl = slice(h * HEAD_DIM, (h + 1) * HEAD_DIM)
        q = q_ref[:, sl]
        kw = kv_ref[0, pl.ds(start, N_WIN), sl]
        vw = kv_ref[1, pl.ds(start, N_WIN), sl]
        kc = ckv_ref[0, :, sl].astype(BF16)
        vc = ckv_ref[1, :, sl].astype(BF16)
        bias = jnp.concatenate([bias_ref[h, dr0 + i] for i in range(0, NA_KH, 2)], axis=1)
        s_win = _dot_nt(q, kw) * scale + bias
        s_ctx = _dot_nt(q, kc) * scale
        s = jnp.concatenate([s_win, s_ctx], axis=1)
        p, l = _softmax_rows(s)
        pb = p.astype(BF16)
        o = _dot(pb[:, 0:N_WIN], vw) + _dot(pb[:, N_WIN:], vc)
        o_ref[:, sl] = (o / l).astype(BF16)


def _na_s_call(naq, nakv_s, cache_na_l, bias_l):
    first = T_P // GRID_W
    return pl.pallas_call(
        _na_s_body,
        grid=(DEC_BATCH, GRID_ROWS),
        in_specs=[
            pl.BlockSpec((GRID_W, GROUP_W), lambda b, r: (first + b * GRID_ROWS + r, 0)),
            pl.BlockSpec((None, 2, DEC_SEQ, GROUP_W), lambda b, r: (b, 0, 0, 0)),
            pl.BlockSpec((None, 2, PAST_LEN, GROUP_W), lambda b, r: (b, 0, 0, 0)),
            _full((N_HEADS, N_DR - 1, GRID_W, LANES)),
        ],
        out_specs=pl.BlockSpec((GRID_W, GROUP_W), lambda b, r: (b * GRID_ROWS + r, 0)),
        out_shape=jax.ShapeDtypeStruct((T_S, GROUP_W), BF16),
        compiler_params=pltpu.CompilerParams(vmem_limit_bytes=VMEM_LIMIT),
        name="attn_na_latent",
    )(naq, nakv_s, cache_na_l, bias_l)


GQ_ROWS = 128


def _gqa_s_body(q_ref, kv_ref, ckv_ref, o_ref):
    scale = HEAD_DIM ** -0.5
    for h in range(N_HEADS):
        sl = slice(h * HEAD_DIM, (h + 1) * HEAD_DIM)
        kvh = h // 2
        ksl = slice(kvh * HEAD_DIM, (kvh + 1) * HEAD_DIM)
        q = q_ref[:, sl]
        kl = kv_ref[0, :, ksl]
        vl = kv_ref[1, :, ksl]
        kc = ckv_ref[0, :, ksl].astype(BF16)
        vc = ckv_ref[1, :, ksl].astype(BF16)
        s = jnp.concatenate([_dot_nt(q, kl), _dot_nt(q, kc)], axis=1) * scale
        p, l = _softmax_rows(s)
        pb = p.astype(BF16)
        o = _dot(pb[:, 0:DEC_SEQ], vl) + _dot(pb[:, DEC_SEQ:], vc)
        o_ref[:, sl] = (o / l).astype(BF16)


def _gqa_s_call(gq, gkv_s, cache_gqa_l):
    nq = DEC_SEQ // GQ_ROWS
    first = T_P // GQ_ROWS
    return pl.pallas_call(
        _gqa_s_body,
        grid=(DEC_BATCH, nq),
        in_specs=[
            pl.BlockSpec((GQ_ROWS, GROUP_W), lambda b, j: (first + b * nq + j, 0)),
            pl.BlockSpec((None, 2, DEC_SEQ, KV_W), lambda b, j: (b, 0, 0, 0)),
            pl.BlockSpec((None, 2, PAST_LEN, KV_W), lambda b, j: (b, 0, 0, 0)),
        ],
        out_specs=pl.BlockSpec((GQ_ROWS, GROUP_W), lambda b, j: (b * nq + j, 0)),
        out_shape=jax.ShapeDtypeStruct((T_S, GROUP_W), BF16),
        compiler_params=pltpu.CompilerParams(vmem_limit_bytes=VMEM_LIMIT),
        name="attn_gqa_latent",
    )(gq, gkv_s, cache_gqa_l)


def _k3_body(x_ref, ap_ref, bp_ref, cp_ref, dp_ref, as_ref, bs_ref, cs_ref, ds_ref, mod_ref, wout_ref, g2_ref,
             rwh_ref, rwl_ref, rb_ref, xmid_ref, h2_ref, te_ref, tg_ref, pr_ref, cnt_ref):
    is_ctx = pl.program_id(0) < NB_P
    acc = jnp.zeros((ROWS, D_MODEL), F32)
    for g, (p_ref, s_ref) in enumerate(((ap_ref, as_ref), (bp_ref, bs_ref), (cp_ref, cs_ref), (dp_ref, ds_ref))):
        mixed = jnp.where(is_ctx, p_ref[...], s_ref[...])
        acc += _dot(mixed, wout_ref[g * GROUP_W:(g + 1) * GROUP_W, :])
    gate1 = mod_ref[:, 2 * D_MODEL:3 * D_MODEL]
    sh2 = mod_ref[:, 3 * D_MODEL:4 * D_MODEL]
    sc2 = mod_ref[:, 4 * D_MODEL:5 * D_MODEL]
    xm = x_ref[...] + gate1 * acc
    xmid_ref[...] = xm
    h2 = _rmsnorm_rows(xm, g2_ref[...]) * (1.0 + sc2) + sh2
    hi, lo = _split_bf16(h2)
    h2_ref[...] = hi
    logits = _dot(hi, rwh_ref[...]) + _dot(lo, rwh_ref[...]) + _dot(hi, rwl_ref[...]) + rb_ref[...]

    lane = lax.broadcasted_iota(I32, logits.shape, 1)
    cur = logits
    vals, idxs = [], []
    for _ in range(TOP_K):
        m = jnp.max(cur, axis=-1, keepdims=True)
        idx = jnp.min(jnp.where(cur == m, lane, LANES), axis=-1, keepdims=True)
        vals.append(m)
        idxs.append(idx)
        cur = jnp.where(lane == idx, -jnp.inf, cur)
    exps = [jnp.exp(v - vals[0]) for v in vals]
    tot = exps[0] + exps[1] + exps[2] + exps[3]
    te = jnp.zeros(logits.shape, I32)
    tg = jnp.zeros(logits.shape, F32)
    for k in range(TOP_K):
        te = jnp.where(lane == k, idxs[k], te)
        tg = jnp.where(lane == k, exps[k] / tot, tg)
    te_ref[...] = te
    tg_ref[...] = tg

    member = jnp.zeros(logits.shape, F32)
    for k in range(TOP_K):
        member = jnp.where(lane == idxs[k], 1.0, member)
    rr = lax.broadcasted_iota(I32, (ROWS, ROWS), 0)
    cc = lax.broadcasted_iota(I32, (ROWS, ROWS), 1)
    earlier = jnp.where(cc < rr, 1.0, 0.0).astype(BF16)
    prefix = _dot(earlier, member.astype(BF16))
    cnt = jnp.sum(member, axis=0, keepdims=True)
    er = lax.broadcasted_iota(I32, (LANES, LANES), 0)
    ec = lax.broadcasted_iota(I32, (LANES, LANES), 1)
    lower = jnp.where(er < ec, 1.0, 0.0).astype(BF16)
    loc = _dot(jnp.broadcast_to(cnt, (SUB, LANES)).astype(BF16), lower)[0:1, :]
    pr = jnp.zeros(logits.shape, F32)
    for k in range(TOP_K):
        hit = lane == idxs[k]
        rank = jnp.sum(jnp.where(hit, prefix, 0.0), axis=-1, keepdims=True)
        start = jnp.sum(jnp.where(hit, loc, 0.0), axis=-1, keepdims=True)
        pr = jnp.where(lane == k, start + rank, pr)
        pr = jnp.where(lane == TOP_K + k, rank, pr)
    pr_ref[...] = pr.astype(I32)
    cnt_ref[...] = cnt.astype(I32)


def _k3_call(x, mix_p, mix_s, mods_l, w_out_bf, g2, rwh, rwl, rb):
    row_p = pl.BlockSpec((ROWS, GROUP_W), lambda i: (jnp.minimum(i, NB_P - 1), 0))
    row_s = pl.BlockSpec((ROWS, GROUP_W), lambda i: (_latent_block(i), 0))
    rowd = pl.BlockSpec((ROWS, D_MODEL), lambda i: (i, 0))
    row128 = pl.BlockSpec((ROWS, LANES), lambda i: (i, 0))
    return pl.pallas_call(
        _k3_body,
        grid=(NB_TOK,),
        in_specs=[
            rowd, row_p, row_p, row_p, row_p, row_s, row_s, row_s, row_s,
            pl.BlockSpec((None, 1, 6 * D_MODEL), lambda i: (_mod_row(i), 0, 0)),
            _full((D_MODEL, D_MODEL)),
            _full((1, D_MODEL)),
            _full((D_MODEL, LANES)), _full((D_MODEL, LANES)), _full((1, LANES)),
        ],
        out_specs=[rowd, rowd, row128, row128, row128,
                   pl.BlockSpec((None, 1, LANES), lambda i: (i, 0, 0))],
        out_shape=[
            jax.ShapeDtypeStruct((T_ALL, D_MODEL), F32),
            jax.ShapeDtypeStruct((T_ALL, D_MODEL), BF16),
            jax.ShapeDtypeStruct((T_ALL, LANES), I32),
            jax.ShapeDtypeStruct((T_ALL, LANES), F32),
            jax.ShapeDtypeStruct((T_ALL, LANES), I32),
            jax.ShapeDtypeStruct((NB_TOK, 1, LANES), I32),
        ],
        compiler_params=pltpu.CompilerParams(vmem_limit_bytes=VMEM_LIMIT),
        name="k3_out_router",
    )(x, *mix_p, *mix_s, mods_l, w_out_bf, g2, rwh, rwl, rb)


TAB_CNT, TAB_LOC, TAB_BASE, TAB_LSTAGE, TAB_SOFF, TAB_TOTAL, TAB_NCHUNK = range(7)


def _routing_tables(cnt):
    c = cnt.reshape(NB_TOK, LANES)[:, :N_EXPERTS]
    counts = jnp.sum(c, axis=0)
    padded = (counts + (SEG - 1) + (MOE_ROWS - 1)) // MOE_ROWS * MOE_ROWS
    pend = jnp.cumsum(padded)
    pstart = pend - padded
    base = pstart[None, :] + jnp.cumsum(c, axis=0) - c
    loc = jnp.cumsum(c, axis=1) - c
    off = base % SUB
    nch = jnp.where(c > 0, (off + c + SEG - 1) // SEG, 0)
    lstage = (jnp.cumsum(nch, axis=1) - nch) * SEG
    soff = lstage + off
    total = jnp.sum(nch, axis=1, keepdims=True) * SEG

    def pad(t):
        return jnp.zeros((NB_TOK, LANES), I32).at[:, :t.shape[1]].set(t.astype(I32))

    tab = jnp.stack([pad(c), pad(loc), pad(base), pad(lstage), pad(soff), pad(total), pad(nch),
                     jnp.zeros((NB_TOK, LANES), I32)], axis=1)
    ztab = jnp.zeros((SUB, LANES), I32).at[0, :N_EXPERTS].set((pstart + counts).astype(I32))
    ztab = ztab.at[1, :N_EXPERTS].set(pend.astype(I32))
    ztab = ztab.at[2, 0].set((pend[-1] // MOE_ROWS).astype(I32))
    first_blk = (pstart // MOE_ROWS).astype(I32)
    n_blk = (padded // MOE_ROWS).astype(I32)
    n_used = (pend[-1] // MOE_ROWS).astype(I32).reshape(1)
    return tab, ztab, first_blk, n_blk, counts.astype(I32), n_used


def _dispatch_body(tab_ref, ztab_ref, h2_ref, pr_ref, xs_ref, stage0, stage1, zbuf, pending, sems):
    b = pl.program_id(0)
    sem = sems.at[0]

    def rows_copy(src, src_row, dst_row, n_rows, sem=sem):
        return pltpu.make_async_copy(
            src.at[pl.ds(pl.multiple_of(src_row * SUB, SUB), n_rows * SUB)],
            xs_ref.at[pl.ds(pl.multiple_of(dst_row * SUB, SUB), n_rows * SUB)], sem)

    def seg_copy(src, src_row, dst_row, sem=sem):
        return rows_copy(src, src_row, dst_row, SEG, sem)

    def drain(n, sem=sem):
        def body(j, carry):
            seg_copy(zbuf, 0, 0, sem).wait()
            return carry
        lax.fori_loop(0, n, body, 0)

    @pl.when(b == 0)
    def _():
        zbuf[...] = jnp.zeros(zbuf.shape, F32)
        stage0[pl.ds(BLOCK_ASSIGN * SUB, SEG * SUB), :] = jnp.zeros((SEG * SUB, LANES), F32)
        stage1[pl.ds(BLOCK_ASSIGN * SUB, SEG * SUB), :] = jnp.zeros((SEG * SUB, LANES), F32)

        def per_expert(e, carry):
            n_seg, n_row = carry
            end = ztab_ref[0, e]
            length = ztab_ref[1, e] - end
            n = lax.shift_right_logical(length, SEG_SHIFT)
            rem = length & (SEG - 1)

            def one(j, c):
                seg_copy(zbuf, 0, end + j * SEG).start()
                return c
            lax.fori_loop(0, n, one, 0)

            def one_row(j, c):
                rows_copy(zbuf, 0, end + n * SEG + j, 1).start()
                return c
            lax.fori_loop(0, rem, one_row, 0)
            return n_seg + n, n_row + rem
        n_seg, n_row = lax.fori_loop(0, N_EXPERTS, per_expert, (0, 0))
        drain(n_seg)

        def row_wait(j, c):
            rows_copy(zbuf, 0, 0, 1).wait()
            return c
        lax.fori_loop(0, n_row, row_wait, 0)

        first_free = ztab_ref[2, 0]

        def fill(j, c):
            rows_copy(zbuf, 0, (first_free + j) * MOE_ROWS, MOE_ROWS).start()
            return c
        lax.fori_loop(0, N_BLOCKS - first_free, fill, 0)

        def fill_wait(j, c):
            rows_copy(zbuf, 0, 0, MOE_ROWS).wait()
            return c
        lax.fori_loop(0, N_BLOCKS - first_free, fill_wait, 0)

    pos_t = pr_ref[...].astype(F32).T
    h2 = h2_ref[...]

    def sort(stage):
        for c in range(BLOCK_ASSIGN // ROWS):
            p = (lax.broadcasted_iota(I32, (ROWS, ROWS), 0) + c * ROWS).astype(F32)
            sel = p == pos_t[0:1, :]
            for k in range(1, TOP_K):
                sel = sel | (p == pos_t[k:k + 1, :])
            rows = _dot(jnp.where(sel, 1.0, 0.0).astype(BF16), h2)
            for s in range(ROW_TILES):
                stage[pl.ds(c * ROWS * SUB + s, ROWS, stride=SUB), :] = rows[:, s * LANES:(s + 1) * LANES]

    def send(stage, sem):
        def per_expert(e, tot):
            loc = tab_ref[TAB_LOC, e]
            base = tab_ref[TAB_BASE, e]
            n = lax.shift_right_logical(tab_ref[TAB_CNT, e] + (SEG - 1), SEG_SHIFT)

            def one(j, carry):
                seg_copy(stage, loc + j * SEG, base + j * SEG, sem).start()
                return carry
            lax.fori_loop(0, n, one, 0)
            return tot + n
        return lax.fori_loop(0, N_EXPERTS, per_expert, 0)

    for slot, stage in enumerate((stage0, stage1)):
        @pl.when((b & 1) == slot)
        def _(stage=stage, slot=slot):
            sort(stage)

            @pl.when(b > 0)
            def _():
                drain(pending[0], sems.at[1 - slot])
            n_sent = send(stage, sems.at[slot])
            pending[0] = n_sent

            @pl.when(b == NB_TOK - 1)
            def _():
                drain(n_sent, sems.at[slot])


def _dispatch_call(tab, ztab, h2, pr):
    return pl.pallas_call(
        _dispatch_body,
        grid=(NB_TOK,),
        in_specs=[
            pl.BlockSpec((None, SUB, LANES), lambda i: (i, 0, 0), memory_space=pltpu.SMEM),
            pl.BlockSpec(memory_space=pltpu.SMEM),
            pl.BlockSpec((ROWS, D_MODEL), lambda i: (i, 0)),
            pl.BlockSpec((ROWS, LANES), lambda i: (i, 0)),
        ],
        out_specs=pl.BlockSpec(memory_space=pl.ANY),
        out_shape=jax.ShapeDtypeStruct((N_SORTED * ROW_TILES, LANES), F32),
        scratch_shapes=[
            pltpu.VMEM(((BLOCK_ASSIGN + SEG) * SUB, LANES), F32),
            pltpu.VMEM(((BLOCK_ASSIGN + SEG) * SUB, LANES), F32),
            pltpu.VMEM((MOE_ROWS * SUB, LANES), F32),
            pltpu.SMEM((1,), I32),
            pltpu.SemaphoreType.DMA((2,)),
        ],
        compiler_params=pltpu.CompilerParams(vmem_limit_bytes=VMEM_LIMIT),
        name="moe_dispatch",
    )(tab, ztab, h2, pr)


HALF_FF = D_FF // 2


def _moe_body(fb_ref, nb_ref, nv_ref, nu_ref, xs_ref, wgu_ref, bgu_ref, wd_ref, bd_ref, yp_ref,
              wgu_bf, wd_bf, wd_scr, xb_scr, act_scr, xbuf0, xbuf1, ybuf0, ybuf1, xsems, ysems):
    e = pl.program_id(0)
    first = fb_ref[e]
    n_blk = nb_ref[e]
    n_used = nu_ref[0]
    xbufs = (xbuf0, xbuf1)
    ybufs = (ybuf0, ybuf1)

    def x_copy(g, slot):
        src = xs_ref.at[pl.ds(pl.multiple_of(g * (MOE_ROWS * SUB), MOE_ROWS * SUB), MOE_ROWS * SUB)]
        return pltpu.make_async_copy(src, xbufs[slot], xsems.at[slot])

    def y_copy(g, slot):
        dst = yp_ref.at[pl.ds(pl.multiple_of(g * MOE_ROWS, MOE_ROWS), MOE_ROWS)]
        return pltpu.make_async_copy(ybufs[slot], dst, ysems.at[slot])

    @pl.when(e == 0)
    def _():
        x_copy(0, 0).start()

    @pl.when(n_blk > 0)
    def _():
        for c in range(4):
            cs = slice(c * 512, (c + 1) * 512)
            wgu_bf[:, cs] = wgu_ref[:, cs].astype(BF16)
        for c in range(ROW_TILES):
            cs = slice(c * LANES, (c + 1) * LANES)
            wd_scr[pl.ds(0, HALF_FF, stride=2), :] = wd_ref[0:HALF_FF, cs]
            wd_scr[pl.ds(1, HALF_FF, stride=2), :] = wd_ref[HALF_FF:D_FF, cs]
            wd_bf[:, cs] = wd_scr[...].astype(BF16)

    def ffn(x_ref, y_ref, m):
        for s in range(ROW_TILES):
            xb_scr[0:m, s * LANES:(s + 1) * LANES] = x_ref[pl.ds(s, m, stride=ROW_TILES), :].astype(BF16)
        even = (lax.broadcasted_iota(I32, (m, LANES), 1) % 2) == 0
        for c in range(D_FF // GROUP_W):
            ca = slice(c * GROUP_W, (c + 1) * GROUP_W)
            cb = slice(D_FF + c * GROUP_W, D_FF + (c + 1) * GROUP_W)
            ha_full = _dot(xb_scr[0:m, :], wgu_bf[:, ca]) + bgu_ref[:, ca]
            hb_full = _dot(xb_scr[0:m, :], wgu_bf[:, cb]) + bgu_ref[:, cb]
            for h in range(GROUP_W // LANES):
                ha = ha_full[:, h * LANES:(h + 1) * LANES]
                hb = hb_full[:, h * LANES:(h + 1) * LANES]
                gate = jnp.where(even, ha, pltpu.roll(hb, 1, 1))
                up = jnp.where(even, pltpu.roll(ha, LANES - 1, 1), hb)
                gate = jnp.minimum(gate, SWIGLU_LIMIT)
                up = jnp.clip(up, -SWIGLU_LIMIT, SWIGLU_LIMIT)
                glu = gate * (1.0 / (1.0 + jnp.exp(-gate * SWIGLU_ALPHA)))
                lo = c * GROUP_W + h * LANES
                act_scr[0:m, lo:lo + LANES] = ((up + 1.0) * glu).astype(BF16)
        for c in range(D_MODEL // GROUP_W):
            cs = slice(c * GROUP_W, (c + 1) * GROUP_W)
            y_ref[0:m, cs] = _dot(act_scr[0:m, :], wd_bf[:, cs]) + bd_ref[:, cs]
        if m < MOE_ROWS:
            y_ref[m:MOE_ROWS, :] = jnp.zeros((MOE_ROWS - m, D_MODEL), F32)

    def block(j, carry):
        g = first + j
        for slot in range(2):
            @pl.when((g & 1) == slot)
            def _(slot=slot):
                x_copy(g, slot).wait()

                @pl.when(g + 1 < n_used)
                def _():
                    x_copy(g + 1, 1 - slot).start()
                @pl.when(g >= 2)
                def _():
                    y_copy(g - 2, slot).wait()
                valid = nv_ref[e] - j * MOE_ROWS
                n_groups = 8
                group = MOE_ROWS // n_groups
                for q in range(1, n_groups + 1):
                    above = valid > (q - 1) * group if q > 1 else True
                    below = valid <= q * group if q < n_groups else True

                    @pl.when(jnp.logical_and(above, below))
                    def _(q=q):
                        ffn(xbufs[slot], ybufs[slot], q * group)
                y_copy(g, slot).start()
        return carry
    lax.fori_loop(0, n_blk, block, 0)

    @pl.when(e == N_EXPERTS - 1)
    def _():
        for slot in range(2):
            @pl.when(((n_used - 1) & 1) == slot)
            def _(slot=slot):
                @pl.when(n_used >= 2)
                def _():
                    y_copy(n_used - 2, 1 - slot).wait()
                y_copy(n_used - 1, slot).wait()
        ybuf0[...] = jnp.zeros(ybuf0.shape, F32)

        def fill(j, carry):
            y_copy(n_used + j, 0).start()
            return carry
        lax.fori_loop(0, N_BLOCKS - n_used, fill, 0)

        def fill_wait(j, carry):
            y_copy(0, 0).wait()
            return carry
        lax.fori_loop(0, N_BLOCKS - n_used, fill_wait, 0)


def _moe_call(layer, first_blk, n_blk, n_valid, n_used, xs, w_gu, b_gu4, w_down, b_down4):
    wmap = lambda e, fb, nb, nv, nu: (layer, e, 0, 0)
    grid_spec = pltpu.PrefetchScalarGridSpec(
        num_scalar_prefetch=4,
        grid=(N_EXPERTS,),
        in_specs=[
            pl.BlockSpec(memory_space=pl.ANY),
            pl.BlockSpec((None, None, D_MODEL, 2 * D_FF), wmap),
            pl.BlockSpec((None, None, 1, 2 * D_FF), wmap),
            pl.BlockSpec((None, None, D_FF, D_MODEL), wmap),
            pl.BlockSpec((None, None, 1, D_MODEL), wmap),
        ],
        out_specs=pl.BlockSpec(memory_space=pl.ANY),
        scratch_shapes=[
            pltpu.VMEM((D_MODEL, 2 * D_FF), BF16),
            pltpu.VMEM((D_FF, D_MODEL), BF16),
            pltpu.VMEM((D_FF, LANES), F32),
            pltpu.VMEM((MOE_ROWS, D_MODEL), BF16),
            pltpu.VMEM((MOE_ROWS, D_FF), BF16),
            pltpu.VMEM((MOE_ROWS * SUB, LANES), F32),
            pltpu.VMEM((MOE_ROWS * SUB, LANES), F32),
            pltpu.VMEM((MOE_ROWS, D_MODEL), F32),
            pltpu.VMEM((MOE_ROWS, D_MODEL), F32),
            pltpu.SemaphoreType.DMA((2,)),
            pltpu.SemaphoreType.DMA((2,)),
        ],
    )
    return pl.pallas_call(
        _moe_body,
        grid_spec=grid_spec,
        out_shape=jax.ShapeDtypeStruct((N_SORTED, D_MODEL), F32),
        compiler_params=pltpu.CompilerParams(vmem_limit_bytes=VMEM_LIMIT),
        name="moe_experts",
    )(first_blk, n_blk, n_valid, n_used, xs, w_gu, b_gu4, w_down, b_down4)


def _combine_body(final, tab_ref, tabn_ref, tabv_ref, yp_ref, te_ref, tg_ref, pr_ref, xmid_ref, mod_ref, fg_ref,
                  *rest):
    outs = rest[:-4]
    stage0, stage1, acc, sems = rest[-4:]
    b = pl.program_id(0)
    stages = (stage0, stage1)

    def chunk_copy(src_row, dst_row, slot):
        return pltpu.make_async_copy(
            yp_ref.at[pl.ds(pl.multiple_of(src_row, SUB), SEG)],
            stages[slot].at[pl.ds(pl.multiple_of(dst_row, SUB), SEG)], sems.at[slot])

    def fetch(tab, slot):
        def per_expert(e, carry):
            base = tab[TAB_BASE, e]
            off = base & (SUB - 1)
            lstage = tab[TAB_LSTAGE, e]

            def one(j, c):
                chunk_copy(base - off + j * SEG, lstage + j * SEG, slot).start()
                return c
            lax.fori_loop(0, tab[TAB_NCHUNK, e], one, 0)
            return carry
        lax.fori_loop(0, N_EXPERTS, per_expert, 0)

        total = tab[TAB_TOTAL, 0]

        def zero_tail(j, c):
            stages[slot][pl.ds(pl.multiple_of(total + j * SEG, SEG), SEG), :] = jnp.zeros((SEG, D_MODEL), F32)
            return c
        lax.fori_loop(0, lax.shift_right_logical(STAGE_ROWS - total, SEG_SHIFT), zero_tail, 0)

    @pl.when(b == 0)
    def _():
        fetch(tab_ref, 0)

    for slot in range(2):
        @pl.when(((b & 1) == slot) & (b + 1 < NB_TOK))
        def _(slot=slot):
            fetch(tabn_ref, 1 - slot)

    total = tab_ref[TAB_TOTAL, 0]
    for slot in range(2):
        @pl.when((b & 1) == slot)
        def _(slot=slot):
            def drain(j, c):
                chunk_copy(0, 0, slot).wait()
                return c
            lax.fori_loop(0, lax.shift_right_logical(total, SEG_SHIFT), drain, 0)

    lane = lax.broadcasted_iota(I32, (ROWS, LANES), 1)
    te = te_ref[...]
    pr = pr_ref[...]
    soff = tabv_ref[TAB_SOFF:TAB_SOFF + 1, :].astype(F32)
    packed = jnp.where((lane >= TOP_K) & (lane < 2 * TOP_K), pltpu.roll(tg_ref[...], TOP_K, 1), 0.0)
    sp = []
    for k in range(TOP_K):
        seg_start = jnp.sum(jnp.where(lane == te[:, k:k + 1], soff, 0.0), axis=-1, keepdims=True)
        spk = seg_start + pr[:, TOP_K + k:TOP_K + k + 1].astype(F32)
        sp.append(spk)
        packed = jnp.where(lane == k, spk, packed)
    packed_t = packed.T

    def unsort(stage, c):
        pcol = (lax.broadcasted_iota(I32, (ROWS, ROWS), 1) + c * ROWS).astype(F32)
        sel = pcol == sp[0]
        for k in range(1, TOP_K):
            sel = sel | (pcol == sp[k])
        prow = (lax.broadcasted_iota(I32, (ROWS, ROWS), 0) + c * ROWS).astype(F32)
        gmat = jnp.zeros((ROWS, ROWS), F32)
        for k in range(TOP_K):
            gmat = jnp.where(prow == packed_t[k:k + 1, :], packed_t[TOP_K + k:TOP_K + k + 1, :], gmat)
        row_gate = jnp.sum(gmat, axis=-1, keepdims=True)
        rows = (stage[c * ROWS:(c + 1) * ROWS, :] * row_gate).astype(BF16)
        return _dot(jnp.where(sel, 1.0, 0.0).astype(BF16), rows)

    n_always = BLOCK_ASSIGN // ROWS
    for slot in range(2):
        @pl.when((b & 1) == slot)
        def _(slot=slot):
            out = unsort(stages[slot], 0)
            for c in range(1, n_always):
                out += unsort(stages[slot], c)
            acc[...] = out
            for c in range(n_always, STAGE_ROWS // ROWS):
                @pl.when(c * ROWS < total)
                def _(c=c):
                    acc[...] += unsort(stages[slot], c)

    gate2 = mod_ref[:, 5 * D_MODEL:6 * D_MODEL]
    x = xmid_ref[...] + gate2 * acc[...]
    if final:
        x = _rmsnorm_rows(x, fg_ref[...])
        out_ctx, out_lat = outs

        @pl.when(b < NB_P)
        def _():
            out_ctx[...] = x

        @pl.when(b >= NB_P)
        def _():
            out_lat[...] = x
    else:
        outs[0][...] = x


def _combine_call(tab, yp, te, tg, pr, xmid, mods_l, final_g, final):
    rowd = pl.BlockSpec((ROWS, D_MODEL), lambda i: (i, 0))
    row128 = pl.BlockSpec((ROWS, LANES), lambda i: (i, 0))
    if final:
        out_specs = [pl.BlockSpec((ROWS, D_MODEL), lambda i: (jnp.minimum(i, NB_P - 1), 0)),
                     pl.BlockSpec((ROWS, D_MODEL), lambda i: (_latent_block(i), 0))]
        out_shape = [jax.ShapeDtypeStruct((T_P, D_MODEL), F32), jax.ShapeDtypeStruct((T_S, D_MODEL), F32)]
    else:
        out_specs = rowd
        out_shape = jax.ShapeDtypeStruct((T_ALL, D_MODEL), F32)
    return pl.pallas_call(
        functools.partial(_combine_body, final),
        grid=(NB_TOK,),
        in_specs=[
            pl.BlockSpec((None, SUB, LANES), lambda i: (i, 0, 0), memory_space=pltpu.SMEM),
            pl.BlockSpec((None, SUB, LANES), lambda i: (jnp.minimum(i + 1, NB_TOK - 1), 0, 0),
                         memory_space=pltpu.SMEM),
            pl.BlockSpec((None, SUB, LANES), lambda i: (i, 0, 0)),
            pl.BlockSpec(memory_space=pl.ANY),
            row128, row128, row128,
            rowd,
            pl.BlockSpec((None, 1, 6 * D_MODEL), lambda i: (_mod_row(i), 0, 0)),
            _full((1, D_MODEL)),
        ],
        out_specs=out_specs,
        out_shape=out_shape,
        scratch_shapes=[
            pltpu.VMEM((STAGE_ROWS, D_MODEL), F32),
            pltpu.VMEM((STAGE_ROWS, D_MODEL), F32),
            pltpu.VMEM((ROWS, D_MODEL), F32),
            pltpu.SemaphoreType.DMA((2,)),
        ],
        compiler_params=pltpu.CompilerParams(vmem_limit_bytes=VMEM_LIMIT),
        name="moe_combine",
    )(tab, tab, tab, yp, te, tg, pr, xmid, mods_l, final_g)


def kernel(x_prompt, x_sample, cache_na_kv, cache_gqa_kv, c, c_ctx, norm1_g, norm2_g, w_mod, b_mod,
           w_in, w_fourier, na_rpb, q_norm_g, k_norm_g, w_pool, pool_scale, w_out, router_w, router_b,
           w_gu, b_gu, w_down, b_down, final_g):
    def lin_tables(L):
        cl, sl, bc, bs = _dft_tables(L)
        bands, icnt = _pool_tables(L)
        return (jnp.asarray(bc).astype(BF16), jnp.asarray(bs).astype(BF16),
                jnp.asarray(cl).astype(BF16), jnp.asarray(sl).astype(BF16),
                jnp.asarray(bands).astype(BF16), jnp.asarray(icnt))

    tabs_p = lin_tables(SEQ)
    tabs_s = lin_tables(DEC_SEQ)
    cos_np, sin_np = _rope_tables()
    cos_t, sin_t = jnp.asarray(cos_np), jnp.asarray(sin_np)
    gm = jnp.asarray(_group_mean_matrix(GROUP_W)).astype(BF16)

    cond8 = jnp.zeros((SUB, D_MODEL), F32).at[0].set(c_ctx).at[1:1 + DEC_BATCH].set(c)
    mods = _mods_call(cond8, w_mod, b_mod.reshape(DEPTH, 1, 6 * D_MODEL))
    mods = mods.reshape(DEPTH, SUB, 1, 6 * D_MODEL)

    bias_tiles = _nabias_call(na_rpb.reshape(DEPTH * N_HEADS, N_DR * N_DC))
    bias_tiles = bias_tiles.reshape(DEPTH, N_HEADS, N_DR - 1, GRID_W, LANES)

    w_in_bf = w_in.astype(BF16)
    w_out_bf = w_out.astype(BF16)
    w_f_bf = w_fourier.astype(BF16)
    eye4 = jnp.eye(4, dtype=F32)
    w_pool_bd = jnp.einsum('lgce,gh->lgche', w_pool, eye4).reshape(DEPTH, GROUP_W, GROUP_W).astype(BF16)
    qg = jnp.tile(q_norm_g, (1, N_HEADS)).reshape(DEPTH, 1, GROUP_W)
    kg = jnp.tile(k_norm_g, (1, N_HEADS)).reshape(DEPTH, 1, GROUP_W)
    rw_pad = jnp.zeros((DEPTH, D_MODEL, LANES), F32).at[:, :, :N_EXPERTS].set(router_w)
    rw_hi = rw_pad.astype(BF16)
    rw_lo = (rw_pad - rw_hi.astype(F32)).astype(BF16)
    rb_pad = jnp.full((DEPTH, 1, LANES), NEG_INF, F32).at[:, 0, :N_EXPERTS].set(router_b)
    b_gu4 = b_gu.reshape(DEPTH, N_EXPERTS, 1, 2 * D_FF)
    b_down4 = b_down.reshape(DEPTH, N_EXPERTS, 1, D_MODEL)
    final_g2 = final_g.reshape(1, D_MODEL)
    cache_na = cache_na_kv.reshape(DEC_BATCH, DEPTH, 2, PAST_LEN, GROUP_W)
    cache_gqa = cache_gqa_kv.reshape(DEC_BATCH, DEPTH, 2, PAST_LEN, KV_W)

    x = jnp.concatenate([x_prompt.reshape(T_P, D_MODEL), x_sample.reshape(T_S, D_MODEL)], axis=0)
    na_glob = jnp.zeros((BATCH, DEPTH, 2, SEQ, GROUP_W), F32)
    gqa_glob = jnp.zeros((BATCH, DEPTH, 2, SEQ, KV_W), F32)
    for l in range(DEPTH):
        g1 = norm1_g[l].reshape(1, D_MODEL)
        g2 = norm2_g[l].reshape(1, D_MODEL)
        ps = pool_scale[l].reshape(1, GROUP_W)

        uf, naq, gq, up, na_glob, gqa_glob, nakv_s, gkv_s = _k1_call(
            l, x, mods[l], g1, w_in_bf[l], qg[l], kg[l], gm, cos_t, sin_t, na_glob, gqa_glob)

        a_p, d_p = _linmix_call(uf, up, SEQ, 0, BATCH, tabs_p, w_f_bf[l], w_pool_bd[l], ps, "linmix_ctx")
        a_s, d_s = _linmix_call(uf, up, DEC_SEQ, T_P // DEC_SEQ, DEC_BATCH, tabs_s, w_f_bf[l], w_pool_bd[l], ps,
                                "linmix_latent")
        b_p, c_p = _attn_p_call(l, naq, na_glob, gq, gqa_glob)
        b_s = _na_s_call(naq, nakv_s, cache_na[:, l], bias_tiles[l])
        c_s = _gqa_s_call(gq, gkv_s, cache_gqa[:, l])

        xmid, h2, te, tg, pr, cnt = _k3_call(x, (a_p, b_p, c_p, d_p), (a_s, b_s, c_s, d_s), mods[l], w_out_bf[l],
                                             g2, rw_hi[l], rw_lo[l], rb_pad[l])
        tab, ztab, first_blk, n_blk, n_valid, n_used = _routing_tables(cnt)
        xs = _dispatch_call(tab, ztab, h2, pr)
        yp = _moe_call(l, first_blk, n_blk, n_valid, n_used, xs, w_gu, b_gu4, w_down, b_down4)
        x = _combine_call(tab, yp, te, tg, pr, xmid, mods[l], final_g2, l == DEPTH - 1)

    y_ctx, y_lat = x
    y_prompt = y_ctx.reshape(BATCH, SEQ, D_MODEL)
    y_sample = y_lat.reshape(DEC_BATCH, DEC_SEQ, D_MODEL)
    new_na_kv = na_glob.reshape(BATCH, DEPTH, 2, SEQ, N_HEADS, HEAD_DIM)
    new_gqa_kv = gqa_glob.reshape(BATCH, DEPTH, 2, SEQ, 2, HEAD_DIM)
    return (y_prompt, y_sample, new_na_kv, new_gqa_kv)
```
